```python
import jax, jax.numpy as jnp
from jax import lax
import numpy as np

D_MODEL = 1024
BATCH = 4
SEQ = 4096
DEPTH = 2
DEC_BATCH = 32
DEC_SEQ = 8
PAST_LEN = 8192
PAGE_SIZE = 128

HEAD_DIM = 64
N_HEADS_A = 8
N_HEADS_B = 8
D_MIX = (N_HEADS_A + N_HEADS_B) * HEAD_DIM
N_IDX_HEADS = 8
D_IDX = 64
TOPK_MAX = 256
Q_BLOCK = 128
ROPE_THETA = 10000.0
CHUNK = 128
D_CG = 2 * D_MODEL
N_GROUPS_C = 8
D_FF = 2816
CONV_W = 3
N_ATTN = (DEPTH + 1) // 2
N_CMLP = DEPTH // 2
ALPHA = (2 * DEPTH) ** 0.25
BETA = (8 * DEPTH) ** -0.25
LN_EPS = 1e-5
FORGET_BIAS = 3.0
AB_SIZES = (N_HEADS_A * HEAD_DIM, N_HEADS_A * HEAD_DIM, N_HEADS_A * HEAD_DIM,
            N_IDX_HEADS * D_IDX, D_IDX, N_IDX_HEADS,
            N_HEADS_B * HEAD_DIM, N_HEADS_B * HEAD_DIM, N_HEADS_B * HEAD_DIM, N_HEADS_B)
P_AB = 3 * N_HEADS_A * HEAD_DIM + N_IDX_HEADS * D_IDX + D_IDX + N_IDX_HEADS + 3 * N_HEADS_B * HEAD_DIM + N_HEADS_B

kernel_name = 'hybrid_dsa_fox_chunkgmlp_convffn_step'


def layer_norm(x, g, b):
    xf = x.astype(jnp.float32)
    mu = jnp.mean(xf, -1, keepdims=True)
    var = jnp.mean(jnp.square(xf - mu), -1, keepdims=True)
    return ((xf - mu) * lax.rsqrt(var + LN_EPS)).astype(x.dtype) * g + b


def adaln(c, w_mod, b_mod):
    mod = jax.nn.silu(c) @ w_mod + b_mod
    return jnp.split(mod[:, None, :], 6, axis=-1)


def rope(x, pos):
    half = x.shape[-1] // 2
    inv = ROPE_THETA ** (-jnp.arange(half, dtype=jnp.float32) / half)
    ang = pos.astype(jnp.float32)[:, None] * inv[None, :]
    cos = jnp.cos(ang)[:, None, :]
    sin = jnp.sin(ang)[:, None, :]
    x1 = x[..., :half].astype(jnp.float32)
    x2 = x[..., half:].astype(jnp.float32)
    return jnp.concatenate([x1 * cos - x2 * sin, x2 * cos + x1 * sin], -1).astype(x.dtype)


def project_ab(h, w_in, b_f, pos):
    B, T, _ = h.shape
    offs = []
    acc = 0
    for s in AB_SIZES[:-1]:
        acc += s
        offs.append(acc)
    qa, ka, va, qi, ki, wi, qb, kb, vb, fb = jnp.split(h @ w_in, offs, axis=-1)
    qa = rope(qa.reshape(B, T, N_HEADS_A, HEAD_DIM), pos)
    ka = rope(ka.reshape(B, T, N_HEADS_A, HEAD_DIM), pos)
    va = va.reshape(B, T, N_HEADS_A, HEAD_DIM)
    qi = rope(qi.reshape(B, T, N_IDX_HEADS, D_IDX), pos)
    ki = rope(ki[:, :, None, :], pos)[:, :, 0, :]
    qb = qb.reshape(B, T, N_HEADS_B, HEAD_DIM)
    kb = kb.reshape(B, T, N_HEADS_B, HEAD_DIM)
    vb = vb.reshape(B, T, N_HEADS_B, HEAD_DIM)
    logf = jax.nn.log_sigmoid(fb.astype(jnp.float32) + b_f.astype(jnp.float32))
    return qa, ka, va, qi, ki, wi, qb, kb, vb, logf


def indexer_topk(qi, wi, ki, q_pos, k_sel):
    logits = jnp.einsum('bqhd,bld->bqhl', qi, ki, preferred_element_type=jnp.float32) * (D_IDX ** -0.5)
    score = jnp.einsum('bqh,bqhl->bql', wi.astype(jnp.float32) * (N_IDX_HEADS ** -0.5), jax.nn.relu(logits))
    key_pos = jnp.arange(ki.shape[1], dtype=jnp.int32)
    causal = key_pos[None, :] <= q_pos[:, None]
    score = jnp.where(causal[None], score, -jnp.inf)
    _, sel = lax.top_k(score, k_sel)
    valid = sel <= q_pos[None, :, None]
    return sel, valid


def sparse_attend(q, k_g, v_g, valid):
    logits = jnp.einsum('bqhd,bqkhd->bqhk', q, k_g, preferred_element_type=jnp.float32) * (HEAD_DIM ** -0.5)
    logits = jnp.where(valid[:, :, None, :], logits, -jnp.inf)
    p = jax.nn.softmax(logits, axis=-1)
    return jnp.einsum('bqhk,bqkhd->bqhd', p.astype(v_g.dtype), v_g)


def take_rows(a, idx):
    return jax.vmap(lambda aa, ii: aa[ii])(a, idx)


def dsa_prompt(qa, ka, va, qi, ki, wi):
    B, T = qa.shape[:2]
    nb = T // Q_BLOCK
    k_sel = min(TOPK_MAX, T // 4)

    def to_blocks(a):
        return jnp.swapaxes(a.reshape((B, nb, Q_BLOCK) + a.shape[2:]), 0, 1)

    q_pos_blocks = jnp.arange(T, dtype=jnp.int32).reshape(nb, Q_BLOCK)

    def one_block(args):
        qa_b, qi_b, wi_b, qp = args
        sel, valid = indexer_topk(qi_b, wi_b, ki, qp, k_sel)
        return sparse_attend(qa_b, take_rows(ka, sel), take_rows(va, sel), valid)

    out = lax.map(one_block, (to_blocks(qa), to_blocks(qi), to_blocks(wi), q_pos_blocks))
    return jnp.swapaxes(out, 0, 1).reshape(B, T, N_HEADS_A * HEAD_DIM)


def dsa_sample(qa, ka, va, qi, ki, wi, pool_k, pool_v, pool_kidx, page_table):
    DB, Tn = qa.shape[:2]
    n_pages = page_table.shape[1]
    past = n_pages * PAGE_SIZE
    ki_past = pool_kidx[page_table].reshape(DB, past, D_IDX)
    ki_all = jnp.concatenate([ki_past, ki.astype(ki_past.dtype)], axis=1)
    q_pos = past + jnp.arange(Tn, dtype=jnp.int32)
    k_sel = min(TOPK_MAX, (past + Tn) // 4)
    sel, valid = indexer_topk(qi, wi, ki_all, q_pos, k_sel)
    in_past = (sel < past)[..., None, None]
    sp = jnp.minimum(sel, past - 1)
    phys = jax.vmap(lambda pt, s: pt[s])(page_table, sp // PAGE_SIZE)
    off = sp % PAGE_SIZE
    sn = jnp.clip(sel - past, 0, Tn - 1)
    k_g = jnp.where(in_past, pool_k[phys, off], take_rows(ka, sn).astype(pool_k.dtype))
    v_g = jnp.where(in_past, pool_v[phys, off], take_rows(va, sn).astype(pool_v.dtype))
    return sparse_attend(qa.astype(k_g.dtype), k_g, v_g, valid).reshape(DB, Tn, N_HEADS_A * HEAD_DIM)


def fox_prompt(q, k, v, logf):
    B, T = q.shape[:2]
    nb = T // Q_BLOCK
    cum_k = jnp.transpose(jnp.cumsum(logf, axis=1), (0, 2, 1))
    key_pos = jnp.arange(T, dtype=jnp.int32)
    q_blocks = jnp.swapaxes(q.reshape(B, nb, Q_BLOCK, N_HEADS_B, HEAD_DIM), 0, 1)
    cq_blocks = jnp.swapaxes(cum_k.reshape(B, N_HEADS_B, nb, Q_BLOCK), 0, 2).swapaxes(1, 2)
    q_pos_blocks = jnp.arange(T, dtype=jnp.int32).reshape(nb, Q_BLOCK)

    def one_block(args):
        q_b, cq_b, qp = args
        s = jnp.einsum('bqhd,bshd->bhqs', q_b, k, preferred_element_type=jnp.float32) * (HEAD_DIM ** -0.5)
        s = s + cq_b[..., None] - cum_k[:, :, None, :]
        s = jnp.where((key_pos[None, :] <= qp[:, None])[None, None], s, -jnp.inf)
        p = jax.nn.softmax(s, axis=-1)
        return jnp.einsum('bhqs,bshd->bqhd', p.astype(v.dtype), v)

    out = lax.map(one_block, (q_blocks, cq_blocks, q_pos_blocks))
    return jnp.swapaxes(out, 0, 1).reshape(B, T, N_HEADS_B * HEAD_DIM)


def fox_sample(q, k, v, logf, pool_k, pool_v, pool_logf, page_table):
    DB, Tn = q.shape[:2]
    n_pages = page_table.shape[1]
    past = n_pages * PAGE_SIZE
    k_past = pool_k[page_table]
    v_past = pool_v[page_table]
    lf_past = pool_logf[page_table].reshape(DB, past, N_HEADS_B).astype(jnp.float32)
    cum_k = jnp.transpose(jnp.cumsum(jnp.concatenate([lf_past, logf], axis=1), axis=1), (0, 2, 1))
    cq = cum_k[:, :, past:]
    qc = q.astype(k_past.dtype)
    s_past = jnp.einsum('bqhd,bnphd->bhqnp', qc, k_past, preferred_element_type=jnp.float32).reshape(DB, N_HEADS_B, Tn, past)
    s_new = jnp.einsum('bqhd,bshd->bhqs', qc, k.astype(k_past.dtype), preferred_element_type=jnp.float32)
    s = jnp.concatenate([s_past, s_new], axis=-1) * (HEAD_DIM ** -0.5) + cq[..., None] - cum_k[:, :, None, :]
    key_pos = jnp.arange(past + Tn, dtype=jnp.int32)
    q_pos = past + jnp.arange(Tn, dtype=jnp.int32)
    s = jnp.where((key_pos[None, :] <= q_pos[:, None])[None, None], s, -jnp.inf)
    p = jax.nn.softmax(s, axis=-1).astype(v_past.dtype)
    out = jnp.einsum('bhqnp,bnphd->bqhd', p[..., :past].reshape(DB, N_HEADS_B, Tn, n_pages, PAGE_SIZE), v_past) \
        + jnp.einsum('bhqs,bshd->bqhd', p[..., past:], v.astype(v_past.dtype))
    return out.reshape(DB, Tn, N_HEADS_B * HEAD_DIM)


def chunk_gate_inputs(h, w_in, lnv_g, lnv_b):
    z = jax.nn.gelu(h @ w_in, approximate=False)
    u, v = jnp.split(z, 2, axis=-1)
    return u, layer_norm(v, lnv_g, lnv_b)


def spatial_mix(v, w_s, b_s):
    B, T, _ = v.shape
    vg = v.reshape(B, T // CHUNK, CHUNK, N_GROUPS_C, D_CG // N_GROUPS_C)
    tri = (jnp.arange(CHUNK)[:, None] >= jnp.arange(CHUNK)[None, :]).astype(w_s.dtype)
    mixed = jnp.einsum('gts,bnsgc->bntgc', w_s * tri[None], vg) + b_s.T[:, :, None]
    return mixed.reshape(B, T, D_CG)


def conv_ffn(h, prev, w_up, w_conv, b_conv, w_down):
    T = h.shape[1]
    up = h @ w_up
    full = jnp.concatenate([prev.astype(up.dtype), up], axis=1)
    y = b_conv
    for k in range(CONV_W):
        y = y + w_conv[k] * full[:, k:k + T]
    gate, val = jnp.split(y, 2, axis=-1)
    out = (jax.nn.gelu(gate, approximate=False) * val) @ w_down
    return out, full[:, T:]


def setup_inputs(seed: int = 0) -> dict:
    key = jax.random.key(seed)
    ks = jax.random.split(key, 32)
    n_pages = PAST_LEN // PAGE_SIZE
    n_used = DEC_BATCH * n_pages
    n_phys = n_used + n_used // 4
    d = D_MODEL

    def nrm(i, shape, scale):
        return jax.random.normal(ks[i], shape, jnp.float32) * scale

    return {
        'x_prompt': nrm(0, (BATCH, SEQ, d), 1.0),
        'x_sample': nrm(1, (DEC_BATCH, DEC_SEQ, d), 1.0),
        'cache_a_k': nrm(2, (N_ATTN, n_phys, PAGE_SIZE, N_HEADS_A, HEAD_DIM), 1.0),
        'cache_a_v': nrm(3, (N_ATTN, n_phys, PAGE_SIZE, N_HEADS_A, HEAD_DIM), 1.0),
        'cache_a_kidx': nrm(4, (N_ATTN, n_phys, PAGE_SIZE, D_IDX), 1.0),
        'cache_b_k': nrm(5, (N_ATTN, n_phys, PAGE_SIZE, N_HEADS_B, HEAD_DIM), 1.0),
        'cache_b_v': nrm(6, (N_ATTN, n_phys, PAGE_SIZE, N_HEADS_B, HEAD_DIM), 1.0),
        'cache_b_logf': jax.nn.log_sigmoid(FORGET_BIAS + nrm(7, (N_ATTN, n_phys, PAGE_SIZE, N_HEADS_B), 1.0)),
        'state_ffn_conv': nrm(8, (DEPTH, DEC_BATCH, CONV_W - 1, 2 * D_FF), 1.0),
        'page_table': jax.random.permutation(ks[9], n_phys)[:n_used].reshape(DEC_BATCH, n_pages).astype(jnp.int32),
        'c_prompt': nrm(10, (BATCH, d), 1.0),
        'c_sample': nrm(11, (DEC_BATCH, d), 1.0),
        'w_mod': nrm(12, (DEPTH, d, 6 * d), 0.2 * d ** -0.5),
        'b_mod': nrm(13, (DEPTH, 6 * d), 0.01),
        'ln1_g': 1.0 + nrm(14, (DEPTH, d), 0.02),
        'ln1_b': nrm(15, (DEPTH, d), 0.02),
        'ln2_g': 1.0 + nrm(16, (DEPTH, d), 0.02),
        'ln2_b': nrm(17, (DEPTH, d), 0.02),
        'w_in_ab': nrm(18, (N_ATTN, d, P_AB), d ** -0.5),
        'b_forget': FORGET_BIAS + nrm(19, (N_ATTN, N_HEADS_B), 0.1),
        'w_out_ab': nrm(20, (N_ATTN, D_MIX, d), BETA * D_MIX ** -0.5),
        'w_in_c': nrm(21, (N_CMLP, d, 2 * D_CG), d ** -0.5),
        'lnv_g': 1.0 + nrm(22, (N_CMLP, D_CG), 0.02),
        'lnv_b': nrm(23, (N_CMLP, D_CG), 0.02),
        'w_spatial': nrm(24, (N_CMLP, N_GROUPS_C, CHUNK, CHUNK), 0.5 * CHUNK ** -0.5),
        'b_spatial': 1.0 + nrm(25, (N_CMLP, N_GROUPS_C, CHUNK), 0.02),
        'w_out_c': nrm(26, (N_CMLP, D_CG, d), BETA * D_CG ** -0.5),
        'w_up': nrm(27, (DEPTH, d, 2 * D_FF), d ** -0.5),
        'w_conv': nrm(28, (DEPTH, CONV_W, 2 * D_FF), CONV_W ** -0.5),
        'b_conv': nrm(29, (DEPTH, 2 * D_FF), 0.02),
        'w_down': nrm(30, (DEPTH, D_FF, d), BETA * D_FF ** -0.5),
    }


def reference(x_prompt, x_sample, cache_a_k, cache_a_v, cache_a_kidx, cache_b_k, cache_b_v, cache_b_logf,
              state_ffn_conv, page_table, c_prompt, c_sample,
              w_mod, b_mod, ln1_g, ln1_b, ln2_g, ln2_b,
              w_in_ab, b_forget, w_out_ab,
              w_in_c, lnv_g, lnv_b, w_spatial, b_spatial, w_out_c,
              w_up, w_conv, b_conv, w_down):
    t_p = x_prompt.shape[1]
    t_s = x_sample.shape[1]
    past = page_table.shape[1] * PAGE_SIZE
    pos_p = jnp.arange(t_p, dtype=jnp.int32)
    pos_s = past + jnp.arange(t_s, dtype=jnp.int32)
    xp, xs = x_prompt, x_sample
    ak_p, av_p, aki_p, bk_p, bv_p, blf_p, conv_p = [], [], [], [], [], [], []
    ak_s, av_s, aki_s, bk_s, bv_s, blf_s, conv_s, cv_s = [], [], [], [], [], [], [], []
    for i in range(DEPTH):
        j = i // 2
        sh1p, sc1p, g1p, sh2p, sc2p, g2p = adaln(c_prompt, w_mod[i], b_mod[i])
        sh1s, sc1s, g1s, sh2s, sc2s, g2s = adaln(c_sample, w_mod[i], b_mod[i])
        hp = xp * (1 + sc1p) + sh1p
        hs = xs * (1 + sc1s) + sh1s
        if i % 2 == 0:
            qa, ka, va, qi, ki, wi, qb, kb, vb, lf = project_ab(hp, w_in_ab[j], b_forget[j], pos_p)
            mix_p = jnp.concatenate([dsa_prompt(qa, ka, va, qi, ki, wi), fox_prompt(qb, kb, vb, lf)], axis=-1) @ w_out_ab[j]
            ak_p.append(ka); av_p.append(va); aki_p.append(ki)
            bk_p.append(kb); bv_p.append(vb); blf_p.append(lf)
            qa, ka, va, qi, ki, wi, qb, kb, vb, lf = project_ab(hs, w_in_ab[j], b_forget[j], pos_s)
            a_out = dsa_sample(qa, ka, va, qi, ki, wi, cache_a_k[j], cache_a_v[j], cache_a_kidx[j], page_table)
            b_out = fox_sample(qb, kb, vb, lf, cache_b_k[j], cache_b_v[j], cache_b_logf[j], page_table)
            mix_s = jnp.concatenate([a_out.astype(hs.dtype), b_out.astype(hs.dtype)], axis=-1) @ w_out_ab[j]
            ak_s.append(ka); av_s.append(va); aki_s.append(ki)
            bk_s.append(kb); bv_s.append(vb); blf_s.append(lf)
        else:
            u, v = chunk_gate_inputs(hp, w_in_c[j], lnv_g[j], lnv_b[j])
            mix_p = (u * spatial_mix(v, w_spatial[j], b_spatial[j])) @ w_out_c[j]
            u, v = chunk_gate_inputs(hs, w_in_c[j], lnv_g[j], lnv_b[j])
            pad = (-t_s) % CHUNK
            mixed = spatial_mix(jnp.pad(v, ((0, 0), (0, pad), (0, 0))), w_spatial[j], b_spatial[j])[:, :t_s]
            mix_s = (u * mixed) @ w_out_c[j]
            cv_s.append(v)
        xp = layer_norm(ALPHA * xp + (1 + g1p) * mix_p, ln1_g[i], ln1_b[i])
        xs = layer_norm(ALPHA * xs + (1 + g1s) * mix_s, ln1_g[i], ln1_b[i])
        hp = xp * (1 + sc2p) + sh2p
        hs = xs * (1 + sc2s) + sh2s
        prev_p = jnp.zeros((xp.shape[0], CONV_W - 1, 2 * D_FF), xp.dtype)
        ff_p, cp = conv_ffn(hp, prev_p, w_up[i], w_conv[i], b_conv[i], w_down[i])
        ff_s, cs = conv_ffn(hs, state_ffn_conv[i], w_up[i], w_conv[i], b_conv[i], w_down[i])
        conv_p.append(cp); conv_s.append(cs)
        xp = layer_norm(ALPHA * xp + (1 + g2p) * ff_p, ln2_g[i], ln2_b[i])
        xs = layer_norm(ALPHA * xs + (1 + g2s) * ff_s, ln2_g[i], ln2_b[i])
    return (xp, xs,
            jnp.stack(ak_p), jnp.stack(av_p), jnp.stack(aki_p), jnp.stack(bk_p), jnp.stack(bv_p), jnp.stack(blf_p),
            jnp.stack(conv_p),
            jnp.stack(ak_s), jnp.stack(av_s), jnp.stack(aki_s), jnp.stack(bk_s), jnp.stack(bv_s), jnp.stack(blf_s),
            jnp.stack(conv_s), jnp.stack(cv_s))
```

```python
import functools

import jax
import jax.numpy as jnp
from jax import lax
from jax.experimental import pallas as pl
from jax.experimental.pallas import tpu as pltpu

F32 = jnp.float32
BF16 = jnp.bfloat16

HEAD_DIM = 64
N_HEADS = 8
N_IDX_HEADS = 8
D_IDX = 64
TOPK = 256
PAGE = 128
CHUNK = 128
N_GROUPS_C = 8
CONV_W = 3
LN_EPS = 1e-5
ROPE_THETA = 10000.0
NEG = -1e30

LANES = 128
SUBLANES = 8
VMEM_LIMIT = 48 * 1024 * 1024

_NT = (((1,), (1,)), ((), ()))


def _cparams(n_axes):
    return pltpu.CompilerParams(dimension_semantics=("arbitrary",) * n_axes,
                                vmem_limit_bytes=VMEM_LIMIT)


def _const_spec(arr):
    return pl.BlockSpec(arr.shape, lambda i: (0,) * arr.ndim)


def _mod_spec(m, tm, nt):
    d = m.shape[2]
    if m.shape[1] == 1:
        tiles_per_group = nt // m.shape[0]
        return pl.BlockSpec((1, 1, d), lambda i: (i // tiles_per_group, 0, 0))
    return pl.BlockSpec((1, tm, d), lambda i: (0, i, 0))


def _ln(z, g, b):
    mu = jnp.mean(z, axis=-1, keepdims=True)
    zc = z - mu
    var = jnp.mean(zc * zc, axis=-1, keepdims=True)
    return zc * lax.rsqrt(var + LN_EPS) * g + b


def _gelu(x):
    return 0.5 * x * (1.0 + lax.erf(x * (2.0 ** -0.5)))


def _split3(x):
    hi = x.astype(BF16)
    r = x - hi.astype(F32)
    mid = r.astype(BF16)
    lo = (r - mid.astype(F32)).astype(BF16)
    return hi, mid, lo


def _dot3(x, m, left):
    parts = _split3(x)
    out = None
    for p in parts:
        d = (jnp.dot(m, p, preferred_element_type=F32) if left
             else jnp.dot(p, m, preferred_element_type=F32))
        out = d if out is None else out + d
    return out


def _mod_kernel(c_ref, w_ref, b_ref, o_ref):
    c = c_ref[...]
    a = (c * jax.nn.sigmoid(c)).astype(BF16)
    o_ref[0] = jnp.dot(a, w_ref[0].astype(BF16), preferred_element_type=F32) + b_ref[0]


def _adaln_mod(c_all, w_mod, b_mod):
    depth, d, n6 = w_mod.shape
    rows = c_all.shape[0]
    tn = n6 // 4
    return pl.pallas_call(
        _mod_kernel,
        grid=(depth, n6 // tn),
        in_specs=[pl.BlockSpec((rows, d), lambda l, j: (0, 0)),
                  pl.BlockSpec((1, d, tn), lambda l, j: (l, 0, j)),
                  pl.BlockSpec((1, 1, tn), lambda l, j: (l, 0, j))],
        out_specs=pl.BlockSpec((1, rows, tn), lambda l, j: (l, 0, j)),
        out_shape=jax.ShapeDtypeStruct((depth, rows, n6), F32),
        compiler_params=_cparams(2),
        name="adaln_mod",
    )(c_all, w_mod, b_mod.reshape(depth, 1, n6))


_G_QA, _G_KA, _G_VA, _G_QI, _G_QB, _G_KB, _G_VB = range(7)
_GW = N_HEADS * HEAD_DIM
_TAIL_OFF = 7 * _GW
_TAIL_WI = D_IDX
_TAIL_F = D_IDX + N_IDX_HEADS


def _pack_w_ab(w):
    sizes = (_GW, _GW, _GW, N_IDX_HEADS * D_IDX, D_IDX, N_IDX_HEADS, _GW, _GW, _GW, N_HEADS)
    offs, acc = [], 0
    for s in sizes[:-1]:
        acc += s
        offs.append(acc)
    qa, ka, va, qi, ki, wi, qb, kb, vb, fb = jnp.split(w, offs, axis=1)
    pad = jnp.zeros((w.shape[0], LANES - D_IDX - N_IDX_HEADS - N_HEADS), w.dtype)
    return jnp.concatenate([qa, ka, va, qi, qb, kb, vb, ki, wi, fb, pad], axis=1).astype(BF16)


def _rope_tables(pos):
    half = HEAD_DIM // 2
    inv = ROPE_THETA ** (-jnp.arange(half, dtype=F32) / half)
    ang = pos.astype(F32)[:, None] * inv[None, :]
    cos, sin = jnp.cos(ang), jnp.sin(ang)
    return (jnp.concatenate([cos] * 4, axis=1),
            jnp.concatenate([-sin, sin, -sin, sin], axis=1))


def _proj_ab_kernel(x_ref, sc_ref, sh_ref, cos_ref, sin_ref, w_ref, bf_ref,
                    qa_ref, kaf_ref, kab_ref, vaf_ref, vab_ref, qi_ref,
                    qb_ref, kbf_ref, kbb_ref, vbf_ref, vbb_ref, tail_ref, kid_ref):
    tm = x_ref.shape[0]
    h = (x_ref[...] * (1.0 + sc_ref[0]) + sh_ref[0]).astype(BF16)
    c1, s1 = cos_ref[...], sin_ref[...]
    c4 = jnp.concatenate([c1] * 4, axis=1)
    s4 = jnp.concatenate([s1] * 4, axis=1)
    lane4 = lax.broadcasted_iota(jnp.int32, (tm, _GW), 1)
    first4 = (lane4 & (HEAD_DIM - 1)) < HEAD_DIM // 2
    q_scale = HEAD_DIM ** -0.5

    def proj(g):
        return jnp.dot(h, w_ref[:, g * _GW:(g + 1) * _GW], preferred_element_type=F32)

    def rope(p):
        rot = jnp.where(first4, pltpu.roll(p, _GW - HEAD_DIM // 2, 1), pltpu.roll(p, HEAD_DIM // 2, 1))
        return p * c4 + rot * s4

    qa_ref[...] = (rope(proj(_G_QA)) * q_scale).astype(BF16)
    ka = rope(proj(_G_KA))
    kaf_ref[...] = ka
    kab_ref[...] = ka.astype(BF16)
    va = proj(_G_VA)
    vaf_ref[...] = va
    vab_ref[...] = va.astype(BF16)
    qi_ref[...] = (rope(proj(_G_QI)) * (D_IDX ** -0.5)).astype(BF16)
    qb_ref[...] = (proj(_G_QB) * q_scale).astype(BF16)
    kb = proj(_G_KB)
    kbf_ref[...] = kb
    kbb_ref[...] = kb.astype(BF16)
    vb = proj(_G_VB)
    vbf_ref[...] = vb
    vbb_ref[...] = vb.astype(BF16)

    t = jnp.dot(h, w_ref[:, _TAIL_OFF:_TAIL_OFF + LANES], preferred_element_type=F32)
    lane = lax.broadcasted_iota(jnp.int32, (tm, LANES), 1)
    first = (lane & (HEAD_DIM - 1)) < HEAD_DIM // 2
    rot = jnp.where(first, pltpu.roll(t, LANES - HEAD_DIM // 2, 1), pltpu.roll(t, HEAD_DIM // 2, 1))
    roped = t * c1 + rot * s1
    z = t + bf_ref[...]
    logf = jnp.minimum(z, 0.0) - jnp.log1p(jnp.exp(-jnp.abs(z)))
    tail = jnp.where(lane < D_IDX, roped,
                     jnp.where(lane < _TAIL_F, t * (N_IDX_HEADS ** -0.5),
                               jnp.where(lane < _TAIL_F + N_HEADS, logf, 0.0)))
    tail_ref[...] = tail
    kid_ref[...] = jnp.where(lane < D_IDX, roped, pltpu.roll(roped, D_IDX, 1)).astype(BF16)


def _proj_ab(x, sc, sh, cos_t, sin_t, w_packed, bf_row, tm):
    rows, d = x.shape
    nt = rows // tm
    t_tiles = cos_t.shape[0] // tm
    mod_spec = _mod_spec(sc, tm, nt)
    rope_spec = pl.BlockSpec((tm, LANES), lambda i: (i % t_tiles, 0))
    wide = lambda: pl.BlockSpec((tm, _GW), lambda i: (i, 0))
    narrow = lambda: pl.BlockSpec((tm, LANES), lambda i: (i, 0))
    f = jax.ShapeDtypeStruct((rows, _GW), F32)
    b = jax.ShapeDtypeStruct((rows, _GW), BF16)
    out_shape = (b, f, b, f, b, b, b, f, b, f, b,
                 jax.ShapeDtypeStruct((rows, LANES), F32), jax.ShapeDtypeStruct((rows, LANES), BF16))
    return pl.pallas_call(
        _proj_ab_kernel,
        grid=(nt,),
        in_specs=[pl.BlockSpec((tm, d), lambda i: (i, 0)), mod_spec, mod_spec, rope_spec, rope_spec,
                  _const_spec(w_packed), _const_spec(bf_row)],
        out_specs=tuple([wide() for _ in range(11)] + [narrow(), narrow()]),
        out_shape=out_shape,
        compiler_params=_cparams(1),
        name="proj_ab",
    )(x, sc, sh, cos_t, sin_t, w_packed, bf_row)


def _cumsum_kernel(t_ref, o_ref):
    n = t_ref.shape[1] // LANES
    r = lax.broadcasted_iota(jnp.int32, (LANES, LANES), 0)
    c = lax.broadcasted_iota(jnp.int32, (LANES, LANES), 1)
    tri = jnp.where(c <= r, 1.0, 0.0).astype(BF16)
    carry = jnp.zeros((1, LANES), F32)
    for k in range(n):
        x = t_ref[0, k * LANES:(k + 1) * LANES, :]
        cum = _dot3(x, tri, left=True) + carry
        carry = cum[LANES - 1:LANES, :]
        o_ref[0, :, k * LANES:(k + 1) * LANES] = cum.T[_TAIL_F:_TAIL_F + N_HEADS, :]


def _logf_cumsum(tail):
    bsz, t, _ = tail.shape
    return pl.pallas_call(
        _cumsum_kernel,
        grid=(bsz,),
        in_specs=[pl.BlockSpec((1, t, LANES), lambda b: (b, 0, 0))],
        out_specs=pl.BlockSpec((1, N_HEADS, t), lambda b: (b, 0, 0)),
        out_shape=jax.ShapeDtypeStruct((bsz, N_HEADS, t), F32),
        compiler_params=_cparams(1),
        name="logf_cumsum",
    )(tail)


def _pair_step(q0, q1, kc, vc, b0, b1, carry, lt64):
    m0, l0, m1, l1, acc = carry
    s0 = lax.dot_general(q0, kc, _NT, preferred_element_type=F32) + b0
    s1 = lax.dot_general(q1, kc, _NT, preferred_element_type=F32) + b1
    n0 = jnp.maximum(m0, jnp.max(s0, axis=1, keepdims=True))
    n1 = jnp.maximum(m1, jnp.max(s1, axis=1, keepdims=True))
    a0, a1 = jnp.exp(m0 - n0), jnp.exp(m1 - n1)
    p0, p1 = jnp.exp(s0 - n0), jnp.exp(s1 - n1)
    l0 = a0 * l0 + jnp.sum(p0, axis=1, keepdims=True)
    l1 = a1 * l1 + jnp.sum(p1, axis=1, keepdims=True)
    pv0 = jnp.dot(p0.astype(BF16), vc, preferred_element_type=F32)
    pv1 = jnp.dot(p1.astype(BF16), vc, preferred_element_type=F32)
    acc = jnp.where(lt64, a0 * acc + pv0, a1 * acc + pv1)
    return n0, l0, n1, l1, acc


def _pair_init(tq):
    col = lambda v: jnp.full((tq, 1), v, F32)
    return col(-jnp.inf), col(0.0), col(-jnp.inf), col(0.0), jnp.zeros((tq, LANES), F32)


def _pair_finish(carry, lt64):
    _, l0, _, l1, acc = carry
    return acc / jnp.where(lt64, l0, l1)


def _fox_kernel(q_ref, k_ref, v_ref, ck_ref, o_ref):
    tq = q_ref.shape[1]
    it = pl.program_id(2)
    q2 = q_ref[0]
    lt64 = lax.broadcasted_iota(jnp.int32, (tq, LANES), 1) < HEAD_DIM
    zero = jnp.zeros_like(q2)
    q0, q1 = jnp.where(lt64, q2, zero), jnp.where(lt64, zero, q2)
    row = lax.broadcasted_iota(jnp.int32, (tq, tq), 0)
    col = lax.broadcasted_iota(jnp.int32, (tq, tq), 1)

    def step(j, carry, diag):
        off = pl.multiple_of(j * tq, tq)
        kc = k_ref[0, pl.ds(off, tq), :]
        vc = v_ref[0, pl.ds(off, tq), :]
        ck = ck_ref[0, j, 0]
        b0, b1 = -ck[0:1, :], -ck[1:2, :]
        if diag:
            causal = col <= row
            b0, b1 = jnp.where(causal, b0, NEG), jnp.where(causal, b1, NEG)
        return _pair_step(q0, q1, kc, vc, b0, b1, carry, lt64)

    carry = lax.fori_loop(0, it, lambda j, c: step(j, c, False), _pair_init(tq))
    carry = step(it, carry, True)
    o_ref[0] = _pair_finish(carry, lt64).astype(o_ref.dtype)


def _fox_prompt(qb, kb, vb, cum, tq):
    bsz, t, _ = qb.shape
    nq = t // tq
    npair = N_HEADS // 2
    ck = cum.reshape(bsz, npair, 2, nq, tq).transpose(0, 3, 1, 2, 4)
    return pl.pallas_call(
        _fox_kernel,
        grid=(bsz, npair, nq),
        in_specs=[pl.BlockSpec((1, tq, LANES), lambda b, p, i: (b, i, p)),
                  pl.BlockSpec((1, t, LANES), lambda b, p, i: (b, 0, p)),
                  pl.BlockSpec((1, t, LANES), lambda b, p, i: (b, 0, p)),
                  pl.BlockSpec((1, nq, 1, 2, tq), lambda b, p, i: (b, 0, p, 0, 0))],
        out_specs=pl.BlockSpec((1, tq, LANES), lambda b, p, i: (b, i, p)),
        out_shape=jax.ShapeDtypeStruct((bsz, t, _GW), BF16),
        compiler_params=_cparams(3),
        name="fox_prompt",
    )(qb, kb, vb, ck)


def _topk_to_bias(sc_ref, nch, nvalid, ksel, n_bisect=24):
    rows, kc = sc_ref.shape[1], sc_ref.shape[2]
    kf = float(ksel)
    ninf = -jnp.inf
    col0 = lambda v: jnp.full((rows, 1), v, F32)

    def count_ge(t):
        return lax.fori_loop(
            0, nch, lambda j, c: c + jnp.sum(jnp.where(sc_ref[j] >= t, 1.0, 0.0), axis=1, keepdims=True),
            col0(0.0))

    def max_below(h):
        def body(j, m):
            s = sc_ref[j]
            return jnp.maximum(m, jnp.max(jnp.where(s < h, s, ninf), axis=1, keepdims=True))
        return lax.fori_loop(0, nch, body, col0(ninf))

    def min_valid():
        def body(j, m):
            s = sc_ref[j]
            return jnp.minimum(m, jnp.min(jnp.where(s == ninf, jnp.inf, s), axis=1, keepdims=True))
        return lax.fori_loop(0, nch, body, col0(jnp.inf))

    small = nvalid <= kf
    lo = min_valid()
    v = max_below(col0(jnp.inf))
    ok = count_ge(v) >= kf
    res = jnp.where(jnp.logical_or(small, ok), 1.0, 0.0)
    thr = jnp.where(small, ninf, v)
    hi = jnp.where(res > 0.0, jnp.inf, v)

    def bisect(_, st):
        lo, hi = st
        mid = 0.5 * (lo + hi)
        ge = count_ge(mid) >= kf
        return jnp.where(ge, mid, lo), jnp.where(ge, hi, mid)

    lo, hi = lax.fori_loop(0, n_bisect, bisect, (lo, hi))

    def snap_cond(st):
        return jnp.min(st[2]) < 1.0

    def snap(st):
        thr, hi, res = st
        v = max_below(hi)
        ok = count_ge(v) >= kf
        open_ = res < 1.0
        new = jnp.logical_and(open_, ok)
        drop = jnp.logical_and(open_, jnp.logical_not(ok))
        return jnp.where(new, v, thr), jnp.where(drop, v, hi), jnp.where(ok, 1.0, res)

    thr, hi, res = lax.while_loop(snap_cond, snap, (thr, hi, res))

    n_gt = lax.fori_loop(
        0, nch, lambda j, c: c + jnp.sum(jnp.where(sc_ref[j] > thr, 1.0, 0.0), axis=1, keepdims=True),
        col0(0.0))
    need = kf - n_gt
    r = lax.broadcasted_iota(jnp.int32, (kc, kc), 0)
    c = lax.broadcasted_iota(jnp.int32, (kc, kc), 1)
    upper = jnp.where(r < c, 1.0, 0.0).astype(BF16)

    def to_bias(j, run):
        s = sc_ref[j]
        tie = jnp.logical_and(s == thr, s > ninf)
        tief = jnp.where(tie, 1.0, 0.0)
        before = jnp.dot(tief.astype(BF16), upper, preferred_element_type=F32) + run
        sel = jnp.logical_or(s > thr, jnp.logical_and(tie, before < need))
        sc_ref[j] = jnp.where(sel, 0.0, NEG)
        return run + jnp.sum(tief, axis=1, keepdims=True)

    lax.fori_loop(0, nch, to_bias, col0(0.0))


def _dsa_kernel(qi_ref, w_ref, kid_ref, qa_ref, ka_ref, va_ref, o_ref, sc_ref, *, ksel):
    tq = qi_ref.shape[1]
    it = pl.program_id(1)
    lane = lax.broadcasted_iota(jnp.int32, (tq, LANES), 1)
    lt64 = lane < HEAD_DIM
    row = lax.broadcasted_iota(jnp.int32, (tq, tq), 0)
    col = lax.broadcasted_iota(jnp.int32, (tq, tq), 1)

    def halves(x2):
        zero = jnp.zeros_like(x2)
        return jnp.where(lt64, x2, zero), jnp.where(lt64, zero, x2)

    qi = qi_ref[0]
    qm = []
    for p in range(N_IDX_HEADS // 2):
        qm.extend(halves(qi[:, p * LANES:(p + 1) * LANES]))
    wt = w_ref[0]
    wcol = [wt[:, _TAIL_WI + h:_TAIL_WI + h + 1] for h in range(N_IDX_HEADS)]

    def score_chunk(j, diag):
        off = pl.multiple_of(j * tq, tq)
        kc = kid_ref[0, pl.ds(off, tq), :]
        sc = jnp.zeros((tq, tq), F32)
        for h in range(N_IDX_HEADS):
            lg = lax.dot_general(qm[h], kc, _NT, preferred_element_type=F32)
            sc = sc + jnp.maximum(lg, 0.0) * wcol[h]
        if diag:
            sc = jnp.where(col <= row, sc, -jnp.inf)
        sc_ref[j] = sc

    def score_body(j, c):
        score_chunk(j, False)
        return c

    lax.fori_loop(0, it, score_body, 0)
    score_chunk(it, True)

    nvalid = (it * tq + 1 + lax.broadcasted_iota(jnp.int32, (tq, 1), 0)).astype(F32)
    _topk_to_bias(sc_ref, it + 1, nvalid, ksel)

    for p in range(N_HEADS // 2):
        q0, q1 = halves(qa_ref[0, :, p * LANES:(p + 1) * LANES])

        def step(j, carry, p=p, q0=q0, q1=q1):
            off = pl.multiple_of(j * tq, tq)
            kc = ka_ref[0, pl.ds(off, tq), p * LANES:(p + 1) * LANES]
            vc = va_ref[0, pl.ds(off, tq), p * LANES:(p + 1) * LANES]
            bias = sc_ref[j]
            return _pair_step(q0, q1, kc, vc, bias, bias, carry, lt64)

        carry = lax.fori_loop(0, it + 1, step, _pair_init(tq))
        o_ref[0, :, p * LANES:(p + 1) * LANES] = _pair_finish(carry, lt64).astype(o_ref.dtype)


def _dsa_prompt(qi, tail, kid, qa, ka, va, tq, ksel):
    bsz, t, _ = qa.shape
    nq = t // tq
    q_spec = lambda w: pl.BlockSpec((1, tq, w), lambda b, i: (b, i, 0))
    full = lambda w: pl.BlockSpec((1, t, w), lambda b, i: (b, 0, 0))
    return pl.pallas_call(
        functools.partial(_dsa_kernel, ksel=ksel),
        grid=(bsz, nq),
        in_specs=[q_spec(_GW), q_spec(LANES), full(LANES), q_spec(_GW), full(_GW), full(_GW)],
        out_specs=q_spec(_GW),
        out_shape=jax.ShapeDtypeStruct((bsz, t, _GW), BF16),
        scratch_shapes=[pltpu.VMEM((nq, tq, tq), F32)],
        compiler_params=_cparams(2),
        name="dsa_prompt",
    )(qi, tail, kid, qa, ka, va)


def _sample_idx_kernel(pt_ref, qi_ref, w_ref, knew_ref, *rest, pages_per_step, ksel):
    page_refs = rest[:pages_per_step]
    o_ref = rest[pages_per_step]
    sc_ref = rest[pages_per_step + 1]
    s = pl.program_id(1)
    nsteps = pl.num_programs(1)
    npg = sc_ref.shape[0] - 1
    tn = sc_ref.shape[1]
    qi = qi_ref[0]
    w = w_ref[0]

    def scores(keys_bf16):
        lg = lax.dot_general(qi, keys_bf16, _NT, preferred_element_type=F32)
        z = jnp.maximum(lg, 0.0) * w
        out = z[0:tn]
        for h in range(1, N_IDX_HEADS):
            out = out + z[h * tn:(h + 1) * tn]
        return out

    for i, pr in enumerate(page_refs):
        sc_ref[s * pages_per_step + i] = scores(pr[0].astype(BF16))

    @pl.when(s == nsteps - 1)
    def _():
        r = lax.broadcasted_iota(jnp.int32, (tn, PAGE), 0)
        c = lax.broadcasted_iota(jnp.int32, (tn, PAGE), 1)
        sc_ref[npg] = jnp.where(c <= r, scores(knew_ref[0]), -jnp.inf)
        nvalid = (npg * PAGE + 1 + lax.broadcasted_iota(jnp.int32, (tn, 1), 0)).astype(F32)
        _topk_to_bias(sc_ref, npg + 1, nvalid, ksel)
        o_ref[0] = sc_ref[...]


def _sample_indexer(page_table, qi_rows, w_rows, ki_new_pad, kidx_pool, pages_per_step, ksel):
    db, npg = page_table.shape
    tn = qi_rows.shape[1] // N_IDX_HEADS
    nsteps = npg // pages_per_step

    def page_spec(i):
        return pl.BlockSpec((1, PAGE, D_IDX), lambda b, s, pt: (pt[b, s * pages_per_step + i], 0, 0))

    grid_spec = pltpu.PrefetchScalarGridSpec(
        num_scalar_prefetch=1,
        grid=(db, nsteps),
        in_specs=[pl.BlockSpec((1,) + qi_rows.shape[1:], lambda b, s, pt: (b, 0, 0)),
                  pl.BlockSpec((1,) + w_rows.shape[1:], lambda b, s, pt: (b, 0, 0)),
                  pl.BlockSpec((1, PAGE, D_IDX), lambda b, s, pt: (b, 0, 0))]
                 + [page_spec(i) for i in range(pages_per_step)],
        out_specs=pl.BlockSpec((1, npg + 1, tn, PAGE), lambda b, s, pt: (b, 0, 0, 0)),
        scratch_shapes=[pltpu.VMEM((npg + 1, tn, PAGE), F32)],
    )
    return pl.pallas_call(
        functools.partial(_sample_idx_kernel, pages_per_step=pages_per_step, ksel=ksel),
        grid_spec=grid_spec,
        out_shape=jax.ShapeDtypeStruct((db, npg + 1, tn, PAGE), F32),
        compiler_params=_cparams(2),
        name="sample_indexer",
    )(page_table, qi_rows, w_rows, ki_new_pad, *([kidx_pool] * pages_per_step))


def _sample_attn_kernel(pt_ref, qa_ref, qb_ref, bias_ref, biasn_ref, lft_ref, lfn_ref,
                        kan_ref, van_ref, kbn_ref, vbn_ref, ka_ref, va_ref, kb_ref, vb_ref,
                        oa_ref, ob_ref, ma, la, acca, mb, lb, accb, run):
    p = pl.program_id(1)
    npg = pl.num_programs(1)
    rows = qa_ref.shape[1]
    tn = rows // N_HEADS

    @pl.when(p == 0)
    def _():
        for m in (ma, mb):
            m[...] = jnp.full(m.shape, -jnp.inf, F32)
        for z in (la, lb, acca, accb, run):
            z[...] = jnp.zeros(z.shape, F32)

    r = lax.broadcasted_iota(jnp.int32, (PAGE, PAGE), 0)
    c = lax.broadcasted_iota(jnp.int32, (PAGE, PAGE), 1)
    incl = jnp.where(r <= c, 1.0, 0.0).astype(BF16)

    def update(q, k, v, bias, m_ref, l_ref, acc_ref):
        s = lax.dot_general(q, k, _NT, preferred_element_type=F32) + bias
        m_old = m_ref[...]
        m_new = jnp.maximum(m_old, jnp.max(s, axis=1, keepdims=True))
        a = jnp.exp(m_old - m_new)
        pr = jnp.exp(s - m_new)
        l_ref[...] = a * l_ref[...] + jnp.sum(pr, axis=1, keepdims=True)
        acc_ref[...] = a * acc_ref[...] + jnp.dot(pr.astype(BF16), v, preferred_element_type=F32)
        m_ref[...] = m_new

    def head_rows(x):
        return jnp.concatenate([jnp.broadcast_to(x[h:h + 1], (tn, x.shape[1])) for h in range(N_HEADS)], axis=0)

    def tile_rows(x):
        return jnp.concatenate([x] * N_HEADS, axis=0)

    def forget_bias(lf_t):
        cum = _dot3(head_rows(lf_t), incl, left=False) + run[...]
        run[...] = cum[:, PAGE - 1:PAGE]
        return -cum

    update(qa_ref[0], ka_ref[0].astype(BF16), va_ref[0].astype(BF16), tile_rows(bias_ref[0, 0]), ma, la, acca)
    update(qb_ref[0], kb_ref[0].astype(BF16), vb_ref[0].astype(BF16), forget_bias(lft_ref[0]), mb, lb, accb)

    @pl.when(p == npg - 1)
    def _():
        update(qa_ref[0], kan_ref[0], van_ref[0], tile_rows(biasn_ref[0, 0]), ma, la, acca)
        qrow = lax.broadcasted_iota(jnp.int32, (rows, PAGE), 0) & (tn - 1)
        key = lax.broadcasted_iota(jnp.int32, (rows, PAGE), 1)
        fb = jnp.where(key <= qrow, forget_bias(lfn_ref[0]), NEG)
        update(qb_ref[0], kbn_ref[0], vbn_ref[0], fb, mb, lb, accb)
        lane = lax.broadcasted_iota(jnp.int32, (tn, _GW), 1)
        for l_ref, acc_ref, o_ref in ((la, acca, oa_ref), (lb, accb, ob_ref)):
            full = acc_ref[...] / l_ref[...]
            out = jnp.zeros((tn, _GW), F32)
            for h in range(N_HEADS):
                mine = jnp.logical_and(lane >= h * HEAD_DIM, lane < (h + 1) * HEAD_DIM)
                out = jnp.where(mine, full[h * tn:(h + 1) * tn], out)
            o_ref[0] = out.astype(o_ref.dtype)


def _sample_attn(page_table, qa_bd, qb_bd, bias, lft_pool, lfn, kan, van, kbn, vbn,
                 ka_pool, va_pool, kb_pool, vb_pool):
    db, npg = page_table.shape
    rows = qa_bd.shape[1]
    tn = rows // N_HEADS
    per_b = lambda shp: pl.BlockSpec((1,) + shp, lambda b, p, pt: (b,) + (0,) * len(shp))
    paged = lambda shp: pl.BlockSpec((1,) + shp, lambda b, p, pt: (pt[b, p],) + (0,) * len(shp))
    grid_spec = pltpu.PrefetchScalarGridSpec(
        num_scalar_prefetch=1,
        grid=(db, npg),
        in_specs=[per_b((rows, _GW)), per_b((rows, _GW)),
                  pl.BlockSpec((1, 1, tn, PAGE), lambda b, p, pt: (b, p, 0, 0)),
                  pl.BlockSpec((1, 1, tn, PAGE), lambda b, p, pt: (b, npg, 0, 0)),
                  paged((N_HEADS, PAGE)), per_b((N_HEADS, PAGE)),
                  per_b((PAGE, _GW)), per_b((PAGE, _GW)), per_b((PAGE, _GW)), per_b((PAGE, _GW)),
                  paged((PAGE, _GW)), paged((PAGE, _GW)), paged((PAGE, _GW)), paged((PAGE, _GW))],
        out_specs=(per_b((tn, _GW)), per_b((tn, _GW))),
        scratch_shapes=[pltpu.VMEM((rows, 1), F32), pltpu.VMEM((rows, 1), F32), pltpu.VMEM((rows, _GW), F32),
                        pltpu.VMEM((rows, 1), F32), pltpu.VMEM((rows, 1), F32), pltpu.VMEM((rows, _GW), F32),
                        pltpu.VMEM((rows, 1), F32)],
    )
    o = jax.ShapeDtypeStruct((db, tn, _GW), BF16)
    return pl.pallas_call(
        _sample_attn_kernel,
        grid_spec=grid_spec,
        out_shape=(o, o),
        compiler_params=_cparams(2),
        name="sample_attn",
    )(page_table, qa_bd, qb_bd, bias, bias, lft_pool, lfn, kan, van, kbn, vbn,
      ka_pool, va_pool, kb_pool, vb_pool)


def _out_ln_kernel(a_ref, b_ref, x_ref, g_ref, wa_ref, wb_ref, lg_ref, lb_ref, o_ref, *, alpha):
    mix = (jnp.dot(a_ref[...], wa_ref[...], preferred_element_type=F32)
           + jnp.dot(b_ref[...], wb_ref[...], preferred_element_type=F32))
    z = alpha * x_ref[...] + (1.0 + g_ref[0]) * mix
    o_ref[...] = _ln(z, lg_ref[...], lb_ref[...])


def _out_ln(a, b, x, gate, w_out, ln_g, ln_b, alpha, tm):
    rows, d = x.shape
    nt = rows // tm
    ka = a.shape[1]
    wa, wb = w_out[:ka].astype(BF16), w_out[ka:].astype(BF16)
    const = _const_spec
    lg, lb = ln_g.reshape(1, d), ln_b.reshape(1, d)
    return pl.pallas_call(
        functools.partial(_out_ln_kernel, alpha=alpha),
        grid=(nt,),
        in_specs=[pl.BlockSpec((tm, ka), lambda i: (i, 0)), pl.BlockSpec((tm, b.shape[1]), lambda i: (i, 0)),
                  pl.BlockSpec((tm, d), lambda i: (i, 0)),
                  _mod_spec(gate, tm, nt),
                  const(wa), const(wb), const(lg), const(lb)],
        out_specs=pl.BlockSpec((tm, d), lambda i: (i, 0)),
        out_shape=jax.ShapeDtypeStruct((rows, d), F32),
        compiler_params=_cparams(1),
        name="out_ln",
    )(a, b, x, gate, wa, wb, lg, lb)


_FF_CW = 256


def _ffn_kernel(x_ref, sc_ref, sh_ref, g_ref, prev_ref, wg_ref, wv_ref, cw_ref, wd_ref, lg_ref, lb_ref,
                o_ref, up_ref, bufg, bufv, carry, *, alpha, nseq, tiles_per_seq):
    tm, d = x_ref.shape
    ts = tm // nseq
    nch, _, cw = wg_ref.shape
    dff = nch * cw
    x = x_ref[...]
    h = (x * (1.0 + sc_ref[0]) + sh_ref[0]).astype(BF16)
    chained = tiles_per_seq > 1

    if chained:
        @pl.when(pl.program_id(0) % tiles_per_seq == 0)
        def _():
            carry[...] = prev_ref[0]

    acc = jnp.zeros((tm, d), F32)
    for c in range(nch):
        cwr = cw_ref[c]
        ys = []
        for half, (w_ref, buf) in enumerate(((wg_ref, bufg), (wv_ref, bufv))):
            lo = half * dff + c * cw
            u = jnp.dot(h, w_ref[c], preferred_element_type=F32)
            if chained:
                buf[0, 0:SUBLANES, :] = carry[:, lo:lo + cw]
                carry[:, lo:lo + cw] = u[tm - SUBLANES:, :]
                up_ref[:, lo:lo + cw] = u[tm - SUBLANES:, :]
            else:
                buf[:, 0:SUBLANES, :] = prev_ref[:, :, lo:lo + cw]
                up_ref[:, lo:lo + cw] = u
            buf[:, SUBLANES:SUBLANES + ts, :] = u.reshape(nseq, ts, cw)
            um1 = buf[:, SUBLANES - 1:SUBLANES - 1 + ts, :].reshape(tm, cw)
            um2 = buf[:, SUBLANES - 2:SUBLANES - 2 + ts, :].reshape(tm, cw)
            k0 = 4 * half
            y = cwr[k0 + 3:k0 + 4, :] + cwr[k0:k0 + 1, :] * um2
            y = y + cwr[k0 + 1:k0 + 2, :] * um1
            ys.append(y + cwr[k0 + 2:k0 + 3, :] * u)
        act = (_gelu(ys[0]) * ys[1]).astype(BF16)
        acc = acc + jnp.dot(act, wd_ref[c], preferred_element_type=F32)

    z = alpha * x + (1.0 + g_ref[0]) * acc
    o_ref[...] = _ln(z, lg_ref[...], lb_ref[...])


def _conv_ffn(x, sc, sh, gate, prev8, w_up, w_conv, b_conv, w_down, ln_g, ln_b, alpha, tm, seq_len):
    rows, d = x.shape
    dff = w_down.shape[0]
    nch = dff // _FF_CW
    nt = rows // tm
    if seq_len >= tm:
        nseq, tiles_per_seq = 1, seq_len // tm
        prev_spec = pl.BlockSpec((1, SUBLANES, 2 * dff), lambda i: (i // tiles_per_seq, 0, 0))
        up_rows, up_spec = nt * SUBLANES, pl.BlockSpec((SUBLANES, 2 * dff), lambda i: (i, 0))
    else:
        nseq, tiles_per_seq = tm // seq_len, 1
        prev_spec = pl.BlockSpec((nseq, SUBLANES, 2 * dff), lambda i: (i, 0, 0))
        up_rows, up_spec = rows, pl.BlockSpec((tm, 2 * dff), lambda i: (i, 0))
    ts = tm // nseq
    chunked = lambda w: w.reshape(w.shape[0], nch, _FF_CW).transpose(1, 0, 2)
    wg = chunked(w_up[:, :dff]).astype(BF16)
    wv = chunked(w_up[:, dff:]).astype(BF16)
    wd = w_down.reshape(nch, _FF_CW, d).astype(BF16)
    conv_rows = jnp.concatenate([w_conv[:, :dff], b_conv[None, :dff], w_conv[:, dff:], b_conv[None, dff:]], axis=0)
    cwr = chunked(conv_rows)
    lg, lb = ln_g.reshape(1, d), ln_b.reshape(1, d)
    const = _const_spec
    mod_spec = lambda m: _mod_spec(m, tm, nt)
    return pl.pallas_call(
        functools.partial(_ffn_kernel, alpha=alpha, nseq=nseq, tiles_per_seq=tiles_per_seq),
        grid=(nt,),
        in_specs=[pl.BlockSpec((tm, d), lambda i: (i, 0)), mod_spec(sc), mod_spec(sh), mod_spec(gate), prev_spec,
                  const(wg), const(wv), const(cwr), const(wd), const(lg), const(lb)],
        out_specs=(pl.BlockSpec((tm, d), lambda i: (i, 0)), up_spec),
        out_shape=(jax.ShapeDtypeStruct((rows, d), F32), jax.ShapeDtypeStruct((up_rows, 2 * dff), F32)),
        scratch_shapes=[pltpu.VMEM((nseq, ts + SUBLANES, _FF_CW), F32),
                        pltpu.VMEM((nseq, ts + SUBLANES, _FF_CW), F32),
                        pltpu.VMEM((SUBLANES, 2 * dff), F32)],
        compiler_params=_cparams(1),
        name="conv_ffn",
    )(x, sc, sh, gate, prev8, wg, wv, cwr, wd, lg, lb)


def _gmlp_kernel(x_ref, sc_ref, sh_ref, g_ref, wv_ref, wu_ref, lvg_ref, lvb_ref, wmix_ref, bs_ref, wo_ref,
                 lg_ref, lb_ref, o_ref, *maybe_v_ref, alpha):
    tm, d = x_ref.shape
    ng, _, gw = wu_ref.shape
    x = x_ref[...]
    h = (x * (1.0 + sc_ref[0]) + sh_ref[0]).astype(BF16)
    v = _ln(_gelu(jnp.dot(h, wv_ref[...], preferred_element_type=F32)), lvg_ref[...], lvb_ref[...])
    if maybe_v_ref:
        maybe_v_ref[0][...] = v
    vb = v.astype(BF16)
    bs = bs_ref[...]
    acc = jnp.zeros((tm, d), F32)
    for g in range(ng):
        u = _gelu(jnp.dot(h, wu_ref[g], preferred_element_type=F32))
        mixed = jnp.dot(wmix_ref[g], vb[:, g * gw:(g + 1) * gw], preferred_element_type=F32) + bs[:, g:g + 1]
        acc = acc + jnp.dot((u * mixed).astype(BF16), wo_ref[g], preferred_element_type=F32)
    z = alpha * x + (1.0 + g_ref[0]) * acc
    o_ref[...] = _ln(z, lg_ref[...], lb_ref[...])


def _chunk_gmlp(x, sc, sh, gate, w_in, lnv_g, lnv_b, w_spatial, b_spatial, w_out, ln_g, ln_b,
                alpha, tm, chunk_len, emit_v):
    rows, d = x.shape
    dcg = w_out.shape[0]
    ng = w_spatial.shape[0]
    gw = dcg // ng
    nt = rows // tm
    wu = w_in[:, :dcg].reshape(d, ng, gw).transpose(1, 0, 2).astype(BF16)
    wv = w_in[:, dcg:].astype(BF16)
    wo = w_out.reshape(ng, gw, d).astype(BF16)
    tri = (jnp.arange(chunk_len)[:, None] >= jnp.arange(chunk_len)[None, :]).astype(w_spatial.dtype)
    wc = w_spatial[:, :chunk_len, :chunk_len] * tri[None]
    eye = jnp.eye(tm // chunk_len, dtype=w_spatial.dtype)
    wmix = jnp.einsum("ab,gts->gatbs", eye, wc).reshape(ng, tm, tm).astype(BF16)
    bs = jnp.tile(b_spatial[:, :chunk_len].T, (tm // chunk_len, 1))
    bs = jnp.pad(bs, ((0, 0), (0, LANES - ng)))
    lvg, lvb = lnv_g.reshape(1, dcg), lnv_b.reshape(1, dcg)
    lg, lb = ln_g.reshape(1, d), ln_b.reshape(1, d)
    const = _const_spec
    mod_spec = lambda m: _mod_spec(m, tm, nt)
    row_spec = lambda w: pl.BlockSpec((tm, w), lambda i: (i, 0))
    out_specs, out_shape = [row_spec(d)], [jax.ShapeDtypeStruct((rows, d), F32)]
    if emit_v:
        out_specs.append(row_spec(dcg))
        out_shape.append(jax.ShapeDtypeStruct((rows, dcg), F32))
    return pl.pallas_call(
        functools.partial(_gmlp_kernel, alpha=alpha),
        grid=(nt,),
        in_specs=[row_spec(d), mod_spec(sc), mod_spec(sh), mod_spec(gate), const(wv), const(wu), const(lvg),
                  const(lvb), const(wmix), const(bs), const(wo), const(lg), const(lb)],
        out_specs=tuple(out_specs),
        out_shape=tuple(out_shape),
        compiler_params=_cparams(1),
        name="chunk_gmlp",
    )(x, sc, sh, gate, wv, wu, lvg, lvb, wmix, bs, wo, lg, lb)


def kernel(x_prompt, x_sample, cache_a_k, cache_a_v, cache_a_kidx, cache_b_k, cache_b_v, cache_b_logf,
           state_ffn_conv, page_table, c_prompt, c_sample, w_mod, b_mod, ln1_g, ln1_b, ln2_g, ln2_b,
           w_in_ab, b_forget, w_out_ab, w_in_c, lnv_g, lnv_b, w_spatial, b_spatial, w_out_c,
           w_up, w_conv, b_conv, w_down):
    bsz, t, d = x_prompt.shape
    db, tn, _ = x_sample.shape
    depth = w_mod.shape[0]
    dff2 = w_up.shape[2]
    npg = page_table.shape[1]
    past = npg * PAGE
    alpha = (2 * depth) ** 0.25
    rows_p, rows_s = bsz * t, db * tn
    tm_p = 256
    tq = 256

    nc = bsz + db
    c_all = jnp.concatenate([c_prompt, c_sample, jnp.zeros((-nc % SUBLANES, d), F32)], axis=0)
    mod = _adaln_mod(c_all, w_mod, b_mod)

    xp = x_prompt.reshape(rows_p, d)
    xs = x_sample.reshape(rows_s, d)
    cos_p, sin_p = _rope_tables(jnp.arange(t, dtype=jnp.int32))
    cos_s, sin_s = _rope_tables(jnp.tile(past + jnp.arange(tn, dtype=jnp.int32), db))

    outs_p = {k: [] for k in ("ak", "av", "aki", "bk", "bv", "blf", "conv")}
    outs_s = {k: [] for k in ("ak", "av", "aki", "bk", "bv", "blf", "conv", "cv")}

    for i in range(depth):
        j = i // 2
        mp = mod[i, :bsz].reshape(bsz, 6, 1, d)
        ms = jnp.repeat(mod[i, bsz:nc].reshape(db, 6, d), tn, axis=0).reshape(1, rows_s, 6, d)
        sh1p, sc1p, g1p, sh2p, sc2p, g2p = (mp[:, k] for k in range(6))
        sh1s, sc1s, g1s, sh2s, sc2s, g2s = (ms[:, :, k] for k in range(6))

        if i % 2 == 0:
            w_packed = _pack_w_ab(w_in_ab[j])
            bf_row = jnp.zeros((1, LANES), F32).at[0, _TAIL_F:_TAIL_F + N_HEADS].set(b_forget[j])
            w_out = w_out_ab[j]

            (qa, kaf, kab, vaf, vab, qi, qb, kbf, kbb, vbf, vbb, tail, kid) = _proj_ab(
                xp, sc1p, sh1p, cos_p, sin_p, w_packed, bf_row, tm_p)
            r3 = lambda a: a.reshape(bsz, t, a.shape[-1])
            tail3 = r3(tail)
            cum = _logf_cumsum(tail3)
            b_out = _fox_prompt(r3(qb), r3(kbb), r3(vbb), cum, tq)
            a_out = _dsa_prompt(r3(qi), tail3, r3(kid), r3(qa), r3(kab), r3(vab), tq, min(TOPK, t // 4))
            xp = _out_ln(a_out.reshape(rows_p, _GW), b_out.reshape(rows_p, _GW), xp, g1p, w_out,
                         ln1_g[i], ln1_b[i], alpha, tm_p)
            heads = lambda a: a.reshape(bsz, t, N_HEADS, HEAD_DIM)
            outs_p["ak"].append(heads(kaf)); outs_p["av"].append(heads(vaf))
            outs_p["aki"].append(tail3[:, :, :D_IDX])
            outs_p["bk"].append(heads(kbf)); outs_p["bv"].append(heads(vbf))
            outs_p["blf"].append(tail3[:, :, _TAIL_F:_TAIL_F + N_HEADS])

            (qa, kaf, kab, vaf, vab, qi, qb, kbf, kbb, vbf, vbb, tail, kid) = _proj_ab(
                xs, sc1s, sh1s, cos_s, sin_s, w_packed, bf_row, rows_s)
            s3 = lambda a: a.reshape(db, tn, a.shape[-1])
            tail3 = s3(tail)
            hq = lambda a: a.reshape(db, tn, N_HEADS, HEAD_DIM).transpose(0, 2, 1, 3)
            qi_rows = hq(qi).reshape(db, N_IDX_HEADS * tn, D_IDX)
            w_hq = tail3[:, :, _TAIL_WI:_TAIL_WI + N_IDX_HEADS].transpose(0, 2, 1).reshape(db, N_IDX_HEADS * tn, 1)
            w_rows = jnp.broadcast_to(w_hq, (db, N_IDX_HEADS * tn, LANES))
            pad_keys = lambda a: jnp.pad(a, ((0, 0), (0, PAGE - tn), (0, 0)))
            ki_new = pad_keys(s3(kid)[:, :, :D_IDX])
            bias = _sample_indexer(page_table, qi_rows, w_rows, ki_new,
                                   cache_a_kidx[j], pages_per_step=8, ksel=min(TOPK, (past + tn) // 4))
            eye = jnp.eye(N_HEADS, dtype=BF16)
            bd = lambda a: jnp.einsum("bhqd,hg->bhqgd", hq(a), eye).reshape(db, N_HEADS * tn, _GW)
            lft_pool = cache_b_logf[j].transpose(0, 2, 1)
            lfn = jnp.pad(tail3[:, :, _TAIL_F:_TAIL_F + N_HEADS].transpose(0, 2, 1),
                          ((0, 0), (0, 0), (0, PAGE - tn)))
            pool = lambda cch: cch[j].reshape(cch.shape[1], PAGE, _GW)
            a_out, b_out = _sample_attn(page_table, bd(qa), bd(qb), bias, lft_pool, lfn,
                                        pad_keys(s3(kab)), pad_keys(s3(vab)), pad_keys(s3(kbb)), pad_keys(s3(vbb)),
                                        pool(cache_a_k), pool(cache_a_v), pool(cache_b_k), pool(cache_b_v))
            xs = _out_ln(a_out.reshape(rows_s, _GW), b_out.reshape(rows_s, _GW), xs, g1s, w_out,
                         ln1_g[i], ln1_b[i], alpha, rows_s)
            heads = lambda a: a.reshape(db, tn, N_HEADS, HEAD_DIM)
            outs_s["ak"].append(heads(kaf)); outs_s["av"].append(heads(vaf))
            outs_s["aki"].append(tail3[:, :, :D_IDX])
            outs_s["bk"].append(heads(kbf)); outs_s["bv"].append(heads(vbf))
            outs_s["blf"].append(tail3[:, :, _TAIL_F:_TAIL_F + N_HEADS])
        else:
            args = (w_in_c[j], lnv_g[j], lnv_b[j], w_spatial[j], b_spatial[j], w_out_c[j], ln1_g[i], ln1_b[i], alpha)
            (xp,) = _chunk_gmlp(xp, sc1p, sh1p, g1p, *args, tm_p, CHUNK, False)
            xs, cv = _chunk_gmlp(xs, sc1s, sh1s, g1s, *args, rows_s, tn, True)
            outs_s["cv"].append(cv.reshape(db, tn, -1))

        ffn_w = (w_up[i], w_conv[i], b_conv[i], w_down[i], ln2_g[i], ln2_b[i], alpha)
        xp, up_p = _conv_ffn(xp, sc2p, sh2p, g2p, jnp.zeros((bsz, SUBLANES, dff2), F32), *ffn_w, tm_p, t)
        outs_p["conv"].append(up_p.reshape(bsz, t // tm_p, SUBLANES, dff2)[:, -1, SUBLANES - (CONV_W - 1):])
        prev_s = jnp.pad(state_ffn_conv[i], ((0, 0), (SUBLANES - (CONV_W - 1), 0), (0, 0)))
        xs, up_s = _conv_ffn(xs, sc2s, sh2s, g2s, prev_s, *ffn_w, min(rows_s, 8 * tn), tn)
        outs_s["conv"].append(up_s.reshape(db, tn, dff2)[:, tn - (CONV_W - 1):])

    st = jnp.stack
    return (xp.reshape(bsz, t, d), xs.reshape(db, tn, d),
            st(outs_p["ak"]), st(outs_p["av"]), st(outs_p["aki"]), st(outs_p["bk"]), st(outs_p["bv"]),
            st(outs_p["blf"]), st(outs_p["conv"]),
            st(outs_s["ak"]), st(outs_s["av"]), st(outs_s["aki"]), st(outs_s["bk"]), st(outs_s["bv"]),
            st(outs_s["blf"]), st(outs_s["conv"]), st(outs_s["cv"]))
```

```python
import functools

import jax
import jax.numpy as jnp
from jax import lax
from jax.experimental import pallas as pl
from jax.experimental.pallas import tpu as pltpu

F32 = jnp.float32
BF16 = jnp.bfloat16

HEAD_DIM = 64
N_HEADS = 8
N_IDX_HEADS = 8
D_IDX = 64
TOPK = 256
PAGE = 128
CHUNK = 128
N_GROUPS_C = 8
CONV_W = 3
LN_EPS = 1e-5
ROPE_THETA = 10000.0
NEG = -1e30

LANES = 128
SUBLANES = 8
VMEM_LIMIT = 48 * 1024 * 1024

_NT = (((1,), (1,)), ((), ()))


def _cparams(n_axes):
    return pltpu.CompilerParams(dimension_semantics=("arbitrary",) * n_axes,
                                vmem_limit_bytes=VMEM_LIMIT)


def _const_spec(arr):
    return pl.BlockSpec(arr.shape, lambda i: (0,) * arr.ndim)


def _mod_spec(m, tm, nt):
    d = m.shape[2]
    if m.shape[1] == 1:
        tiles_per_group = nt // m.shape[0]
        return pl.BlockSpec((1, 1, d), lambda i: (i // tiles_per_group, 0, 0))
    return pl.BlockSpec((1, tm, d), lambda i: (0, i, 0))


def _ln(z, g, b):
    mu = jnp.mean(z, axis=-1, keepdims=True)
    zc = z - mu
    var = jnp.mean(zc * zc, axis=-1, keepdims=True)
    return zc * lax.rsqrt(var + LN_EPS) * g + b


def _gelu(x):
    return 0.5 * x * (1.0 + lax.erf(x * (2.0 ** -0.5)))


def _split3(x):
    hi = x.astype(BF16)
    r = x - hi.astype(F32)
    mid = r.astype(BF16)
    lo = (r - mid.astype(F32)).astype(BF16)
    return hi, mid, lo


def _dot3(x, m, left):
    parts = _split3(x)
    out = None
    for p in parts:
        d = (jnp.dot(m, p, preferred_element_type=F32) if left
             else jnp.dot(p, m, preferred_element_type=F32))
        out = d if out is None else out + d
    return out


def _mod_kernel(c_ref, w_ref, b_ref, o_ref):
    c = c_ref[...]
    a = (c * jax.nn.sigmoid(c)).astype(BF16)
    o_ref[0] = jnp.dot(a, w_ref[0].astype(BF16), preferred_element_type=F32) + b_ref[0]


def _adaln_mod(c_all, w_mod, b_mod):
    depth, d, n6 = w_mod.shape
    rows = c_all.shape[0]
    tn = n6 // 4
    return pl.pallas_call(
        _mod_kernel,
        grid=(depth, n6 // tn),
        in_specs=[pl.BlockSpec((rows, d), lambda l, j: (0, 0)),
                  pl.BlockSpec((1, d, tn), lambda l, j: (l, 0, j)),
                  pl.BlockSpec((1, 1, tn), lambda l, j: (l, 0, j))],
        out_specs=pl.BlockSpec((1, rows, tn), lambda l, j: (l, 0, j)),
        out_shape=jax.ShapeDtypeStruct((depth, rows, n6), F32),
        compiler_params=_cparams(2),
        name="adaln_mod",
    )(c_all, w_mod, b_mod.reshape(depth, 1, n6))


_G_QA, _G_KA, _G_VA, _G_QI, _G_QB, _G_KB, _G_VB = range(7)
_GW = N_HEADS * HEAD_DIM
_TAIL_OFF = 7 * _GW
_TAIL_WI = D_IDX
_TAIL_F = D_IDX + N_IDX_HEADS


def _pack_w_ab(w):
    sizes = (_GW, _GW, _GW, N_IDX_HEADS * D_IDX, D_IDX, N_IDX_HEADS, _GW, _GW, _GW, N_HEADS)
    offs, acc = [], 0
    for s in sizes[:-1]:
        acc += s
        offs.append(acc)
    qa, ka, va, qi, ki, wi, qb, kb, vb, fb = jnp.split(w, offs, axis=1)
    pad = jnp.zeros((w.shape[0], LANES - D_IDX - N_IDX_HEADS - N_HEADS), w.dtype)
    return jnp.concatenate([qa, ka, va, qi, qb, kb, vb, ki, wi, fb, pad], axis=1).astype(BF16)


def _rope_tables(pos):
    half = HEAD_DIM // 2
    inv = ROPE_THETA ** (-jnp.arange(half, dtype=F32) / half)
    ang = pos.astype(F32)[:, None] * inv[None, :]
    cos, sin = jnp.cos(ang), jnp.sin(ang)
    return (jnp.concatenate([cos] * 4, axis=1),
            jnp.concatenate([-sin, sin, -sin, sin], axis=1))


def _proj_ab_kernel(x_ref, sc_ref, sh_ref, cos_ref, sin_ref, w_ref, bf_ref,
                    qa_ref, kaf_ref, kab_ref, vaf_ref, vab_ref, qi_ref,
                    qb_ref, kbf_ref, kbb_ref, vbf_ref, vbb_ref, tail_ref, kid_ref):
    tm = x_ref.shape[0]
    h = (x_ref[...] * (1.0 + sc_ref[0]) + sh_ref[0]).astype(BF16)
    c1, s1 = cos_ref[...], sin_ref[...]
    c4 = jnp.concatenate([c1] * 4, axis=1)
    s4 = jnp.concatenate([s1] * 4, axis=1)
    lane4 = lax.broadcasted_iota(jnp.int32, (tm, _GW), 1)
    first4 = (lane4 & (HEAD_DIM - 1)) < HEAD_DIM // 2
    q_scale = HEAD_DIM ** -0.5

    def proj(g):
        return jnp.dot(h, w_ref[:, g * _GW:(g + 1) * _GW], preferred_element_type=F32)

    def rope(p):
        rot = jnp.where(first4, pltpu.roll(p, _GW - HEAD_DIM // 2, 1), pltpu.roll(p, HEAD_DIM // 2, 1))
        return p * c4 + rot * s4

    qa_ref[...] = (rope(proj(_G_QA)) * q_scale).astype(BF16)
    ka = rope(proj(_G_KA))
    kaf_ref[...] = ka
    kab_ref[...] = ka.astype(BF16)
    va = proj(_G_VA)
    vaf_ref[...] = va
    vab_ref[...] = va.astype(BF16)
    qi_ref[...] = (rope(proj(_G_QI)) * (D_IDX ** -0.5)).astype(BF16)
    qb_ref[...] = (proj(_G_QB) * q_scale).astype(BF16)
    kb = proj(_G_KB)
    kbf_ref[...] = kb
    kbb_ref[...] = kb.astype(BF16)
    vb = proj(_G_VB)
    vbf_ref[...] = vb
    vbb_ref[...] = vb.astype(BF16)

    t = jnp.dot(h, w_ref[:, _TAIL_OFF:_TAIL_OFF + LANES], preferred_element_type=F32)
    lane = lax.broadcasted_iota(jnp.int32, (tm, LANES), 1)
    first = (lane & (HEAD_DIM - 1)) < HEAD_DIM // 2
    rot = jnp.where(first, pltpu.roll(t, LANES - HEAD_DIM // 2, 1), pltpu.roll(t, HEAD_DIM // 2, 1))
    roped = t * c1 + rot * s1
    z = t + bf_ref[...]
    logf = jnp.minimum(z, 0.0) - jnp.log1p(jnp.exp(-jnp.abs(z)))
    tail = jnp.where(lane < D_IDX, roped,
                     jnp.where(lane < _TAIL_F, t * (N_IDX_HEADS ** -0.5),
                               jnp.where(lane < _TAIL_F + N_HEADS, logf, 0.0)))
    tail_ref[...] = tail
    kid_ref[...] = jnp.where(lane < D_IDX, roped, pltpu.roll(roped, D_IDX, 1)).astype(BF16)


def _proj_ab(x, sc, sh, cos_t, sin_t, w_packed, bf_row, tm):
    rows, d = x.shape
    nt = rows // tm
    t_tiles = cos_t.shape[0] // tm
    mod_spec = _mod_spec(sc, tm, nt)
    rope_spec = pl.BlockSpec((tm, LANES), lambda i: (i % t_tiles, 0))
    wide = lambda: pl.BlockSpec((tm, _GW), lambda i: (i, 0))
    narrow = lambda: pl.BlockSpec((tm, LANES), lambda i: (i, 0))
    f = jax.ShapeDtypeStruct((rows, _GW), F32)
    b = jax.ShapeDtypeStruct((rows, _GW), BF16)
    out_shape = (b, f, b, f, b, b, b, f, b, f, b,
                 jax.ShapeDtypeStruct((rows, LANES), F32), jax.ShapeDtypeStruct((rows, LANES), BF16))
    return pl.pallas_call(
        _proj_ab_kernel,
        grid=(nt,),
        in_specs=[pl.BlockSpec((tm, d), lambda i: (i, 0)), mod_spec, mod_spec, rope_spec, rope_spec,
                  _const_spec(w_packed), _const_spec(bf_row)],
        out_specs=tuple([wide() for _ in range(11)] + [narrow(), narrow()]),
        out_shape=out_shape,
        compiler_params=_cparams(1),
        name="proj_ab",
    )(x, sc, sh, cos_t, sin_t, w_packed, bf_row)


def _cumsum_kernel(t_ref, o_ref):
    n = t_ref.shape[1] // LANES
    r = lax.broadcasted_iota(jnp.int32, (LANES, LANES), 0)
    c = lax.broadcasted_iota(jnp.int32, (LANES, LANES), 1)
    tri = jnp.where(c <= r, 1.0, 0.0).astype(BF16)
    carry = jnp.zeros((1, LANES), F32)
    for k in range(n):
        x = t_ref[0, k * LANES:(k + 1) * LANES, :]
        cum = _dot3(x, tri, left=True) + carry
        carry = cum[LANES - 1:LANES, :]
        o_ref[0, :, k * LANES:(k + 1) * LANES] = cum.T[_TAIL_F:_TAIL_F + N_HEADS, :]


def _logf_cumsum(tail):
    bsz, t, _ = tail.shape
    return pl.pallas_call(
        _cumsum_kernel,
        grid=(bsz,),
        in_specs=[pl.BlockSpec((1, t, LANES), lambda b: (b, 0, 0))],
        out_specs=pl.BlockSpec((1, N_HEADS, t), lambda b: (b, 0, 0)),
        out_shape=jax.ShapeDtypeStruct((bsz, N_HEADS, t), F32),
        compiler_params=_cparams(1),
        name="logf_cumsum",
    )(tail)


def _pair_step(q0, q1, kc, vc, b0, b1, carry, lt64):
    m0, l0, m1, l1, acc = carry
    s0 = lax.dot_general(q0, kc, _NT, preferred_element_type=F32) + b0
    s1 = lax.dot_general(q1, kc, _NT, preferred_element_type=F32) + b1
    n0 = jnp.maximum(m0, jnp.max(s0, axis=1, keepdims=True))
    n1 = jnp.maximum(m1, jnp.max(s1, axis=1, keepdims=True))
    a0, a1 = jnp.exp(m0 - n0), jnp.exp(m1 - n1)
    p0, p1 = jnp.exp(s0 - n0), jnp.exp(s1 - n1)
    l0 = a0 * l0 + jnp.sum(p0, axis=1, keepdims=True)
    l1 = a1 * l1 + jnp.sum(p1, axis=1, keepdims=True)
    pv0 = jnp.dot(p0.astype(BF16), vc, preferred_element_type=F32)
    pv1 = jnp.dot(p1.astype(BF16), vc, preferred_element_type=F32)
    acc = jnp.where(lt64, a0 * acc + pv0, a1 * acc + pv1)
    return n0, l0, n1, l1, acc


def _pair_init(tq):
    col = lambda v: jnp.full((tq, 1), v, F32)
    return col(-jnp.inf), col(0.0), col(-jnp.inf), col(0.0), jnp.zeros((tq, LANES), F32)


def _pair_finish(carry, lt64):
    _, l0, _, l1, acc = carry
    return acc / jnp.where(lt64, l0, l1)


def _fox_kernel(q_ref, k_ref, v_ref, ck_ref, o_ref):
    tq = q_ref.shape[1]
    it = pl.program_id(2)
    q2 = q_ref[0]
    lt64 = lax.broadcasted_iota(jnp.int32, (tq, LANES), 1) < HEAD_DIM
    zero = jnp.zeros_like(q2)
    q0, q1 = jnp.where(lt64, q2, zero), jnp.where(lt64, zero, q2)
    row = lax.broadcasted_iota(jnp.int32, (tq, tq), 0)
    col = lax.broadcasted_iota(jnp.int32, (tq, tq), 1)

    def step(j, carry, diag):
        off = pl.multiple_of(j * tq, tq)
        kc = k_ref[0, pl.ds(off, tq), :]
        vc = v_ref[0, pl.ds(off, tq), :]
        ck = ck_ref[0, j, 0]
        b0, b1 = -ck[0:1, :], -ck[1:2, :]
        if diag:
            causal = col <= row
            b0, b1 = jnp.where(causal, b0, NEG), jnp.where(causal, b1, NEG)
        return _pair_step(q0, q1, kc, vc, b0, b1, carry, lt64)

    carry = lax.fori_loop(0, it, lambda j, c: step(j, c, False), _pair_init(tq))
    carry = step(it, carry, True)
    o_ref[0] = _pair_finish(carry, lt64).astype(o_ref.dtype)


def _fox_prompt(qb, kb, vb, cum, tq):
    bsz, t, _ = qb.shape
    nq = t // tq
    npair = N_HEADS // 2
    ck = cum.reshape(bsz, npair, 2, nq, tq).transpose(0, 3, 1, 2, 4)
    return pl.pallas_call(
        _fox_kernel,
        grid=(bsz, npair, nq),
        in_specs=[pl.BlockSpec((1, tq, LANES), lambda b, p, i: (b, i, p)),
                  pl.BlockSpec((1, t, LANES), lambda b, p, i: (b, 0, p)),
                  pl.BlockSpec((1, t, LANES), lambda b, p, i: (b, 0, p)),
                  pl.BlockSpec((1, nq, 1, 2, tq), lambda b, p, i: (b, 0, p, 0, 0))],
        out_specs=pl.BlockSpec((1, tq, LANES), lambda b, p, i: (b, i, p)),
        out_shape=jax.ShapeDtypeStruct((bsz, t, _GW), BF16),
        compiler_params=_cparams(3),
        name="fox_prompt",
    )(qb, kb, vb, ck)


def _topk_to_bias(sc_ref, nch, nvalid, ksel, n_bisect=24):
    rows, kc = sc_ref.shape[1], sc_ref.shape[2]
    kf = float(ksel)
    ninf = -jnp.inf
    col0 = lambda v: jnp.full((rows, 1), v, F32)

    static = isinstance(nch, int)

    def lanes(t):
        return jnp.broadcast_to(t, (rows, LANES))

    def over_chunks(elem, reduce0, combine, init):
        def part(s):
            acc = None
            for b in range(kc // LANES):
                e = elem(s[..., b * LANES:(b + 1) * LANES])
                if e.ndim == 3:
                    e = reduce0(e, axis=0)
                acc = e if acc is None else combine(acc, e)
            return acc
        if static:
            return part(sc_ref[...])
        return lax.fori_loop(0, nch, lambda j, a: combine(a, part(sc_ref[j])), jnp.full((rows, LANES), init, F32))

    def count(t, strict):
        tb = lanes(t)
        hit = (lambda x: jnp.where(x > tb, 1.0, 0.0)) if strict else (lambda x: jnp.where(x >= tb, 1.0, 0.0))
        return jnp.sum(over_chunks(hit, jnp.sum, jnp.add, 0.0), axis=1, keepdims=True)

    def count_ge(t):
        return count(t, False)

    def max_below(h):
        hb = lanes(h)
        part = over_chunks(lambda x: jnp.where(x < hb, x, ninf), jnp.max, jnp.maximum, ninf)
        return jnp.max(part, axis=1, keepdims=True)

    def min_valid():
        part = over_chunks(lambda x: jnp.where(x == ninf, jnp.inf, x), jnp.min, jnp.minimum, jnp.inf)
        return jnp.min(part, axis=1, keepdims=True)

    small = nvalid <= kf
    lo = min_valid()
    v = max_below(col0(jnp.inf))
    ok = count_ge(v) >= kf
    res = jnp.where(jnp.logical_or(small, ok), 1.0, 0.0)
    thr = jnp.where(small, ninf, v)
    hi = jnp.where(res > 0.0, jnp.inf, v)

    def bisect(_, st):
        lo, hi = st
        mid = 0.5 * (lo + hi)
        ge = count_ge(mid) >= kf
        return jnp.where(ge, mid, lo), jnp.where(ge, hi, mid)

    lo, hi = lax.fori_loop(0, n_bisect, bisect, (lo, hi))

    def snap_cond(st):
        return jnp.min(st[2]) < 1.0

    def snap(st):
        thr, hi, res = st
        v = max_below(hi)
        ok = count_ge(v) >= kf
        open_ = res < 1.0
        new = jnp.logical_and(open_, ok)
        drop = jnp.logical_and(open_, jnp.logical_not(ok))
        return jnp.where(new, v, thr), jnp.where(drop, v, hi), jnp.where(ok, 1.0, res)

    thr, hi, res = lax.while_loop(snap_cond, snap, (thr, hi, res))

    need = kf - count(thr, True)
    r = lax.broadcasted_iota(jnp.int32, (kc, kc), 0)
    c = lax.broadcasted_iota(jnp.int32, (kc, kc), 1)
    upper = jnp.where(r < c, 1.0, 0.0).astype(BF16)

    def to_bias(j, run):
        s = sc_ref[j]
        tie = jnp.logical_and(s == thr, s > ninf)
        tief = jnp.where(tie, 1.0, 0.0)
        before = jnp.dot(tief.astype(BF16), upper, preferred_element_type=F32) + run
        sel = jnp.logical_or(s > thr, jnp.logical_and(tie, before < need))
        sc_ref[j] = jnp.where(sel, 0.0, NEG)
        return run + jnp.sum(tief, axis=1, keepdims=True)

    lax.fori_loop(0, nch, to_bias, col0(0.0))


def _dsa_kernel(qi_ref, w_ref, kid_ref, qa_ref, ka_ref, va_ref, o_ref, sc_ref, *, ksel):
    tq = qi_ref.shape[1]
    it = pl.program_id(1)
    lane = lax.broadcasted_iota(jnp.int32, (tq, LANES), 1)
    lt64 = lane < HEAD_DIM
    row = lax.broadcasted_iota(jnp.int32, (tq, tq), 0)
    col = lax.broadcasted_iota(jnp.int32, (tq, tq), 1)

    def halves(x2):
        zero = jnp.zeros_like(x2)
        return jnp.where(lt64, x2, zero), jnp.where(lt64, zero, x2)

    qi = qi_ref[0]
    qm = []
    for p in range(N_IDX_HEADS // 2):
        qm.extend(halves(qi[:, p * LANES:(p + 1) * LANES]))
    wt = w_ref[0]
    wcol = [wt[:, _TAIL_WI + h:_TAIL_WI + h + 1] for h in range(N_IDX_HEADS)]

    def score_chunk(j, diag):
        off = pl.multiple_of(j * tq, tq)
        kc = kid_ref[0, pl.ds(off, tq), :]
        sc = jnp.zeros((tq, tq), F32)
        for h in range(N_IDX_HEADS):
            lg = lax.dot_general(qm[h], kc, _NT, preferred_element_type=F32)
            sc = sc + jnp.maximum(lg, 0.0) * wcol[h]
        if diag:
            sc = jnp.where(col <= row, sc, -jnp.inf)
        sc_ref[j] = sc

    def score_body(j, c):
        score_chunk(j, False)
        return c

    lax.fori_loop(0, it, score_body, 0)
    score_chunk(it, True)

    nvalid = (it * tq + 1 + lax.broadcasted_iota(jnp.int32, (tq, 1), 0)).astype(F32)
    _topk_to_bias(sc_ref, it + 1, nvalid, ksel)

    for p in range(N_HEADS // 2):
        q0, q1 = halves(qa_ref[0, :, p * LANES:(p + 1) * LANES])

        def step(j, carry, p=p, q0=q0, q1=q1):
            off = pl.multiple_of(j * tq, tq)
            kc = ka_ref[0, pl.ds(off, tq), p * LANES:(p + 1) * LANES]
            vc = va_ref[0, pl.ds(off, tq), p * LANES:(p + 1) * LANES]
            bias = sc_ref[j]
            return _pair_step(q0, q1, kc, vc, bias, bias, carry, lt64)

        carry = lax.fori_loop(0, it + 1, step, _pair_init(tq))
        o_ref[0, :, p * LANES:(p + 1) * LANES] = _pair_finish(carry, lt64).astype(o_ref.dtype)


def _dsa_prompt(qi, tail, kid, qa, ka, va, tq, ksel):
    bsz, t, _ = qa.shape
    nq = t // tq
    q_spec = lambda w: pl.BlockSpec((1, tq, w), lambda b, i: (b, i, 0))
    full = lambda w: pl.BlockSpec((1, t, w), lambda b, i: (b, 0, 0))
    return pl.pallas_call(
        functools.partial(_dsa_kernel, ksel=ksel),
        grid=(bsz, nq),
        in_specs=[q_spec(_GW), q_spec(LANES), full(LANES), q_spec(_GW), full(_GW), full(_GW)],
        out_specs=q_spec(_GW),
        out_shape=jax.ShapeDtypeStruct((bsz, t, _GW), BF16),
        scratch_shapes=[pltpu.VMEM((nq, tq, tq), F32)],
        compiler_params=_cparams(2),
        name="dsa_prompt",
    )(qi, tail, kid, qa, ka, va)


def _sample_idx_kernel(pt_ref, qi_ref, w_ref, knew_ref, *rest, pages_per_step, ksel):
    page_refs = rest[:pages_per_step]
    o_ref = rest[pages_per_step]
    sc_ref = rest[pages_per_step + 1]
    s = pl.program_id(1)
    nsteps = pl.num_programs(1)
    npg = sc_ref.shape[0] - 1
    tn = sc_ref.shape[1]
    qi = qi_ref[0]
    w = w_ref[0]

    def scores(keys_t):
        lg = jnp.dot(qi, keys_t, preferred_element_type=F32)
        z = jnp.maximum(lg, 0.0) * w
        out = z[0:tn]
        for h in range(1, N_IDX_HEADS):
            out = out + z[h * tn:(h + 1) * tn]
        return out

    for i, pr in enumerate(page_refs):
        sc_ref[s * pages_per_step + i] = scores(pr[0].astype(BF16))

    @pl.when(s == nsteps - 1)
    def _():
        r = lax.broadcasted_iota(jnp.int32, (tn, PAGE), 0)
        c = lax.broadcasted_iota(jnp.int32, (tn, PAGE), 1)
        sc_ref[npg] = jnp.where(c <= r, scores(knew_ref[0]), -jnp.inf)
        nvalid = (npg * PAGE + 1 + lax.broadcasted_iota(jnp.int32, (tn, 1), 0)).astype(F32)
        _topk_to_bias(sc_ref, npg + 1, nvalid, ksel)
        o_ref[0] = sc_ref[...]


def _sample_indexer(page_table, qi_rows, w_rows, ki_new_pad, kidx_pool, pages_per_step, ksel):
    db, npg = page_table.shape
    tn = qi_rows.shape[1] // N_IDX_HEADS
    nsteps = npg // pages_per_step

    def page_spec(i):
        return pl.BlockSpec((1, D_IDX, PAGE), lambda b, s, pt: (pt[b, s * pages_per_step + i], 0, 0))

    grid_spec = pltpu.PrefetchScalarGridSpec(
        num_scalar_prefetch=1,
        grid=(db, nsteps),
        in_specs=[pl.BlockSpec((1,) + qi_rows.shape[1:], lambda b, s, pt: (b, 0, 0)),
                  pl.BlockSpec((1,) + w_rows.shape[1:], lambda b, s, pt: (b, 0, 0)),
                  pl.BlockSpec((1, D_IDX, PAGE), lambda b, s, pt: (b, 0, 0))]
                 + [page_spec(i) for i in range(pages_per_step)],
        out_specs=pl.BlockSpec((1, npg + 1, tn, PAGE), lambda b, s, pt: (b, 0, 0, 0)),
        scratch_shapes=[pltpu.VMEM((npg + 1, tn, PAGE), F32)],
    )
    return pl.pallas_call(
        functools.partial(_sample_idx_kernel, pages_per_step=pages_per_step, ksel=ksel),
        grid_spec=grid_spec,
        out_shape=jax.ShapeDtypeStruct((db, npg + 1, tn, PAGE), F32),
        compiler_params=_cparams(2),
        name="sample_indexer",
    )(page_table, qi_rows, w_rows, ki_new_pad, *([kidx_pool] * pages_per_step))


def _sample_attn_kernel(pt_ref, qa_ref, qb_ref, bias_ref, biasn_ref, lfn_ref,
                        kan_ref, van_ref, kbn_ref, vbn_ref, *rest, pps):
    lft_refs, ka_refs, va_refs, kb_refs, vb_refs = (rest[i * pps:(i + 1) * pps] for i in range(5))
    oa_ref, ob_ref, ma, la, acca, mb, lb, accb, run = rest[5 * pps:]
    p = pl.program_id(1)
    nsteps = pl.num_programs(1)
    rows = qa_ref.shape[1]
    tn = rows // N_HEADS

    @pl.when(p == 0)
    def _():
        for m in (ma, mb):
            m[...] = jnp.full(m.shape, -jnp.inf, F32)
        for z in (la, lb, acca, accb, run):
            z[...] = jnp.zeros(z.shape, F32)

    r = lax.broadcasted_iota(jnp.int32, (PAGE, PAGE), 0)
    c = lax.broadcasted_iota(jnp.int32, (PAGE, PAGE), 1)
    incl = jnp.where(r <= c, 1.0, 0.0).astype(BF16)

    def update(q, kts, vts, biases, m_ref, l_ref, acc_ref):
        s = [jnp.dot(q, kt, preferred_element_type=F32) + b for kt, b in zip(kts, biases)]
        smax = functools.reduce(jnp.maximum, s)
        m_old = m_ref[...]
        m_new = jnp.maximum(m_old, jnp.max(smax, axis=1, keepdims=True))
        a = jnp.exp(m_old - m_new)
        ps = [jnp.exp(x - m_new) for x in s]
        l_ref[...] = a * l_ref[...] + jnp.sum(functools.reduce(jnp.add, ps), axis=1, keepdims=True)
        pv = [lax.dot_general(pr.astype(BF16), vt, _NT, preferred_element_type=F32) for pr, vt in zip(ps, vts)]
        acc_ref[...] = a * acc_ref[...] + functools.reduce(jnp.add, pv)
        m_ref[...] = m_new

    def head_rows(x):
        return jnp.concatenate([jnp.broadcast_to(x[h:h + 1], (tn, x.shape[1])) for h in range(N_HEADS)], axis=0)

    def tile_rows(x):
        return jnp.concatenate([x] * N_HEADS, axis=0)

    def forget_biases(lf_ts):
        out, run_v = [], run[...]
        for lf_t in lf_ts:
            cum = _dot3(head_rows(lf_t), incl, left=False) + run_v
            run_v = cum[:, PAGE - 1:PAGE]
            out.append(-cum)
        run[...] = run_v
        return out

    cast = lambda refs: [x[0].astype(BF16) for x in refs]
    update(qa_ref[0], cast(ka_refs), cast(va_refs), [tile_rows(bias_ref[0, i]) for i in range(pps)], ma, la, acca)
    update(qb_ref[0], cast(kb_refs), cast(vb_refs), forget_biases([x[0] for x in lft_refs]), mb, lb, accb)

    @pl.when(p == nsteps - 1)
    def _():
        update(qa_ref[0], [kan_ref[0]], [van_ref[0]], [tile_rows(biasn_ref[0, 0])], ma, la, acca)
        qrow = lax.broadcasted_iota(jnp.int32, (rows, PAGE), 0) & (tn - 1)
        key = lax.broadcasted_iota(jnp.int32, (rows, PAGE), 1)
        fb = jnp.where(key <= qrow, forget_biases([lfn_ref[0]])[0], NEG)
        update(qb_ref[0], [kbn_ref[0]], [vbn_ref[0]], [fb], mb, lb, accb)
        lane = lax.broadcasted_iota(jnp.int32, (tn, _GW), 1)
        for l_ref, acc_ref, o_ref in ((la, acca, oa_ref), (lb, accb, ob_ref)):
            full = acc_ref[...] / l_ref[...]
            out = jnp.zeros((tn, _GW), F32)
            for h in range(N_HEADS):
                mine = jnp.logical_and(lane >= h * HEAD_DIM, lane < (h + 1) * HEAD_DIM)
                out = jnp.where(mine, full[h * tn:(h + 1) * tn], out)
            o_ref[0] = out.astype(o_ref.dtype)


def _sample_attn(page_table, qa_bd, qb_bd, bias, lft_pool, lfn, kan, van, kbn, vbn,
                 ka_pool, va_pool, kb_pool, vb_pool, pps):
    db, npg = page_table.shape
    rows = qa_bd.shape[1]
    tn = rows // N_HEADS
    per_b = lambda shp: pl.BlockSpec((1,) + shp, lambda b, p, pt: (b,) + (0,) * len(shp))

    def paged(shp):
        return [pl.BlockSpec((1,) + shp, lambda b, p, pt, i=i: (pt[b, p * pps + i],) + (0,) * len(shp))
                for i in range(pps)]

    grid_spec = pltpu.PrefetchScalarGridSpec(
        num_scalar_prefetch=1,
        grid=(db, npg // pps),
        in_specs=[per_b((rows, _GW)), per_b((rows, _GW)),
                  pl.BlockSpec((1, pps, tn, PAGE), lambda b, p, pt: (b, p, 0, 0)),
                  pl.BlockSpec((1, 1, tn, PAGE), lambda b, p, pt: (b, npg, 0, 0)),
                  per_b((N_HEADS, PAGE)),
                  per_b((_GW, PAGE)), per_b((_GW, PAGE)), per_b((_GW, PAGE)), per_b((_GW, PAGE))]
                 + paged((N_HEADS, PAGE)) + paged((_GW, PAGE)) + paged((_GW, PAGE))
                 + paged((_GW, PAGE)) + paged((_GW, PAGE)),
        out_specs=(per_b((tn, _GW)), per_b((tn, _GW))),
        scratch_shapes=[pltpu.VMEM((rows, 1), F32), pltpu.VMEM((rows, 1), F32), pltpu.VMEM((rows, _GW), F32),
                        pltpu.VMEM((rows, 1), F32), pltpu.VMEM((rows, 1), F32), pltpu.VMEM((rows, _GW), F32),
                        pltpu.VMEM((rows, 1), F32)],
    )
    o = jax.ShapeDtypeStruct((db, tn, _GW), BF16)
    rep = lambda a: [a] * pps
    return pl.pallas_call(
        functools.partial(_sample_attn_kernel, pps=pps),
        grid_spec=grid_spec,
        out_shape=(o, o),
        compiler_params=_cparams(2),
        name="sample_attn",
    )(page_table, qa_bd, qb_bd, bias, bias, lfn, kan, van, kbn, vbn,
      *rep(lft_pool), *rep(ka_pool), *rep(va_pool), *rep(kb_pool), *rep(vb_pool))


def _out_ln_kernel(a_ref, b_ref, x_ref, g_ref, wa_ref, wb_ref, lg_ref, lb_ref, o_ref, *, alpha):
    mix = (jnp.dot(a_ref[...], wa_ref[...], preferred_element_type=F32)
           + jnp.dot(b_ref[...], wb_ref[...], preferred_element_type=F32))
    z = alpha * x_ref[...] + (1.0 + g_ref[0]) * mix
    o_ref[...] = _ln(z, lg_ref[...], lb_ref[...])


def _out_ln(a, b, x, gate, w_out, ln_g, ln_b, alpha, tm):
    rows, d = x.shape
    nt = rows // tm
    ka = a.shape[1]
    wa, wb = w_out[:ka].astype(BF16), w_out[ka:].astype(BF16)
    const = _const_spec
    lg, lb = ln_g.reshape(1, d), ln_b.reshape(1, d)
    return pl.pallas_call(
        functools.partial(_out_ln_kernel, alpha=alpha),
        grid=(nt,),
        in_specs=[pl.BlockSpec((tm, ka), lambda i: (i, 0)), pl.BlockSpec((tm, b.shape[1]), lambda i: (i, 0)),
                  pl.BlockSpec((tm, d), lambda i: (i, 0)),
                  _mod_spec(gate, tm, nt),
                  const(wa), const(wb), const(lg), const(lb)],
        out_specs=pl.BlockSpec((tm, d), lambda i: (i, 0)),
        out_shape=jax.ShapeDtypeStruct((rows, d), F32),
        compiler_params=_cparams(1),
        name="out_ln",
    )(a, b, x, gate, wa, wb, lg, lb)


_FF_CW = 256


def _ffn_kernel(x_ref, sc_ref, sh_ref, g_ref, prev_ref, wg_ref, wv_ref, cw_ref, wd_ref, lg_ref, lb_ref,
                o_ref, up_ref, bufg, bufv, carry, *, alpha, nseq, tiles_per_seq):
    tm, d = x_ref.shape
    ts = tm // nseq
    nch, _, cw = wg_ref.shape
    dff = nch * cw
    x = x_ref[...]
    h = (x * (1.0 + sc_ref[0]) + sh_ref[0]).astype(BF16)
    chained = tiles_per_seq > 1

    if chained:
        @pl.when(pl.program_id(0) % tiles_per_seq == 0)
        def _():
            carry[...] = prev_ref[0]

    acc = jnp.zeros((tm, d), F32)
    for c in range(nch):
        cwr = cw_ref[c]
        ys = []
        for half, (w_ref, buf) in enumerate(((wg_ref, bufg), (wv_ref, bufv))):
            lo = half * dff + c * cw
            u = jnp.dot(h, w_ref[c], preferred_element_type=F32)
            if chained:
                buf[0, 0:SUBLANES, :] = carry[:, lo:lo + cw]
                carry[:, lo:lo + cw] = u[tm - SUBLANES:, :]
                up_ref[:, lo:lo + cw] = u[tm - SUBLANES:, :]
            else:
                buf[:, 0:SUBLANES, :] = prev_ref[:, :, lo:lo + cw]
                up_ref[:, lo:lo + cw] = u
            buf[:, SUBLANES:SUBLANES + ts, :] = u.reshape(nseq, ts, cw)
            um1 = buf[:, SUBLANES - 1:SUBLANES - 1 + ts, :].reshape(tm, cw)
            um2 = buf[:, SUBLANES - 2:SUBLANES - 2 + ts, :].reshape(tm, cw)
            k0 = 4 * half
            y = cwr[k0 + 3:k0 + 4, :] + cwr[k0:k0 + 1, :] * um2
            y = y + cwr[k0 + 1:k0 + 2, :] * um1
            ys.append(y + cwr[k0 + 2:k0 + 3, :] * u)
        act = (_gelu(ys[0]) * ys[1]).astype(BF16)
        acc = acc + jnp.dot(act, wd_ref[c], preferred_element_type=F32)

    z = alpha * x + (1.0 + g_ref[0]) * acc
    o_ref[...] = _ln(z, lg_ref[...], lb_ref[...])


def _conv_ffn(x, sc, sh, gate, prev8, w_up, w_conv, b_conv, w_down, ln_g, ln_b, alpha, tm, seq_len):
    rows, d = x.shape
    dff = w_down.shape[0]
    nch = dff // _FF_CW
    nt = rows // tm
    if seq_len >= tm:
        nseq, tiles_per_seq = 1, seq_len // tm
        prev_spec = pl.BlockSpec((1, SUBLANES, 2 * dff), lambda i: (i // tiles_per_seq, 0, 0))
        up_rows, up_spec = nt * SUBLANES, pl.BlockSpec((SUBLANES, 2 * dff), lambda i: (i, 0))
    else:
        nseq, tiles_per_seq = tm // seq_len, 1
        prev_spec = pl.BlockSpec((nseq, SUBLANES, 2 * dff), lambda i: (i, 0, 0))
        up_rows, up_spec = rows, pl.BlockSpec((tm, 2 * dff), lambda i: (i, 0))
    ts = tm // nseq
    chunked = lambda w: w.reshape(w.shape[0], nch, _FF_CW).transpose(1, 0, 2)
    wg = chunked(w_up[:, :dff]).astype(BF16)
    wv = chunked(w_up[:, dff:]).astype(BF16)
    wd = w_down.reshape(nch, _FF_CW, d).astype(BF16)
    conv_rows = jnp.concatenate([w_conv[:, :dff], b_conv[None, :dff], w_conv[:, dff:], b_conv[None, dff:]], axis=0)
    cwr = chunked(conv_rows)
    lg, lb = ln_g.reshape(1, d), ln_b.reshape(1, d)
    const = _const_spec
    mod_spec = lambda m: _mod_spec(m, tm, nt)
    return pl.pallas_call(
        functools.partial(_ffn_kernel, alpha=alpha, nseq=nseq, tiles_per_seq=tiles_per_seq),
        grid=(nt,),
        in_specs=[pl.BlockSpec((tm, d), lambda i: (i, 0)), mod_spec(sc), mod_spec(sh), mod_spec(gate), prev_spec,
                  const(wg), const(wv), const(cwr), const(wd), const(lg), const(lb)],
        out_specs=(pl.BlockSpec((tm, d), lambda i: (i, 0)), up_spec),
        out_shape=(jax.ShapeDtypeStruct((rows, d), F32), jax.ShapeDtypeStruct((up_rows, 2 * dff), F32)),
        scratch_shapes=[pltpu.VMEM((nseq, ts + SUBLANES, _FF_CW), F32),
                        pltpu.VMEM((nseq, ts + SUBLANES, _FF_CW), F32),
                        pltpu.VMEM((SUBLANES, 2 * dff), F32)],
        compiler_params=_cparams(1),
        name="conv_ffn",
    )(x, sc, sh, gate, prev8, wg, wv, cwr, wd, lg, lb)


def _gmlp_kernel(x_ref, sc_ref, sh_ref, g_ref, wv_ref, wu_ref, lvg_ref, lvb_ref, wmix_ref, bs_ref, wo_ref,
                 lg_ref, lb_ref, o_ref, *maybe_v_ref, alpha):
    tm, d = x_ref.shape
    ng, _, gw = wu_ref.shape
    x = x_ref[...]
    h = (x * (1.0 + sc_ref[0]) + sh_ref[0]).astype(BF16)
    v = _ln(_gelu(jnp.dot(h, wv_ref[...], preferred_element_type=F32)), lvg_ref[...], lvb_ref[...])
    if maybe_v_ref:
        maybe_v_ref[0][...] = v
    vb = v.astype(BF16)
    bs = bs_ref[...]
    acc = jnp.zeros((tm, d), F32)
    for g in range(ng):
        u = _gelu(jnp.dot(h, wu_ref[g], preferred_element_type=F32))
        mixed = jnp.dot(wmix_ref[g], vb[:, g * gw:(g + 1) * gw], preferred_element_type=F32) + bs[:, g:g + 1]
        acc = acc + jnp.dot((u * mixed).astype(BF16), wo_ref[g], preferred_element_type=F32)
    z = alpha * x + (1.0 + g_ref[0]) * acc
    o_ref[...] = _ln(z, lg_ref[...], lb_ref[...])


def _chunk_gmlp(x, sc, sh, gate, w_in, lnv_g, lnv_b, w_spatial, b_spatial, w_out, ln_g, ln_b,
                alpha, tm, chunk_len, emit_v):
    rows, d = x.shape
    dcg = w_out.shape[0]
    ng = w_spatial.shape[0]
    gw = dcg // ng
    nt = rows // tm
    wu = w_in[:, :dcg].reshape(d, ng, gw).transpose(1, 0, 2).astype(BF16)
    wv = w_in[:, dcg:].astype(BF16)
    wo = w_out.reshape(ng, gw, d).astype(BF16)
    tri = (jnp.arange(chunk_len)[:, None] >= jnp.arange(chunk_len)[None, :]).astype(w_spatial.dtype)
    wc = w_spatial[:, :chunk_len, :chunk_len] * tri[None]
    eye = jnp.eye(tm // chunk_len, dtype=w_spatial.dtype)
    wmix = jnp.einsum("ab,gts->gatbs", eye, wc).reshape(ng, tm, tm).astype(BF16)
    bs = jnp.tile(b_spatial[:, :chunk_len].T, (tm // chunk_len, 1))
    bs = jnp.pad(bs, ((0, 0), (0, LANES - ng)))
    lvg, lvb = lnv_g.reshape(1, dcg), lnv_b.reshape(1, dcg)
    lg, lb = ln_g.reshape(1, d), ln_b.reshape(1, d)
    const = _const_spec
    mod_spec = lambda m: _mod_spec(m, tm, nt)
    row_spec = lambda w: pl.BlockSpec((tm, w), lambda i: (i, 0))
    out_specs, out_shape = [row_spec(d)], [jax.ShapeDtypeStruct((rows, d), F32)]
    if emit_v:
        out_specs.append(row_spec(dcg))
        out_shape.append(jax.ShapeDtypeStruct((rows, dcg), F32))
    return pl.pallas_call(
        functools.partial(_gmlp_kernel, alpha=alpha),
        grid=(nt,),
        in_specs=[row_spec(d), mod_spec(sc), mod_spec(sh), mod_spec(gate), const(wv), const(wu), const(lvg),
                  const(lvb), const(wmix), const(bs), const(wo), const(lg), const(lb)],
        out_specs=tuple(out_specs),
        out_shape=tuple(out_shape),
        compiler_params=_cparams(1),
        name="chunk_gmlp",
    )(x, sc, sh, gate, wv, wu, lvg, lvb, wmix, bs, wo, lg, lb)


def kernel(x_prompt, x_sample, cache_a_k, cache_a_v, cache_a_kidx, cache_b_k, cache_b_v, cache_b_logf,
           state_ffn_conv, page_table, c_prompt, c_sample, w_mod, b_mod, ln1_g, ln1_b, ln2_g, ln2_b,
           w_in_ab, b_forget, w_out_ab, w_in_c, lnv_g, lnv_b, w_spatial, b_spatial, w_out_c,
           w_up, w_conv, b_conv, w_down):
    bsz, t, d = x_prompt.shape
    db, tn, _ = x_sample.shape
    depth = w_mod.shape[0]
    dff2 = w_up.shape[2]
    npg = page_table.shape[1]
    past = npg * PAGE
    alpha = (2 * depth) ** 0.25
    rows_p, rows_s = bsz * t, db * tn
    tm_p = 256
    tq = 256

    nc = bsz + db
    c_all = jnp.concatenate([c_prompt, c_sample, jnp.zeros((-nc % SUBLANES, d), F32)], axis=0)
    mod = _adaln_mod(c_all, w_mod, b_mod)

    xp = x_prompt.reshape(rows_p, d)
    xs = x_sample.reshape(rows_s, d)
    cos_p, sin_p = _rope_tables(jnp.arange(t, dtype=jnp.int32))
    cos_s, sin_s = _rope_tables(jnp.tile(past + jnp.arange(tn, dtype=jnp.int32), db))

    outs_p = {k: [] for k in ("ak", "av", "aki", "bk", "bv", "blf", "conv")}
    outs_s = {k: [] for k in ("ak", "av", "aki", "bk", "bv", "blf", "conv", "cv")}

    for i in range(depth):
        j = i // 2
        mp = mod[i, :bsz].reshape(bsz, 6, 1, d)
        ms = jnp.repeat(mod[i, bsz:nc].reshape(db, 6, d), tn, axis=0).reshape(1, rows_s, 6, d)
        sh1p, sc1p, g1p, sh2p, sc2p, g2p = (mp[:, k] for k in range(6))
        sh1s, sc1s, g1s, sh2s, sc2s, g2s = (ms[:, :, k] for k in range(6))

        if i % 2 == 0:
            w_packed = _pack_w_ab(w_in_ab[j])
            bf_row = jnp.zeros((1, LANES), F32).at[0, _TAIL_F:_TAIL_F + N_HEADS].set(b_forget[j])
            w_out = w_out_ab[j]

            (qa, kaf, kab, vaf, vab, qi, qb, kbf, kbb, vbf, vbb, tail, kid) = _proj_ab(
                xp, sc1p, sh1p, cos_p, sin_p, w_packed, bf_row, tm_p)
            r3 = lambda a: a.reshape(bsz, t, a.shape[-1])
            tail3 = r3(tail)
            cum = _logf_cumsum(tail3)
            b_out = _fox_prompt(r3(qb), r3(kbb), r3(vbb), cum, tq)
            a_out = _dsa_prompt(r3(qi), tail3, r3(kid), r3(qa), r3(kab), r3(vab), tq, min(TOPK, t // 4))
            xp = _out_ln(a_out.reshape(rows_p, _GW), b_out.reshape(rows_p, _GW), xp, g1p, w_out,
                         ln1_g[i], ln1_b[i], alpha, tm_p)
            heads = lambda a: a.reshape(bsz, t, N_HEADS, HEAD_DIM)
            outs_p["ak"].append(heads(kaf)); outs_p["av"].append(heads(vaf))
            outs_p["aki"].append(tail3[:, :, :D_IDX])
            outs_p["bk"].append(heads(kbf)); outs_p["bv"].append(heads(vbf))
            outs_p["blf"].append(tail3[:, :, _TAIL_F:_TAIL_F + N_HEADS])

            (qa, kaf, kab, vaf, vab, qi, qb, kbf, kbb, vbf, vbb, tail, kid) = _proj_ab(
                xs, sc1s, sh1s, cos_s, sin_s, w_packed, bf_row, rows_s)
            s3 = lambda a: a.reshape(db, tn, a.shape[-1])
            tail3 = s3(tail)
            hq = lambda a: a.reshape(db, tn, N_HEADS, HEAD_DIM).transpose(0, 2, 1, 3)
            qi_rows = hq(qi).reshape(db, N_IDX_HEADS * tn, D_IDX)
            w_hq = tail3[:, :, _TAIL_WI:_TAIL_WI + N_IDX_HEADS].transpose(0, 2, 1).reshape(db, N_IDX_HEADS * tn, 1)
            w_rows = jnp.broadcast_to(w_hq, (db, N_IDX_HEADS * tn, LANES))
            keys_t = lambda a: jnp.pad(a.transpose(0, 2, 1), ((0, 0), (0, 0), (0, PAGE - tn)))
            ki_new = keys_t(s3(kid)[:, :, :D_IDX])
            bias = _sample_indexer(page_table, qi_rows, w_rows, ki_new, cache_a_kidx[j].transpose(0, 2, 1),
                                   pages_per_step=min(npg, 16), ksel=min(TOPK, (past + tn) // 4))
            eye = jnp.eye(N_HEADS, dtype=BF16)
            bd = lambda a: jnp.einsum("bhqd,hg->bhqgd", hq(a), eye).reshape(db, N_HEADS * tn, _GW)
            lft_pool = cache_b_logf[j].transpose(0, 2, 1)
            lfn = jnp.pad(tail3[:, :, _TAIL_F:_TAIL_F + N_HEADS].transpose(0, 2, 1),
                          ((0, 0), (0, 0), (0, PAGE - tn)))
            pool = lambda cch: cch[j].transpose(0, 2, 3, 1).reshape(cch.shape[1], _GW, PAGE)
            a_out, b_out = _sample_attn(page_table, bd(qa), bd(qb), bias, lft_pool, lfn,
                                        keys_t(s3(kab)), keys_t(s3(vab)), keys_t(s3(kbb)), keys_t(s3(vbb)),
                                        pool(cache_a_k), pool(cache_a_v), pool(cache_b_k), pool(cache_b_v),
                                        pps=min(npg, 8))
            xs = _out_ln(a_out.reshape(rows_s, _GW), b_out.reshape(rows_s, _GW), xs, g1s, w_out,
                         ln1_g[i], ln1_b[i], alpha, rows_s)
            heads = lambda a: a.reshape(db, tn, N_HEADS, HEAD_DIM)
            outs_s["ak"].append(heads(kaf)); outs_s["av"].append(heads(vaf))
            outs_s["aki"].append(tail3[:, :, :D_IDX])
            outs_s["bk"].append(heads(kbf)); outs_s["bv"].append(heads(vbf))
            outs_s["blf"].append(tail3[:, :, _TAIL_F:_TAIL_F + N_HEADS])
        else:
            args = (w_in_c[j], lnv_g[j], lnv_b[j], w_spatial[j], b_spatial[j], w_out_c[j], ln1_g[i], ln1_b[i], alpha)
            (xp,) = _chunk_gmlp(xp, sc1p, sh1p, g1p, *args, tm_p, CHUNK, False)
            xs, cv = _chunk_gmlp(xs, sc1s, sh1s, g1s, *args, rows_s, tn, True)
            outs_s["cv"].append(cv.reshape(db, tn, -1))

        ffn_w = (w_up[i], w_conv[i], b_conv[i], w_down[i], ln2_g[i], ln2_b[i], alpha)
        xp, up_p = _conv_ffn(xp, sc2p, sh2p, g2p, jnp.zeros((bsz, SUBLANES, dff2), F32), *ffn_w, tm_p, t)
        outs_p["conv"].append(up_p.reshape(bsz, t // tm_p, SUBLANES, dff2)[:, -1, SUBLANES - (CONV_W - 1):])
        prev_s = jnp.pad(state_ffn_conv[i], ((0, 0), (SUBLANES - (CONV_W - 1), 0), (0, 0)))
        xs, up_s = _conv_ffn(xs, sc2s, sh2s, g2s, prev_s, *ffn_w, min(rows_s, 8 * tn), tn)
        outs_s["conv"].append(up_s.reshape(db, tn, dff2)[:, tn - (CONV_W - 1):])

    st = jnp.stack
    return (xp.reshape(bsz, t, d), xs.reshape(db, tn, d),
            st(outs_p["ak"]), st(outs_p["av"]), st(outs_p["aki"]), st(outs_p["bk"]), st(outs_p["bv"]),
            st(outs_p["blf"]), st(outs_p["conv"]),
            st(outs_s["ak"]), st(outs_s["av"]), st(outs_s["aki"]), st(outs_s["bk"]), st(outs_s["bv"]),
            st(outs_s["blf"]), st(outs_s["conv"]), st(outs_s["cv"]))
```

```python
import functools

import jax
import jax.numpy as jnp
from jax import lax
from jax.experimental import pallas as pl
from jax.experimental.pallas import tpu as pltpu

F32 = jnp.float32
BF16 = jnp.bfloat16

HEAD_DIM = 64
N_HEADS = 8
N_IDX_HEADS = 8
D_IDX = 64
TOPK = 256
PAGE = 128
CHUNK = 128
N_GROUPS_C = 8
CONV_W = 3
LN_EPS = 1e-5
ROPE_THETA = 10000.0
NEG = -1e30

LANES = 128
SUBLANES = 8
VMEM_LIMIT = 48 * 1024 * 1024

_NT = (((1,), (1,)), ((), ()))


def _cparams(n_axes):
    return pltpu.CompilerParams(dimension_semantics=("arbitrary",) * n_axes,
                                vmem_limit_bytes=VMEM_LIMIT)


def _const_spec(arr):
    return pl.BlockSpec(arr.shape, lambda i: (0,) * arr.ndim, pipeline_mode=pl.Buffered(1))


def _mod_spec(m, tm, nt):
    d = m.shape[2]
    if m.shape[1] == 1:
        tiles_per_group = nt // m.shape[0]
        return pl.BlockSpec((1, 1, d), lambda i: (i // tiles_per_group, 0, 0))
    return pl.BlockSpec((1, tm, d), lambda i: (0, i, 0))


def _ln(z, g, b):
    mu = jnp.mean(z, axis=-1, keepdims=True)
    zc = z - mu
    var = jnp.mean(zc * zc, axis=-1, keepdims=True)
    return zc * lax.rsqrt(var + LN_EPS) * g + b


def _gelu(x):
    return 0.5 * x * (1.0 + lax.erf(x * (2.0 ** -0.5)))


def _split3(x):
    hi = x.astype(BF16)
    r = x - hi.astype(F32)
    mid = r.astype(BF16)
    lo = (r - mid.astype(F32)).astype(BF16)
    return hi, mid, lo


def _dot3(x, m, left):
    parts = _split3(x)
    out = None
    for p in parts:
        d = (jnp.dot(m, p, preferred_element_type=F32) if left
             else jnp.dot(p, m, preferred_element_type=F32))
        out = d if out is None else out + d
    return out


def _mod_kernel(c_ref, w_ref, b_ref, o_ref):
    c = c_ref[...]
    a = (c * jax.nn.sigmoid(c)).astype(BF16)
    o_ref[0] = jnp.dot(a, w_ref[0].astype(BF16), preferred_element_type=F32) + b_ref[0]


def _adaln_mod(c_all, w_mod, b_mod):
    depth, d, n6 = w_mod.shape
    rows = c_all.shape[0]
    tn = n6 // 4
    return pl.pallas_call(
        _mod_kernel,
        grid=(depth, n6 // tn),
        in_specs=[pl.BlockSpec((rows, d), lambda l, j: (0, 0)),
                  pl.BlockSpec((1, d, tn), lambda l, j: (l, 0, j)),
                  pl.BlockSpec((1, 1, tn), lambda l, j: (l, 0, j))],
        out_specs=pl.BlockSpec((1, rows, tn), lambda l, j: (l, 0, j)),
        out_shape=jax.ShapeDtypeStruct((depth, rows, n6), F32),
        compiler_params=_cparams(2),
        name="adaln_mod",
    )(c_all, w_mod, b_mod.reshape(depth, 1, n6))


_G_QA, _G_KA, _G_VA, _G_QI, _G_QB, _G_KB, _G_VB = range(7)
_GW = N_HEADS * HEAD_DIM
_TAIL_OFF = 7 * _GW
_TAIL_WI = D_IDX
_TAIL_F = D_IDX + N_IDX_HEADS


def _pack_w_ab(w):
    sizes = (_GW, _GW, _GW, N_IDX_HEADS * D_IDX, D_IDX, N_IDX_HEADS, _GW, _GW, _GW, N_HEADS)
    offs, acc = [], 0
    for s in sizes[:-1]:
        acc += s
        offs.append(acc)
    qa, ka, va, qi, ki, wi, qb, kb, vb, fb = jnp.split(w, offs, axis=1)
    pad = jnp.zeros((w.shape[0], LANES - D_IDX - N_IDX_HEADS - N_HEADS), w.dtype)
    return jnp.concatenate([qa, ka, va, qi, qb, kb, vb, ki, wi, fb, pad], axis=1).astype(BF16)


def _rope_tables(pos):
    half = HEAD_DIM // 2
    inv = ROPE_THETA ** (-jnp.arange(half, dtype=F32) / half)
    ang = pos.astype(F32)[:, None] * inv[None, :]
    cos, sin = jnp.cos(ang), jnp.sin(ang)
    return (jnp.concatenate([cos] * 4, axis=1),
            jnp.concatenate([-sin, sin, -sin, sin], axis=1))


def _proj_ab_kernel(x_ref, sc_ref, sh_ref, cos_ref, sin_ref, w_ref, bf_ref,
                    qa_ref, kaf_ref, kab_ref, vaf_ref, vab_ref, qi_ref,
                    qb_ref, kbf_ref, kbb_ref, vbf_ref, vbb_ref, tail_ref, kid_ref):
    tm = x_ref.shape[0]
    h = (x_ref[...] * (1.0 + sc_ref[0]) + sh_ref[0]).astype(BF16)
    c1, s1 = cos_ref[...], sin_ref[...]
    c4 = jnp.concatenate([c1] * 4, axis=1)
    s4 = jnp.concatenate([s1] * 4, axis=1)
    lane4 = lax.broadcasted_iota(jnp.int32, (tm, _GW), 1)
    first4 = (lane4 & (HEAD_DIM - 1)) < HEAD_DIM // 2
    q_scale = HEAD_DIM ** -0.5

    def proj(g):
        return jnp.dot(h, w_ref[:, g * _GW:(g + 1) * _GW], preferred_element_type=F32)

    def rope(p):
        rot = jnp.where(first4, pltpu.roll(p, _GW - HEAD_DIM // 2, 1), pltpu.roll(p, HEAD_DIM // 2, 1))
        return p * c4 + rot * s4

    qa_ref[...] = (rope(proj(_G_QA)) * q_scale).astype(BF16)
    ka = rope(proj(_G_KA))
    kaf_ref[...] = ka
    kab_ref[...] = ka.astype(BF16)
    va = proj(_G_VA)
    vaf_ref[...] = va
    vab_ref[...] = va.astype(BF16)
    qi_ref[...] = (rope(proj(_G_QI)) * (D_IDX ** -0.5)).astype(BF16)
    qb_ref[...] = (proj(_G_QB) * q_scale).astype(BF16)
    kb = proj(_G_KB)
    kbf_ref[...] = kb
    kbb_ref[...] = kb.astype(BF16)
    vb = proj(_G_VB)
    vbf_ref[...] = vb
    vbb_ref[...] = vb.astype(BF16)

    t = jnp.dot(h, w_ref[:, _TAIL_OFF:_TAIL_OFF + LANES], preferred_element_type=F32)
    lane = lax.broadcasted_iota(jnp.int32, (tm, LANES), 1)
    first = (lane & (HEAD_DIM - 1)) < HEAD_DIM // 2
    rot = jnp.where(first, pltpu.roll(t, LANES - HEAD_DIM // 2, 1), pltpu.roll(t, HEAD_DIM // 2, 1))
    roped = t * c1 + rot * s1
    z = t + bf_ref[...]
    logf = jnp.minimum(z, 0.0) - jnp.log1p(jnp.exp(-jnp.abs(z)))
    tail = jnp.where(lane < D_IDX, roped,
                     jnp.where(lane < _TAIL_F, t * (N_IDX_HEADS ** -0.5),
                               jnp.where(lane < _TAIL_F + N_HEADS, logf, 0.0)))
    tail_ref[...] = tail
    kid_ref[...] = jnp.where(lane < D_IDX, roped, pltpu.roll(roped, D_IDX, 1)).astype(BF16)


def _proj_ab(x, sc, sh, cos_t, sin_t, w_packed, bf_row, tm):
    rows, d = x.shape
    nt = rows // tm
    t_tiles = cos_t.shape[0] // tm
    mod_spec = _mod_spec(sc, tm, nt)
    rope_spec = pl.BlockSpec((tm, LANES), lambda i: (i % t_tiles, 0))
    wide = lambda: pl.BlockSpec((tm, _GW), lambda i: (i, 0))
    narrow = lambda: pl.BlockSpec((tm, LANES), lambda i: (i, 0))
    f = jax.ShapeDtypeStruct((rows, _GW), F32)
    b = jax.ShapeDtypeStruct((rows, _GW), BF16)
    out_shape = (b, f, b, f, b, b, b, f, b, f, b,
                 jax.ShapeDtypeStruct((rows, LANES), F32), jax.ShapeDtypeStruct((rows, LANES), BF16))
    return pl.pallas_call(
        _proj_ab_kernel,
        grid=(nt,),
        in_specs=[pl.BlockSpec((tm, d), lambda i: (i, 0)), mod_spec, mod_spec, rope_spec, rope_spec,
                  _const_spec(w_packed), _const_spec(bf_row)],
        out_specs=tuple([wide() for _ in range(11)] + [narrow(), narrow()]),
        out_shape=out_shape,
        compiler_params=_cparams(1),
        name="proj_ab",
    )(x, sc, sh, cos_t, sin_t, w_packed, bf_row)


def _cumsum_kernel(t_ref, o_ref, carry):
    @pl.when(pl.program_id(1) == 0)
    def _():
        carry[...] = jnp.zeros(carry.shape, F32)

    r = lax.broadcasted_iota(jnp.int32, (LANES, LANES), 0)
    c = lax.broadcasted_iota(jnp.int32, (LANES, LANES), 1)
    tri = jnp.where(c <= r, 1.0, 0.0).astype(BF16)
    run = carry[...]
    for k in range(t_ref.shape[1] // LANES):
        x = t_ref[0, k * LANES:(k + 1) * LANES, :]
        cum = _dot3(x, tri, left=True) + run
        run = cum[LANES - 1:LANES, :]
        for h in range(N_HEADS):
            o_ref[0, h, k * LANES:(k + 1) * LANES, :] = jnp.broadcast_to(
                cum[:, _TAIL_F + h:_TAIL_F + h + 1], (LANES, LANES))
    carry[...] = run


def _logf_cumsum(tail, tc):
    bsz, t, _ = tail.shape
    return pl.pallas_call(
        _cumsum_kernel,
        grid=(bsz, t // tc),
        in_specs=[pl.BlockSpec((1, tc, LANES), lambda b, i: (b, i, 0))],
        out_specs=pl.BlockSpec((1, N_HEADS, tc, LANES), lambda b, i: (b, 0, i, 0)),
        out_shape=jax.ShapeDtypeStruct((bsz, N_HEADS, t, LANES), F32),
        scratch_shapes=[pltpu.VMEM((1, LANES), F32)],
        compiler_params=_cparams(2),
        name="logf_cumsum",
    )(tail)


_PAIRS = N_HEADS // 2


def _chunk_t(a, width):
    bsz, t, _ = a.shape
    return a.reshape(bsz, t // width, width, _PAIRS, LANES).transpose(0, 3, 1, 4, 2)


def _unchunk_t(a):
    bsz, _, n, _, width = a.shape
    return a.transpose(0, 2, 4, 1, 3).reshape(bsz, n * width, _GW)


def _head_halves(x_t, top):
    zero = jnp.zeros_like(x_t)
    return jnp.where(top, x_t, zero), jnp.where(top, zero, x_t)


def _pair_step_t(q0, q1, kc, v_t, b0, b1, carry, top):
    m0, l0, m1, l1, acc = carry
    s0 = jnp.dot(kc, q0, preferred_element_type=F32) + b0
    s1 = jnp.dot(kc, q1, preferred_element_type=F32) + b1
    n0 = jnp.maximum(m0, jnp.max(s0, axis=0, keepdims=True))
    n1 = jnp.maximum(m1, jnp.max(s1, axis=0, keepdims=True))
    a0, a1 = jnp.exp(m0 - n0), jnp.exp(m1 - n1)
    p0, p1 = jnp.exp(s0 - n0), jnp.exp(s1 - n1)
    l0 = a0 * l0 + jnp.sum(p0, axis=0, keepdims=True)
    l1 = a1 * l1 + jnp.sum(p1, axis=0, keepdims=True)
    pv0 = jnp.dot(v_t, p0.astype(BF16), preferred_element_type=F32)
    pv1 = jnp.dot(v_t, p1.astype(BF16), preferred_element_type=F32)
    acc = jnp.where(top, a0 * acc + pv0, a1 * acc + pv1)
    return n0, l0, n1, l1, acc


def _pair_init_t(tq):
    row = lambda v: jnp.full((1, tq), v, F32)
    return row(-jnp.inf), row(0.0), row(-jnp.inf), row(0.0), jnp.zeros((LANES, tq), F32)


def _pair_finish_t(carry, top):
    _, l0, _, l1, acc = carry
    return acc / jnp.where(top, l0, l1)


def _causal_loop(it, tq, kc, step, init):
    per_tile = tq // kc
    carry = lax.fori_loop(0, it * per_tile, lambda j, c: step(j, c, None), init)
    for d in range(per_tile):
        carry = step(it * per_tile + d, carry, d * kc)
    return carry


def _fox_kernel(q_ref, k_ref, v_ref, ck_ref, o_ref):
    npair, tq, kc = q_ref.shape[1], q_ref.shape[-1], v_ref.shape[-1]
    it = pl.program_id(2)
    top = lax.broadcasted_iota(jnp.int32, (LANES, tq), 0) < HEAD_DIM
    qs = [_head_halves(q_ref[0, p, 0], top) for p in range(npair)]
    key = lax.broadcasted_iota(jnp.int32, (kc, tq), 0)
    qry = lax.broadcasted_iota(jnp.int32, (kc, tq), 1)
    wide = lambda x: jnp.concatenate([x] * (tq // LANES), axis=1)

    def step(j, carries, diag_off):
        off = pl.multiple_of(j * kc, kc)
        causal = None if diag_off is None else key + diag_off <= qry
        out = []
        for p in range(npair):
            b0 = -wide(ck_ref[0, 2 * p, pl.ds(off, kc), :])
            b1 = -wide(ck_ref[0, 2 * p + 1, pl.ds(off, kc), :])
            if causal is not None:
                b0, b1 = jnp.where(causal, b0, NEG), jnp.where(causal, b1, NEG)
            out.append(_pair_step_t(qs[p][0], qs[p][1], k_ref[0, pl.ds(off, kc), p * LANES:(p + 1) * LANES],
                                    v_ref[0, p, j], b0, b1, carries[p], top))
        return tuple(out)

    carries = _causal_loop(it, tq, kc, step, tuple(_pair_init_t(tq) for _ in range(npair)))
    for p in range(npair):
        o_ref[0, p, 0] = _pair_finish_t(carries[p], top).astype(o_ref.dtype)


def _fox_prompt(q_t, kb, v_t, ckb, pairs_per_step):
    bsz, _, nq, _, tq = q_t.shape
    nk, kc = v_t.shape[2], v_t.shape[4]
    t = kb.shape[1]
    pp = pairs_per_step
    return pl.pallas_call(
        _fox_kernel,
        grid=(bsz, _PAIRS // pp, nq),
        in_specs=[pl.BlockSpec((1, pp, 1, LANES, tq), lambda b, g, i: (b, g, i, 0, 0)),
                  pl.BlockSpec((1, t, pp * LANES), lambda b, g, i: (b, 0, g)),
                  pl.BlockSpec((1, pp, nk, LANES, kc), lambda b, g, i: (b, g, 0, 0, 0)),
                  pl.BlockSpec((1, 2 * pp, t, LANES), lambda b, g, i: (b, g, 0, 0))],
        out_specs=pl.BlockSpec((1, pp, 1, LANES, tq), lambda b, g, i: (b, g, i, 0, 0)),
        out_shape=jax.ShapeDtypeStruct(q_t.shape, BF16),
        compiler_params=_cparams(3),
        name="fox_prompt",
    )(q_t, kb, v_t, ckb)


def _topk_to_bias(sc_ref, nch, nvalid, ksel, keys_axis, n_bisect=16):
    kc = sc_ref.shape[1 + keys_axis]
    kf = float(ksel)
    ninf = -jnp.inf
    stat = lambda v: jnp.full(nvalid.shape, v, F32)

    if keys_axis == 1:
        def over_keys(elem, red, combine, init):
            return red(red(elem(sc_ref[...]), axis=0), axis=1, keepdims=True)
    else:
        def over_keys(elem, red, combine, init):
            return lax.fori_loop(
                0, nch, lambda j, a: combine(a, red(elem(sc_ref[j]), axis=0, keepdims=True)), stat(init))

    def count(t, strict):
        hit = (lambda x: jnp.where(x > t, 1.0, 0.0)) if strict else (lambda x: jnp.where(x >= t, 1.0, 0.0))
        return over_keys(hit, jnp.sum, jnp.add, 0.0)

    def count_ge(t):
        return count(t, False)

    def max_below(h):
        return over_keys(lambda x: jnp.where(x < h, x, ninf), jnp.max, jnp.maximum, ninf)

    def min_valid():
        return over_keys(lambda x: jnp.where(x == ninf, jnp.inf, x), jnp.min, jnp.minimum, jnp.inf)

    small = nvalid <= kf
    lo = min_valid()
    v = max_below(stat(jnp.inf))
    ok = count_ge(v) >= kf
    res = jnp.where(jnp.logical_or(small, ok), 1.0, 0.0)
    thr = jnp.where(small, ninf, v)
    hi = jnp.where(res > 0.0, jnp.inf, v)

    def bisect(_, st):
        lo, hi = st
        mid = 0.5 * (lo + hi)
        ge = count_ge(mid) >= kf
        return jnp.where(ge, mid, lo), jnp.where(ge, hi, mid)

    lo, hi = lax.fori_loop(0, n_bisect, bisect, (lo, hi))

    def snap_cond(st):
        return jnp.min(st[2]) < 1.0

    def snap(st):
        thr, hi, res = st
        v = max_below(hi)
        ok = count_ge(v) >= kf
        open_ = res < 1.0
        new = jnp.logical_and(open_, ok)
        drop = jnp.logical_and(open_, jnp.logical_not(ok))
        return jnp.where(new, v, thr), jnp.where(drop, v, hi), jnp.where(ok, 1.0, res)

    thr, hi, res = lax.while_loop(snap_cond, snap, (thr, hi, res))

    need = kf - count(thr, True)
    r = lax.broadcasted_iota(jnp.int32, (kc, kc), 0)
    c = lax.broadcasted_iota(jnp.int32, (kc, kc), 1)
    prefix = jnp.where((r < c) if keys_axis == 1 else (c < r), 1.0, 0.0).astype(BF16)

    def ties(s):
        tie = jnp.logical_and(s == thr, s > ninf)
        return tie, jnp.where(tie, 1.0, 0.0)

    def write_bias(j, s, tie, before):
        sel = jnp.logical_or(s > thr, jnp.logical_and(tie, before < need))
        sc_ref[j] = jnp.where(sel, 0.0, NEG)

    if keys_axis == 1:
        s = sc_ref[...]
        tie, tief = ties(s)
        rows = s.shape[1]
        within = jnp.dot(tief.reshape(nch * rows, kc).astype(BF16), prefix,
                         preferred_element_type=F32).reshape(nch, rows, kc)
        totals = jnp.sum(tief, axis=2, keepdims=True)
        run = stat(0.0)
        for j in range(nch):
            write_bias(j, s[j], tie[j], within[j] + run)
            run = run + totals[j]
    else:
        def to_bias(j, run):
            s = sc_ref[j]
            tie, tief = ties(s)
            before = jnp.dot(prefix, tief.astype(BF16), preferred_element_type=F32) + run
            write_bias(j, s, tie, before)
            return run + jnp.sum(tief, axis=0, keepdims=True)

        lax.fori_loop(0, nch, to_bias, stat(0.0))


def _dsa_kernel(qi_ref, w_ref, kid_ref, qa_ref, ka_ref, va_ref, o_ref, sc_ref, *, ksel):
    tq, kc = qi_ref.shape[-1], va_ref.shape[-1]
    it = pl.program_id(1)
    top = lax.broadcasted_iota(jnp.int32, (LANES, tq), 0) < HEAD_DIM
    key = lax.broadcasted_iota(jnp.int32, (kc, tq), 0)
    qry = lax.broadcasted_iota(jnp.int32, (kc, tq), 1)

    qm = []
    for p in range(N_IDX_HEADS // 2):
        qm.extend(_head_halves(qi_ref[0, p, 0], top))
    w = w_ref[0]

    def score_chunk(j, carry, diag_off):
        off = pl.multiple_of(j * kc, kc)
        keys = kid_ref[0, pl.ds(off, kc), :]
        sc = jnp.zeros((kc, tq), F32)
        for h in range(N_IDX_HEADS):
            lg = jnp.dot(keys, qm[h], preferred_element_type=F32)
            sc = sc + jnp.maximum(lg, 0.0) * w[h:h + 1, :]
        if diag_off is not None:
            sc = jnp.where(key + diag_off <= qry, sc, -jnp.inf)
        sc_ref[j] = sc
        return carry

    _causal_loop(it, tq, kc, score_chunk, 0)
    nch = (it + 1) * (tq // kc)
    nvalid = (it * tq + 1 + lax.broadcasted_iota(jnp.int32, (1, tq), 1)).astype(F32)
    _topk_to_bias(sc_ref, nch, nvalid, ksel, keys_axis=0)

    qs = [_head_halves(qa_ref[0, p, 0], top) for p in range(_PAIRS)]

    def step(j, carries):
        off = pl.multiple_of(j * kc, kc)
        bias = sc_ref[j]
        return tuple(
            _pair_step_t(qs[p][0], qs[p][1], ka_ref[0, pl.ds(off, kc), p * LANES:(p + 1) * LANES], va_ref[0, p, j],
                         bias, bias, carries[p], top)
            for p in range(_PAIRS))

    carries = lax.fori_loop(0, nch, step, tuple(_pair_init_t(tq) for _ in range(_PAIRS)))
    for p in range(_PAIRS):
        o_ref[0, p, 0] = _pair_finish_t(carries[p], top).astype(o_ref.dtype)


def _dsa_prompt(qi_t, w_t, kid, qa_t, ka, va_t, ksel):
    bsz, _, nq, _, tq = qa_t.shape
    nk, kc = va_t.shape[2], va_t.shape[4]
    t = ka.shape[1]
    q_spec = pl.BlockSpec((1, _PAIRS, 1, LANES, tq), lambda b, i: (b, 0, i, 0, 0))
    return pl.pallas_call(
        functools.partial(_dsa_kernel, ksel=ksel),
        grid=(bsz, nq),
        in_specs=[q_spec,
                  pl.BlockSpec((1, N_IDX_HEADS, tq), lambda b, i: (b, 0, i)),
                  pl.BlockSpec((1, t, LANES), lambda b, i: (b, 0, 0)),
                  q_spec,
                  pl.BlockSpec((1, t, _GW), lambda b, i: (b, 0, 0)),
                  pl.BlockSpec((1, _PAIRS, nk, LANES, kc), lambda b, i: (b, 0, 0, 0, 0))],
        out_specs=q_spec,
        out_shape=jax.ShapeDtypeStruct(qa_t.shape, BF16),
        scratch_shapes=[pltpu.VMEM((nk, kc, tq), F32)],
        compiler_params=_cparams(2),
        name="dsa_prompt",
    )(qi_t, w_t, kid, qa_t, ka, va_t)


def _sample_idx_kernel(pt_ref, qi_ref, w_ref, knew_ref, *rest, pages_per_step, ksel):
    page_refs = rest[:pages_per_step]
    o_ref = rest[pages_per_step]
    sc_ref = rest[pages_per_step + 1]
    s = pl.program_id(1)
    nsteps = pl.num_programs(1)
    npg = sc_ref.shape[0] - 1
    tn = sc_ref.shape[1]
    qi = qi_ref[0]
    w = w_ref[0]

    def scores(keys_t):
        lg = jnp.dot(qi, keys_t, preferred_element_type=F32)
        z = jnp.maximum(lg, 0.0) * w
        out = z[0:tn]
        for h in range(1, N_IDX_HEADS):
            out = out + z[h * tn:(h + 1) * tn]
        return out

    for i, pr in enumerate(page_refs):
        sc_ref[s * pages_per_step + i] = scores(pr[0].astype(BF16))

    @pl.when(s == nsteps - 1)
    def _():
        r = lax.broadcasted_iota(jnp.int32, (tn, PAGE), 0)
        c = lax.broadcasted_iota(jnp.int32, (tn, PAGE), 1)
        sc_ref[npg] = jnp.where(c <= r, scores(knew_ref[0]), -jnp.inf)
        nvalid = (npg * PAGE + 1 + lax.broadcasted_iota(jnp.int32, (tn, 1), 0)).astype(F32)
        _topk_to_bias(sc_ref, npg + 1, nvalid, ksel, keys_axis=1)
        o_ref[0] = sc_ref[...]


def _sample_indexer(page_table, qi_rows, w_rows, ki_new_pad, kidx_pool, pages_per_step, ksel):
    db, npg = page_table.shape
    tn = qi_rows.shape[1] // N_IDX_HEADS
    nsteps = npg // pages_per_step

    def page_spec(i):
        return pl.BlockSpec((1, D_IDX, PAGE), lambda b, s, pt: (pt[b, s * pages_per_step + i], 0, 0))

    grid_spec = pltpu.PrefetchScalarGridSpec(
        num_scalar_prefetch=1,
        grid=(db, nsteps),
        in_specs=[pl.BlockSpec((1,) + qi_rows.shape[1:], lambda b, s, pt: (b, 0, 0)),
                  pl.BlockSpec((1,) + w_rows.shape[1:], lambda b, s, pt: (b, 0, 0)),
                  pl.BlockSpec((1, D_IDX, PAGE), lambda b, s, pt: (b, 0, 0))]
                 + [page_spec(i) for i in range(pages_per_step)],
        out_specs=pl.BlockSpec((1, npg + 1, tn, PAGE), lambda b, s, pt: (b, 0, 0, 0)),
        scratch_shapes=[pltpu.VMEM((npg + 1, tn, PAGE), F32)],
    )
    return pl.pallas_call(
        functools.partial(_sample_idx_kernel, pages_per_step=pages_per_step, ksel=ksel),
        grid_spec=grid_spec,
        out_shape=jax.ShapeDtypeStruct((db, npg + 1, tn, PAGE), F32),
        compiler_params=_cparams(2),
        name="sample_indexer",
    )(page_table, qi_rows, w_rows, ki_new_pad, *([kidx_pool] * pages_per_step))


def _sample_attn_kernel(pt_ref, qa_ref, qb_ref, bias_ref, biasn_ref, lfn_ref,
                        kan_ref, van_ref, kbn_ref, vbn_ref, *rest, pps):
    lft_refs, ka_refs, va_refs, kb_refs, vb_refs = (rest[i * pps:(i + 1) * pps] for i in range(5))
    oa_ref, ob_ref, ma, la, acca, mb, lb, accb, run = rest[5 * pps:]
    p = pl.program_id(1)
    nsteps = pl.num_programs(1)
    rows = qa_ref.shape[1]
    tn = rows // N_HEADS

    @pl.when(p == 0)
    def _():
        for m in (ma, mb):
            m[...] = jnp.full(m.shape, -jnp.inf, F32)
        for z in (la, lb, acca, accb, run):
            z[...] = jnp.zeros(z.shape, F32)

    r = lax.broadcasted_iota(jnp.int32, (PAGE, PAGE), 0)
    c = lax.broadcasted_iota(jnp.int32, (PAGE, PAGE), 1)
    incl = jnp.where(r <= c, 1.0, 0.0).astype(BF16)

    def update(q, kts, vts, biases, m_ref, l_ref, acc_ref):
        s = [jnp.dot(q, kt, preferred_element_type=F32) + b for kt, b in zip(kts, biases)]
        smax = functools.reduce(jnp.maximum, s)
        m_old = m_ref[...]
        m_new = jnp.maximum(m_old, jnp.max(smax, axis=1, keepdims=True))
        a = jnp.exp(m_old - m_new)
        ps = [jnp.exp(x - m_new) for x in s]
        l_ref[...] = a * l_ref[...] + jnp.sum(functools.reduce(jnp.add, ps), axis=1, keepdims=True)
        pv = [lax.dot_general(pr.astype(BF16), vt, _NT, preferred_element_type=F32) for pr, vt in zip(ps, vts)]
        acc_ref[...] = a * acc_ref[...] + functools.reduce(jnp.add, pv)
        m_ref[...] = m_new

    def head_rows(x):
        return jnp.concatenate([jnp.broadcast_to(x[h:h + 1], (tn, x.shape[1])) for h in range(N_HEADS)], axis=0)

    def tile_rows(x):
        return jnp.concatenate([x] * N_HEADS, axis=0)

    def forget_biases(lf_ts):
        out, run_v = [], run[...]
        for lf_t in lf_ts:
            cum = _dot3(head_rows(lf_t), incl, left=False) + run_v
            run_v = cum[:, PAGE - 1:PAGE]
            out.append(-cum)
        run[...] = run_v
        return out

    cast = lambda refs: [x[0].astype(BF16) for x in refs]
    update(qa_ref[0], cast(ka_refs), cast(va_refs), [tile_rows(bias_ref[0, i]) for i in range(pps)], ma, la, acca)
    update(qb_ref[0], cast(kb_refs), cast(vb_refs), forget_biases([x[0] for x in lft_refs]), mb, lb, accb)

    @pl.when(p == nsteps - 1)
    def _():
        update(qa_ref[0], [kan_ref[0]], [van_ref[0]], [tile_rows(biasn_ref[0, 0])], ma, la, acca)
        qrow = lax.broadcasted_iota(jnp.int32, (rows, PAGE), 0) & (tn - 1)
        key = lax.broadcasted_iota(jnp.int32, (rows, PAGE), 1)
        fb = jnp.where(key <= qrow, forget_biases([lfn_ref[0]])[0], NEG)
        update(qb_ref[0], [kbn_ref[0]], [vbn_ref[0]], [fb], mb, lb, accb)
        lane = lax.broadcasted_iota(jnp.int32, (tn, _GW), 1)
        for l_ref, acc_ref, o_ref in ((la, acca, oa_ref), (lb, accb, ob_ref)):
            full = acc_ref[...] / l_ref[...]
            out = jnp.zeros((tn, _GW), F32)
            for h in range(N_HEADS):
                mine = jnp.logical_and(lane >= h * HEAD_DIM, lane < (h + 1) * HEAD_DIM)
                out = jnp.where(mine, full[h * tn:(h + 1) * tn], out)
            o_ref[0] = out.astype(o_ref.dtype)


def _sample_attn(page_table, qa_bd, qb_bd, bias, lft_pool, lfn, kan, van, kbn, vbn,
                 ka_pool, va_pool, kb_pool, vb_pool, pps):
    db, npg = page_table.shape
    rows = qa_bd.shape[1]
    tn = rows // N_HEADS
    per_b = lambda shp: pl.BlockSpec((1,) + shp, lambda b, p, pt: (b,) + (0,) * len(shp))

    def paged(shp):
        return [pl.BlockSpec((1,) + shp, lambda b, p, pt, i=i: (pt[b, p * pps + i],) + (0,) * len(shp))
                for i in range(pps)]

    grid_spec = pltpu.PrefetchScalarGridSpec(
        num_scalar_prefetch=1,
        grid=(db, npg // pps),
        in_specs=[per_b((rows, _GW)), per_b((rows, _GW)),
                  pl.BlockSpec((1, pps, tn, PAGE), lambda b, p, pt: (b, p, 0, 0)),
                  pl.BlockSpec((1, 1, tn, PAGE), lambda b, p, pt: (b, npg, 0, 0)),
                  per_b((N_HEADS, PAGE)),
                  per_b((_GW, PAGE)), per_b((_GW, PAGE)), per_b((_GW, PAGE)), per_b((_GW, PAGE))]
                 + paged((N_HEADS, PAGE)) + paged((_GW, PAGE)) + paged((_GW, PAGE))
                 + paged((_GW, PAGE)) + paged((_GW, PAGE)),
        out_specs=(per_b((tn, _GW)), per_b((tn, _GW))),
        scratch_shapes=[pltpu.VMEM((rows, 1), F32), pltpu.VMEM((rows, 1), F32), pltpu.VMEM((rows, _GW), F32),
                        pltpu.VMEM((rows, 1), F32), pltpu.VMEM((rows, 1), F32), pltpu.VMEM((rows, _GW), F32),
                        pltpu.VMEM((rows, 1), F32)],
    )
    o = jax.ShapeDtypeStruct((db, tn, _GW), BF16)
    rep = lambda a: [a] * pps
    return pl.pallas_call(
        functools.partial(_sample_attn_kernel, pps=pps),
        grid_spec=grid_spec,
        out_shape=(o, o),
        compiler_params=_cparams(2),
        name="sample_attn",
    )(page_table, qa_bd, qb_bd, bias, bias, lfn, kan, van, kbn, vbn,
      *rep(lft_pool), *rep(ka_pool), *rep(va_pool), *rep(kb_pool), *rep(vb_pool))


def _out_ln_kernel(a_ref, b_ref, x_ref, g_ref, wa_ref, wb_ref, lg_ref, lb_ref, o_ref, *, alpha):
    mix = (jnp.dot(a_ref[...], wa_ref[...], preferred_element_type=F32)
           + jnp.dot(b_ref[...], wb_ref[...], preferred_element_type=F32))
    z = alpha * x_ref[...] + (1.0 + g_ref[0]) * mix
    o_ref[...] = _ln(z, lg_ref[...], lb_ref[...])


def _out_ln(a, b, x, gate, w_out, ln_g, ln_b, alpha, tm):
    rows, d = x.shape
    nt = rows // tm
    ka = a.shape[1]
    wa, wb = w_out[:ka].astype(BF16), w_out[ka:].astype(BF16)
    const = _const_spec
    lg, lb = ln_g.reshape(1, d), ln_b.reshape(1, d)
    return pl.pallas_call(
        functools.partial(_out_ln_kernel, alpha=alpha),
        grid=(nt,),
        in_specs=[pl.BlockSpec((tm, ka), lambda i: (i, 0)), pl.BlockSpec((tm, b.shape[1]), lambda i: (i, 0)),
                  pl.BlockSpec((tm, d), lambda i: (i, 0)),
                  _mod_spec(gate, tm, nt),
                  const(wa), const(wb), const(lg), const(lb)],
        out_specs=pl.BlockSpec((tm, d), lambda i: (i, 0)),
        out_shape=jax.ShapeDtypeStruct((rows, d), F32),
        compiler_params=_cparams(1),
        name="out_ln",
    )(a, b, x, gate, wa, wb, lg, lb)


_FF_CW = 256


def _ffn_kernel(x_ref, sc_ref, sh_ref, g_ref, prev_ref, wg_ref, wv_ref, cw_ref, wd_ref, lg_ref, lb_ref,
                o_ref, up_ref, bufg, bufv, carry, *, alpha, nseq, tiles_per_seq):
    tm, d = x_ref.shape
    ts = tm // nseq
    nch, _, cw = wg_ref.shape
    dff = nch * cw
    x = x_ref[...]
    h = (x * (1.0 + sc_ref[0]) + sh_ref[0]).astype(BF16)
    chained = tiles_per_seq > 1

    if chained:
        @pl.when(pl.program_id(0) % tiles_per_seq == 0)
        def _():
            carry[...] = prev_ref[0]

    acc = jnp.zeros((tm, d), F32)
    for c in range(nch):
        cwr = cw_ref[c]
        ys = []
        for half, (w_ref, buf) in enumerate(((wg_ref, bufg), (wv_ref, bufv))):
            lo = half * dff + c * cw
            u = jnp.dot(h, w_ref[c], preferred_element_type=F32)
            if chained:
                buf[0, 0:SUBLANES, :] = carry[:, lo:lo + cw]
                carry[:, lo:lo + cw] = u[tm - SUBLANES:, :]
                up_ref[:, lo:lo + cw] = u[tm - SUBLANES:, :]
            else:
                buf[:, 0:SUBLANES, :] = prev_ref[:, :, lo:lo + cw]
                up_ref[:, lo:lo + cw] = u
            buf[:, SUBLANES:SUBLANES + ts, :] = u.reshape(nseq, ts, cw)
            um1 = buf[:, SUBLANES - 1:SUBLANES - 1 + ts, :].reshape(tm, cw)
            um2 = buf[:, SUBLANES - 2:SUBLANES - 2 + ts, :].reshape(tm, cw)
            k0 = 4 * half
            y = cwr[k0 + 3:k0 + 4, :] + cwr[k0:k0 + 1, :] * um2
            y = y + cwr[k0 + 1:k0 + 2, :] * um1
            ys.append(y + cwr[k0 + 2:k0 + 3, :] * u)
        act = (_gelu(ys[0]) * ys[1]).astype(BF16)
        acc = acc + jnp.dot(act, wd_ref[c], preferred_element_type=F32)

    z = alpha * x + (1.0 + g_ref[0]) * acc
    o_ref[...] = _ln(z, lg_ref[...], lb_ref[...])


def _conv_ffn(x, sc, sh, gate, prev8, w_up, w_conv, b_conv, w_down, ln_g, ln_b, alpha, tm, seq_len):
    rows, d = x.shape
    dff = w_down.shape[0]
    nch = dff // _FF_CW
    nt = rows // tm
    if seq_len >= tm:
        nseq, tiles_per_seq = 1, seq_len // tm
        prev_spec = pl.BlockSpec((1, SUBLANES, 2 * dff), lambda i: (i // tiles_per_seq, 0, 0))
        up_rows, up_spec = nt * SUBLANES, pl.BlockSpec((SUBLANES, 2 * dff), lambda i: (i, 0))
    else:
        nseq, tiles_per_seq = tm // seq_len, 1
        prev_spec = pl.BlockSpec((nseq, SUBLANES, 2 * dff), lambda i: (i, 0, 0))
        up_rows, up_spec = rows, pl.BlockSpec((tm, 2 * dff), lambda i: (i, 0))
    ts = tm // nseq
    chunked = lambda w: w.reshape(w.shape[0], nch, _FF_CW).transpose(1, 0, 2)
    wg = chunked(w_up[:, :dff]).astype(BF16)
    wv = chunked(w_up[:, dff:]).astype(BF16)
    wd = w_down.reshape(nch, _FF_CW, d).astype(BF16)
    conv_rows = jnp.concatenate([w_conv[:, :dff], b_conv[None, :dff], w_conv[:, dff:], b_conv[None, dff:]], axis=0)
    cwr = chunked(conv_rows)
    lg, lb = ln_g.reshape(1, d), ln_b.reshape(1, d)
    const = _const_spec
    mod_spec = lambda m: _mod_spec(m, tm, nt)
    return pl.pallas_call(
        functools.partial(_ffn_kernel, alpha=alpha, nseq=nseq, tiles_per_seq=tiles_per_seq),
        grid=(nt,),
        in_specs=[pl.BlockSpec((tm, d), lambda i: (i, 0)), mod_spec(sc), mod_spec(sh), mod_spec(gate), prev_spec,
                  const(wg), const(wv), const(cwr), const(wd), const(lg), const(lb)],
        out_specs=(pl.BlockSpec((tm, d), lambda i: (i, 0)), up_spec),
        out_shape=(jax.ShapeDtypeStruct((rows, d), F32), jax.ShapeDtypeStruct((up_rows, 2 * dff), F32)),
        scratch_shapes=[pltpu.VMEM((nseq, ts + SUBLANES, _FF_CW), F32),
                        pltpu.VMEM((nseq, ts + SUBLANES, _FF_CW), F32),
                        pltpu.VMEM((SUBLANES, 2 * dff), F32)],
        compiler_params=_cparams(1),
        name="conv_ffn",
    )(x, sc, sh, gate, prev8, wg, wv, cwr, wd, lg, lb)


def _gmlp_kernel(x_ref, sc_ref, sh_ref, g_ref, wv_ref, wu_ref, lvg_ref, lvb_ref, wmix_ref, bs_ref, wo_ref,
                 lg_ref, lb_ref, o_ref, *maybe_v_ref, alpha):
    tm, d = x_ref.shape
    ng, _, gw = wu_ref.shape
    x = x_ref[...]
    h = (x * (1.0 + sc_ref[0]) + sh_ref[0]).astype(BF16)
    v = _ln(_gelu(jnp.dot(h, wv_ref[...], preferred_element_type=F32)), lvg_ref[...], lvb_ref[...])
    if maybe_v_ref:
        maybe_v_ref[0][...] = v
    vb = v.astype(BF16)
    bs = bs_ref[...]
    acc = jnp.zeros((tm, d), F32)
    for g in range(ng):
        u = _gelu(jnp.dot(h, wu_ref[g], preferred_element_type=F32))
        mixed = jnp.dot(wmix_ref[g], vb[:, g * gw:(g + 1) * gw], preferred_element_type=F32) + bs[:, g:g + 1]
        acc = acc + jnp.dot((u * mixed).astype(BF16), wo_ref[g], preferred_element_type=F32)
    z = alpha * x + (1.0 + g_ref[0]) * acc
    o_ref[...] = _ln(z, lg_ref[...], lb_ref[...])


def _chunk_gmlp(x, sc, sh, gate, w_in, lnv_g, lnv_b, w_spatial, b_spatial, w_out, ln_g, ln_b,
                alpha, tm, chunk_len, emit_v):
    rows, d = x.shape
    dcg = w_out.shape[0]
    ng = w_spatial.shape[0]
    gw = dcg // ng
    nt = rows // tm
    wu = w_in[:, :dcg].reshape(d, ng, gw).transpose(1, 0, 2).astype(BF16)
    wv = w_in[:, dcg:].astype(BF16)
    wo = w_out.reshape(ng, gw, d).astype(BF16)
    tri = (jnp.arange(chunk_len)[:, None] >= jnp.arange(chunk_len)[None, :]).astype(w_spatial.dtype)
    wc = w_spatial[:, :chunk_len, :chunk_len] * tri[None]
    eye = jnp.eye(tm // chunk_len, dtype=w_spatial.dtype)
    wmix = jnp.einsum("ab,gts->gatbs", eye, wc).reshape(ng, tm, tm).astype(BF16)
    bs = jnp.tile(b_spatial[:, :chunk_len].T, (tm // chunk_len, 1))
    bs = jnp.pad(bs, ((0, 0), (0, LANES - ng)))
    lvg, lvb = lnv_g.reshape(1, dcg), lnv_b.reshape(1, dcg)
    lg, lb = ln_g.reshape(1, d), ln_b.reshape(1, d)
    const = _const_spec
    mod_spec = lambda m: _mod_spec(m, tm, nt)
    row_spec = lambda w: pl.BlockSpec((tm, w), lambda i: (i, 0))
    out_specs, out_shape = [row_spec(d)], [jax.ShapeDtypeStruct((rows, d), F32)]
    if emit_v:
        out_specs.append(row_spec(dcg))
        out_shape.append(jax.ShapeDtypeStruct((rows, dcg), F32))
    return pl.pallas_call(
        functools.partial(_gmlp_kernel, alpha=alpha),
        grid=(nt,),
        in_specs=[row_spec(d), mod_spec(sc), mod_spec(sh), mod_spec(gate), const(wv), const(wu), const(lvg),
                  const(lvb), const(wmix), const(bs), const(wo), const(lg), const(lb)],
        out_specs=tuple(out_specs),
        out_shape=tuple(out_shape),
        compiler_params=_cparams(1),
        name="chunk_gmlp",
    )(x, sc, sh, gate, wv, wu, lvg, lvb, wmix, bs, wo, lg, lb)


def kernel(x_prompt, x_sample, cache_a_k, cache_a_v, cache_a_kidx, cache_b_k, cache_b_v, cache_b_logf,
           state_ffn_conv, page_table, c_prompt, c_sample, w_mod, b_mod, ln1_g, ln1_b, ln2_g, ln2_b,
           w_in_ab, b_forget, w_out_ab, w_in_c, lnv_g, lnv_b, w_spatial, b_spatial, w_out_c,
           w_up, w_conv, b_conv, w_down):
    bsz, t, d = x_prompt.shape
    db, tn, _ = x_sample.shape
    depth = w_mod.shape[0]
    dff2 = w_up.shape[2]
    npg = page_table.shape[1]
    past = npg * PAGE
    alpha = (2 * depth) ** 0.25
    rows_p, rows_s = bsz * t, db * tn
    tm_p = 256
    tm_ffn = 512
    tq = 512
    kc_att = 512

    nc = bsz + db
    c_all = jnp.concatenate([c_prompt, c_sample, jnp.zeros((-nc % SUBLANES, d), F32)], axis=0)
    mod = _adaln_mod(c_all, w_mod, b_mod)

    xp = x_prompt.reshape(rows_p, d)
    xs = x_sample.reshape(rows_s, d)
    cos_p, sin_p = _rope_tables(jnp.arange(t, dtype=jnp.int32))
    cos_s, sin_s = _rope_tables(jnp.tile(past + jnp.arange(tn, dtype=jnp.int32), db))

    outs_p = {k: [] for k in ("ak", "av", "aki", "bk", "bv", "blf", "conv")}
    outs_s = {k: [] for k in ("ak", "av", "aki", "bk", "bv", "blf", "conv", "cv")}

    for i in range(depth):
        j = i // 2
        mp = mod[i, :bsz].reshape(bsz, 6, 1, d)
        ms = jnp.repeat(mod[i, bsz:nc].reshape(db, 6, d), tn, axis=0).reshape(1, rows_s, 6, d)
        sh1p, sc1p, g1p, sh2p, sc2p, g2p = (mp[:, k] for k in range(6))
        sh1s, sc1s, g1s, sh2s, sc2s, g2s = (ms[:, :, k] for k in range(6))

        if i % 2 == 0:
            w_packed = _pack_w_ab(w_in_ab[j])
            bf_row = jnp.zeros((1, LANES), F32).at[0, _TAIL_F:_TAIL_F + N_HEADS].set(b_forget[j])
            w_out = w_out_ab[j]

            (qa, kaf, kab, vaf, vab, qi, qb, kbf, kbb, vbf, vbb, tail, kid) = _proj_ab(
                xp, sc1p, sh1p, cos_p, sin_p, w_packed, bf_row, tm_p)
            r3 = lambda a: a.reshape(bsz, t, a.shape[-1])
            tail3 = r3(tail)
            ckb = _logf_cumsum(tail3, min(t, 512))
            b_out = _unchunk_t(_fox_prompt(_chunk_t(r3(qb), tq), r3(kbb), _chunk_t(r3(vbb), kc_att), ckb, 2))
            w_t = tail3[:, :, _TAIL_WI:_TAIL_WI + N_IDX_HEADS].transpose(0, 2, 1)
            a_out = _unchunk_t(_dsa_prompt(_chunk_t(r3(qi), tq), w_t, r3(kid), _chunk_t(r3(qa), tq), r3(kab),
                                           _chunk_t(r3(vab), kc_att), min(TOPK, t // 4)))
            xp = _out_ln(a_out.reshape(rows_p, _GW), b_out.reshape(rows_p, _GW), xp, g1p, w_out,
                         ln1_g[i], ln1_b[i], alpha, tm_p)
            heads = lambda a: a.reshape(bsz, t, N_HEADS, HEAD_DIM)
            outs_p["ak"].append(heads(kaf)); outs_p["av"].append(heads(vaf))
            outs_p["aki"].append(tail3[:, :, :D_IDX])
            outs_p["bk"].append(heads(kbf)); outs_p["bv"].append(heads(vbf))
            outs_p["blf"].append(tail3[:, :, _TAIL_F:_TAIL_F + N_HEADS])

            (qa, kaf, kab, vaf, vab, qi, qb, kbf, kbb, vbf, vbb, tail, kid) = _proj_ab(
                xs, sc1s, sh1s, cos_s, sin_s, w_packed, bf_row, rows_s)
            s3 = lambda a: a.reshape(db, tn, a.shape[-1])
            tail3 = s3(tail)
            hq = lambda a: a.reshape(db, tn, N_HEADS, HEAD_DIM).transpose(0, 2, 1, 3)
            qi_rows = hq(qi).reshape(db, N_IDX_HEADS * tn, D_IDX)
            w_hq = tail3[:, :, _TAIL_WI:_TAIL_WI + N_IDX_HEADS].transpose(0, 2, 1).reshape(db, N_IDX_HEADS * tn, 1)
            w_rows = jnp.broadcast_to(w_hq, (db, N_IDX_HEADS * tn, LANES))
            keys_t = lambda a: jnp.pad(a.transpose(0, 2, 1), ((0, 0), (0, 0), (0, PAGE - tn)))
            ki_new = keys_t(s3(kid)[:, :, :D_IDX])
            bias = _sample_indexer(page_table, qi_rows, w_rows, ki_new, cache_a_kidx[j].transpose(0, 2, 1),
                                   pages_per_step=min(npg, 16), ksel=min(TOPK, (past + tn) // 4))
            eye = jnp.eye(N_HEADS, dtype=BF16)
            bd = lambda a: jnp.einsum("bhqd,hg->bhqgd", hq(a), eye).reshape(db, N_HEADS * tn, _GW)
            lft_pool = cache_b_logf[j].transpose(0, 2, 1)
            lfn = jnp.pad(tail3[:, :, _TAIL_F:_TAIL_F + N_HEADS].transpose(0, 2, 1),
                          ((0, 0), (0, 0), (0, PAGE - tn)))
            pool = lambda cch: cch[j].transpose(0, 2, 3, 1).reshape(cch.shape[1], _GW, PAGE)
            a_out, b_out = _sample_attn(page_table, bd(qa), bd(qb), bias, lft_pool, lfn,
                                        keys_t(s3(kab)), keys_t(s3(vab)), keys_t(s3(kbb)), keys_t(s3(vbb)),
                                        pool(cache_a_k), pool(cache_a_v), pool(cache_b_k), pool(cache_b_v),
                                        pps=min(npg, 8))
            xs = _out_ln(a_out.reshape(rows_s, _GW), b_out.reshape(rows_s, _GW), xs, g1s, w_out,
                         ln1_g[i], ln1_b[i], alpha, rows_s)
            heads = lambda a: a.reshape(db, tn, N_HEADS, HEAD_DIM)
            outs_s["ak"].append(heads(kaf)); outs_s["av"].append(heads(vaf))
            outs_s["aki"].append(tail3[:, :, :D_IDX])
            outs_s["bk"].append(heads(kbf)); outs_s["bv"].append(heads(vbf))
            outs_s["blf"].append(tail3[:, :, _TAIL_F:_TAIL_F + N_HEADS])
        else:
            args = (w_in_c[j], lnv_g[j], lnv_b[j], w_spatial[j], b_spatial[j], w_out_c[j], ln1_g[i], ln1_b[i], alpha)
            (xp,) = _chunk_gmlp(xp, sc1p, sh1p, g1p, *args, tm_p, CHUNK, False)
            xs, cv = _chunk_gmlp(xs, sc1s, sh1s, g1s, *args, rows_s, tn, True)
            outs_s["cv"].append(cv.reshape(db, tn, -1))

        ffn_w = (w_up[i], w_conv[i], b_conv[i], w_down[i], ln2_g[i], ln2_b[i], alpha)
        xp, up_p = _conv_ffn(xp, sc2p, sh2p, g2p, jnp.zeros((bsz, SUBLANES, dff2), F32), *ffn_w, tm_ffn, t)
        outs_p["conv"].append(up_p.reshape(bsz, t // tm_ffn, SUBLANES, dff2)[:, -1, SUBLANES - (CONV_W - 1):])
        prev_s = jnp.pad(state_ffn_conv[i], ((0, 0), (SUBLANES - (CONV_W - 1), 0), (0, 0)))
        xs, up_s = _conv_ffn(xs, sc2s, sh2s, g2s, prev_s, *ffn_w, min(rows_s, 8 * tn), tn)
        outs_s["conv"].append(up_s.reshape(db, tn, dff2)[:, tn - (CONV_W - 1):])

    st = jnp.stack
    return (xp.reshape(bsz, t, d), xs.reshape(db, tn, d),
            st(outs_p["ak"]), st(outs_p["av"]), st(outs_p["aki"]), st(outs_p["bk"]), st(outs_p["bv"]),
            st(outs_p["blf"]), st(outs_p["conv"]),
            st(outs_s["ak"]), st(outs_s["av"]), st(outs_s["aki"]), st(outs_s["bk"]), st(outs_s["bv"]),
            st(outs_s["blf"]), st(outs_s["conv"]), st(outs_s["cv"]))
```

```python
import functools

import jax
import jax.numpy as jnp
from jax import lax
from jax.experimental import pallas as pl
from jax.experimental.pallas import tpu as pltpu

F32 = jnp.float32
BF16 = jnp.bfloat16

HEAD_DIM = 64
N_HEADS = 8
N_IDX_HEADS = 8
D_IDX = 64
TOPK = 256
PAGE = 128
CHUNK = 128
N_GROUPS_C = 8
CONV_W = 3
LN_EPS = 1e-5
ROPE_THETA = 10000.0
NEG = -1e30

LANES = 128
SUBLANES = 8
VMEM_LIMIT = 48 * 1024 * 1024

_NT = (((1,), (1,)), ((), ()))


def _cparams(n_axes):
    return pltpu.CompilerParams(dimension_semantics=("arbitrary",) * n_axes,
                                vmem_limit_bytes=VMEM_LIMIT)


def _const_spec(arr):
    return pl.BlockSpec(arr.shape, lambda i: (0,) * arr.ndim, pipeline_mode=pl.Buffered(1))


def _mod_spec(m, tm, nt):
    d = m.shape[2]
    if m.shape[1] == 1:
        tiles_per_group = nt // m.shape[0]
        return pl.BlockSpec((1, 1, d), lambda i: (i // tiles_per_group, 0, 0))
    return pl.BlockSpec((1, tm, d), lambda i: (0, i, 0))


def _ln(z, g, b):
    mu = jnp.mean(z, axis=-1, keepdims=True)
    zc = z - mu
    var = jnp.mean(zc * zc, axis=-1, keepdims=True)
    return zc * lax.rsqrt(var + LN_EPS) * g + b


def _gelu(x):
    return 0.5 * x * (1.0 + lax.erf(x * (2.0 ** -0.5)))


def _split3(x):
    hi = x.astype(BF16)
    r = x - hi.astype(F32)
    mid = r.astype(BF16)
    lo = (r - mid.astype(F32)).astype(BF16)
    return hi, mid, lo


def _dot3(x, m, left):
    parts = _split3(x)
    out = None
    for p in parts:
        d = (jnp.dot(m, p, preferred_element_type=F32) if left
             else jnp.dot(p, m, preferred_element_type=F32))
        out = d if out is None else out + d
    return out


def _mod_kernel(c_ref, w_ref, b_ref, o_ref):
    c = c_ref[...]
    a = (c * jax.nn.sigmoid(c)).astype(BF16)
    o_ref[0] = jnp.dot(a, w_ref[0].astype(BF16), preferred_element_type=F32) + b_ref[0]


def _adaln_mod(c_all, w_mod, b_mod):
    depth, d, n6 = w_mod.shape
    rows = c_all.shape[0]
    tn = n6 // 4
    return pl.pallas_call(
        _mod_kernel,
        grid=(depth, n6 // tn),
        in_specs=[pl.BlockSpec((rows, d), lambda l, j: (0, 0)),
                  pl.BlockSpec((1, d, tn), lambda l, j: (l, 0, j)),
                  pl.BlockSpec((1, 1, tn), lambda l, j: (l, 0, j))],
        out_specs=pl.BlockSpec((1, rows, tn), lambda l, j: (l, 0, j)),
        out_shape=jax.ShapeDtypeStruct((depth, rows, n6), F32),
        compiler_params=_cparams(2),
        name="adaln_mod",
    )(c_all, w_mod, b_mod.reshape(depth, 1, n6))


_G_QA, _G_KA, _G_VA, _G_QI, _G_QB, _G_KB, _G_VB = range(7)
_GW = N_HEADS * HEAD_DIM
_TAIL_OFF = 7 * _GW
_TAIL_WI = D_IDX
_TAIL_F = D_IDX + N_IDX_HEADS


def _pack_w_ab(w):
    sizes = (_GW, _GW, _GW, N_IDX_HEADS * D_IDX, D_IDX, N_IDX_HEADS, _GW, _GW, _GW, N_HEADS)
    offs, acc = [], 0
    for s in sizes[:-1]:
        acc += s
        offs.append(acc)
    qa, ka, va, qi, ki, wi, qb, kb, vb, fb = jnp.split(w, offs, axis=1)
    pad = jnp.zeros((w.shape[0], LANES - D_IDX - N_IDX_HEADS - N_HEADS), w.dtype)
    return jnp.concatenate([qa, ka, va, qi, qb, kb, vb, ki, wi, fb, pad], axis=1).astype(BF16)


def _rope_tables(pos):
    half = HEAD_DIM // 2
    inv = ROPE_THETA ** (-jnp.arange(half, dtype=F32) / half)
    ang = pos.astype(F32)[:, None] * inv[None, :]
    cos, sin = jnp.cos(ang), jnp.sin(ang)
    return (jnp.concatenate([cos] * 4, axis=1),
            jnp.concatenate([-sin, sin, -sin, sin], axis=1))


def _proj_ab_kernel(x_ref, sc_ref, sh_ref, cos_ref, sin_ref, w_ref, bf_ref,
                    qa_ref, kaf_ref, kab_ref, vaf_ref, vab_ref, qi_ref,
                    qb_ref, kbf_ref, kbb_ref, vbf_ref, vbb_ref, tail_ref, kid_ref, *, transposed):
    tm = x_ref.shape[0]

    def rows(val, ref):
        ref[...] = val.astype(ref.dtype)

    def emit(val, *refs):
        if not transposed:
            for ref in refs:
                rows(val, ref)
            return
        val_t = val.T
        for ref in refs:
            if len(ref.shape) == 5:
                ref[0, :, 0] = val_t.reshape(ref.shape[1], LANES, tm).astype(ref.dtype)
            else:
                ref[0] = val_t.astype(ref.dtype)

    h = (x_ref[...] * (1.0 + sc_ref[0]) + sh_ref[0]).astype(BF16)
    c1, s1 = cos_ref[...], sin_ref[...]
    c4 = jnp.concatenate([c1] * 4, axis=1)
    s4 = jnp.concatenate([s1] * 4, axis=1)
    lane4 = lax.broadcasted_iota(jnp.int32, (tm, _GW), 1)
    first4 = (lane4 & (HEAD_DIM - 1)) < HEAD_DIM // 2
    q_scale = HEAD_DIM ** -0.5

    def proj(g):
        return jnp.dot(h, w_ref[:, g * _GW:(g + 1) * _GW], preferred_element_type=F32)

    def rope(p):
        rot = jnp.where(first4, pltpu.roll(p, _GW - HEAD_DIM // 2, 1), pltpu.roll(p, HEAD_DIM // 2, 1))
        return p * c4 + rot * s4

    emit(rope(proj(_G_QA)) * q_scale, qa_ref)
    ka = rope(proj(_G_KA))
    emit(ka, kaf_ref)
    rows(ka, kab_ref)
    emit(proj(_G_VA), vaf_ref, vab_ref)
    emit(rope(proj(_G_QI)) * (D_IDX ** -0.5), qi_ref)
    emit(proj(_G_QB) * q_scale, qb_ref)
    kb = proj(_G_KB)
    emit(kb, kbf_ref)
    rows(kb, kbb_ref)
    emit(proj(_G_VB), vbf_ref, vbb_ref)

    t = jnp.dot(h, w_ref[:, _TAIL_OFF:_TAIL_OFF + LANES], preferred_element_type=F32)
    lane = lax.broadcasted_iota(jnp.int32, (tm, LANES), 1)
    first = (lane & (HEAD_DIM - 1)) < HEAD_DIM // 2
    rot = jnp.where(first, pltpu.roll(t, LANES - HEAD_DIM // 2, 1), pltpu.roll(t, HEAD_DIM // 2, 1))
    roped = t * c1 + rot * s1
    z = t + bf_ref[...]
    logf = jnp.minimum(z, 0.0) - jnp.log1p(jnp.exp(-jnp.abs(z)))
    tail = jnp.where(lane < D_IDX, roped,
                     jnp.where(lane < _TAIL_F, t * (N_IDX_HEADS ** -0.5),
                               jnp.where(lane < _TAIL_F + N_HEADS, logf, 0.0)))
    tail_ref[...] = tail
    kid_ref[...] = jnp.where(lane < D_IDX, roped, pltpu.roll(roped, D_IDX, 1)).astype(BF16)


def _proj_ab(x, sc, sh, cos_t, sin_t, w_packed, bf_row, tm, transposed_batch=None):
    rows, d = x.shape
    nt = rows // tm
    t_tiles = cos_t.shape[0] // tm
    mod_spec = _mod_spec(sc, tm, nt)
    rope_spec = pl.BlockSpec((tm, LANES), lambda i: (i % t_tiles, 0))
    narrow = lambda: pl.BlockSpec((tm, LANES), lambda i: (i, 0))
    row_spec = pl.BlockSpec((tm, _GW), lambda i: (i, 0))
    row_f, row_b = jax.ShapeDtypeStruct((rows, _GW), F32), jax.ShapeDtypeStruct((rows, _GW), BF16)
    if transposed_batch is None:
        spec_f = spec_c = row_spec
        shape_f, shape_c = row_f, row_b
    else:
        bsz = transposed_batch
        tpb = nt // bsz
        spec_f = pl.BlockSpec((1, _GW, tm), lambda i: (i // tpb, 0, i % tpb))
        spec_c = pl.BlockSpec((1, _PAIRS, 1, LANES, tm), lambda i: (i // tpb, 0, i % tpb, 0, 0))
        shape_f = jax.ShapeDtypeStruct((bsz, _GW, tpb * tm), F32)
        shape_c = jax.ShapeDtypeStruct((bsz, _PAIRS, tpb, LANES, tm), BF16)
    specs = [spec_c, spec_f, row_spec, spec_f, spec_c, spec_c, spec_c, spec_f, row_spec, spec_f, spec_c]
    shapes = [shape_c, shape_f, row_b, shape_f, shape_c, shape_c, shape_c, shape_f, row_b, shape_f, shape_c]
    out_shape = tuple(shapes) + (jax.ShapeDtypeStruct((rows, LANES), F32), jax.ShapeDtypeStruct((rows, LANES), BF16))
    return pl.pallas_call(
        functools.partial(_proj_ab_kernel, transposed=transposed_batch is not None),
        grid=(nt,),
        in_specs=[pl.BlockSpec((tm, d), lambda i: (i, 0)), mod_spec, mod_spec, rope_spec, rope_spec,
                  _const_spec(w_packed), _const_spec(bf_row)],
        out_specs=tuple(specs + [narrow(), narrow()]),
        out_shape=out_shape,
        compiler_params=_cparams(1),
        name="proj_ab",
    )(x, sc, sh, cos_t, sin_t, w_packed, bf_row)


def _cumsum_kernel(t_ref, o_ref, carry):
    @pl.when(pl.program_id(1) == 0)
    def _():
        carry[...] = jnp.zeros(carry.shape, F32)

    r = lax.broadcasted_iota(jnp.int32, (LANES, LANES), 0)
    c = lax.broadcasted_iota(jnp.int32, (LANES, LANES), 1)
    tri = jnp.where(c <= r, 1.0, 0.0).astype(BF16)
    run = carry[...]
    for k in range(t_ref.shape[1] // LANES):
        x = t_ref[0, k * LANES:(k + 1) * LANES, :]
        cum = _dot3(x, tri, left=True) + run
        run = cum[LANES - 1:LANES, :]
        for h in range(N_HEADS):
            o_ref[0, h, k * LANES:(k + 1) * LANES, :] = jnp.broadcast_to(
                cum[:, _TAIL_F + h:_TAIL_F + h + 1], (LANES, LANES))
    carry[...] = run


def _logf_cumsum(tail, tc):
    bsz, t, _ = tail.shape
    return pl.pallas_call(
        _cumsum_kernel,
        grid=(bsz, t // tc),
        in_specs=[pl.BlockSpec((1, tc, LANES), lambda b, i: (b, i, 0))],
        out_specs=pl.BlockSpec((1, N_HEADS, tc, LANES), lambda b, i: (b, 0, i, 0)),
        out_shape=jax.ShapeDtypeStruct((bsz, N_HEADS, t, LANES), F32),
        scratch_shapes=[pltpu.VMEM((1, LANES), F32)],
        compiler_params=_cparams(2),
        name="logf_cumsum",
    )(tail)


_PAIRS = N_HEADS // 2


def _chunk_t(a, width):
    bsz, t, _ = a.shape
    return a.reshape(bsz, t // width, width, _PAIRS, LANES).transpose(0, 3, 1, 4, 2)


def _unchunk_t(a):
    bsz, _, n, _, width = a.shape
    return a.transpose(0, 2, 4, 1, 3).reshape(bsz, n * width, _GW)


def _head_halves(x_t, top):
    zero = jnp.zeros_like(x_t)
    return jnp.where(top, x_t, zero), jnp.where(top, zero, x_t)


def _pair_step_t(q0, q1, kc, v_t, b0, b1, carry, top):
    m0, l0, m1, l1, acc = carry
    s0 = jnp.dot(kc, q0, preferred_element_type=F32) + b0
    s1 = jnp.dot(kc, q1, preferred_element_type=F32) + b1
    n0 = jnp.maximum(m0, jnp.max(s0, axis=0, keepdims=True))
    n1 = jnp.maximum(m1, jnp.max(s1, axis=0, keepdims=True))
    a0, a1 = jnp.exp(m0 - n0), jnp.exp(m1 - n1)
    p0, p1 = jnp.exp(s0 - n0), jnp.exp(s1 - n1)
    l0 = a0 * l0 + jnp.sum(p0, axis=0, keepdims=True)
    l1 = a1 * l1 + jnp.sum(p1, axis=0, keepdims=True)
    pv0 = jnp.dot(v_t, p0.astype(BF16), preferred_element_type=F32)
    pv1 = jnp.dot(v_t, p1.astype(BF16), preferred_element_type=F32)
    acc = jnp.where(top, a0 * acc + pv0, a1 * acc + pv1)
    return n0, l0, n1, l1, acc


def _pair_init_t(tq):
    row = lambda v: jnp.full((1, tq), v, F32)
    return row(-jnp.inf), row(0.0), row(-jnp.inf), row(0.0), jnp.zeros((LANES, tq), F32)


def _pair_finish_t(carry, top):
    _, l0, _, l1, acc = carry
    return acc / jnp.where(top, l0, l1)


def _causal_loop(it, tq, kc, step, init):
    per_tile = tq // kc
    carry = lax.fori_loop(0, it * per_tile, lambda j, c: step(j, c, None), init)
    for d in range(per_tile):
        carry = step(it * per_tile + d, carry, d * kc)
    return carry


def _fox_kernel(q_ref, k_ref, v_ref, ck_ref, o_ref):
    npair, tq, kc = q_ref.shape[1], q_ref.shape[-1], v_ref.shape[-1]
    it = pl.program_id(2)
    top = lax.broadcasted_iota(jnp.int32, (LANES, tq), 0) < HEAD_DIM
    qs = [_head_halves(q_ref[0, p, 0], top) for p in range(npair)]
    key = lax.broadcasted_iota(jnp.int32, (kc, tq), 0)
    qry = lax.broadcasted_iota(jnp.int32, (kc, tq), 1)
    wide = lambda x: jnp.concatenate([x] * (tq // LANES), axis=1)

    def step(j, carries, diag_off):
        off = pl.multiple_of(j * kc, kc)
        causal = None if diag_off is None else key + diag_off <= qry
        out = []
        for p in range(npair):
            b0 = -wide(ck_ref[0, 2 * p, pl.ds(off, kc), :])
            b1 = -wide(ck_ref[0, 2 * p + 1, pl.ds(off, kc), :])
            if causal is not None:
                b0, b1 = jnp.where(causal, b0, NEG), jnp.where(causal, b1, NEG)
            out.append(_pair_step_t(qs[p][0], qs[p][1], k_ref[0, pl.ds(off, kc), p * LANES:(p + 1) * LANES],
                                    v_ref[0, p, j], b0, b1, carries[p], top))
        return tuple(out)

    carries = _causal_loop(it, tq, kc, step, tuple(_pair_init_t(tq) for _ in range(npair)))
    for p in range(npair):
        o_ref[0, p, 0] = _pair_finish_t(carries[p], top).astype(o_ref.dtype)


def _fox_prompt(q_t, kb, v_t, ckb, pairs_per_step):
    bsz, _, nq, _, tq = q_t.shape
    nk, kc = v_t.shape[2], v_t.shape[4]
    t = kb.shape[1]
    pp = pairs_per_step
    return pl.pallas_call(
        _fox_kernel,
        grid=(bsz, _PAIRS // pp, nq),
        in_specs=[pl.BlockSpec((1, pp, 1, LANES, tq), lambda b, g, i: (b, g, i, 0, 0)),
                  pl.BlockSpec((1, t, pp * LANES), lambda b, g, i: (b, 0, g)),
                  pl.BlockSpec((1, pp, nk, LANES, kc), lambda b, g, i: (b, g, 0, 0, 0)),
                  pl.BlockSpec((1, 2 * pp, t, LANES), lambda b, g, i: (b, g, 0, 0))],
        out_specs=pl.BlockSpec((1, pp, 1, LANES, tq), lambda b, g, i: (b, g, i, 0, 0)),
        out_shape=jax.ShapeDtypeStruct(q_t.shape, BF16),
        compiler_params=_cparams(3),
        name="fox_prompt",
    )(q_t, kb, v_t, ckb)


def _topk_to_bias(sc_ref, nch, nvalid, ksel, keys_axis, n_bisect=16):
    kc = sc_ref.shape[1 + keys_axis]
    kf = float(ksel)
    ninf = -jnp.inf
    stat = lambda v: jnp.full(nvalid.shape, v, F32)

    if keys_axis == 1:
        def over_keys(elem, red, combine, init):
            return red(red(elem(sc_ref[...]), axis=0), axis=1, keepdims=True)
    else:
        def over_keys(elem, red, combine, init):
            return lax.fori_loop(
                0, nch, lambda j, a: combine(a, red(elem(sc_ref[j]), axis=0, keepdims=True)), stat(init))

    def count(t, strict):
        hit = (lambda x: jnp.where(x > t, 1.0, 0.0)) if strict else (lambda x: jnp.where(x >= t, 1.0, 0.0))
        return over_keys(hit, jnp.sum, jnp.add, 0.0)

    def count_ge(t):
        return count(t, False)

    def max_below(h):
        return over_keys(lambda x: jnp.where(x < h, x, ninf), jnp.max, jnp.maximum, ninf)

    def min_valid():
        return over_keys(lambda x: jnp.where(x == ninf, jnp.inf, x), jnp.min, jnp.minimum, jnp.inf)

    small = nvalid <= kf
    lo = min_valid()
    v = max_below(stat(jnp.inf))
    ok = count_ge(v) >= kf
    res = jnp.where(jnp.logical_or(small, ok), 1.0, 0.0)
    thr = jnp.where(small, ninf, v)
    hi = jnp.where(res > 0.0, jnp.inf, v)

    def bisect(_, st):
        lo, hi = st
        mid = 0.5 * (lo + hi)
        ge = count_ge(mid) >= kf
        return jnp.where(ge, mid, lo), jnp.where(ge, hi, mid)

    lo, hi = lax.fori_loop(0, n_bisect, bisect, (lo, hi))

    def snap_cond(st):
        return jnp.min(st[2]) < 1.0

    def snap(st):
        thr, hi, res = st
        v = max_below(hi)
        ok = count_ge(v) >= kf
        open_ = res < 1.0
        new = jnp.logical_and(open_, ok)
        drop = jnp.logical_and(open_, jnp.logical_not(ok))
        return jnp.where(new, v, thr), jnp.where(drop, v, hi), jnp.where(ok, 1.0, res)

    thr, hi, res = lax.while_loop(snap_cond, snap, (thr, hi, res))

    need = kf - count(thr, True)
    r = lax.broadcasted_iota(jnp.int32, (kc, kc), 0)
    c = lax.broadcasted_iota(jnp.int32, (kc, kc), 1)
    prefix = jnp.where((r < c) if keys_axis == 1 else (c < r), 1.0, 0.0).astype(BF16)

    def ties(s):
        tie = jnp.logical_and(s == thr, s > ninf)
        return tie, jnp.where(tie, 1.0, 0.0)

    def write_bias(j, s, tie, before):
        sel = jnp.logical_or(s > thr, jnp.logical_and(tie, before < need))
        sc_ref[j] = jnp.where(sel, 0.0, NEG)

    if keys_axis == 1:
        s = sc_ref[...]
        tie, tief = ties(s)
        rows = s.shape[1]
        within = jnp.dot(tief.reshape(nch * rows, kc).astype(BF16), prefix,
                         preferred_element_type=F32).reshape(nch, rows, kc)
        totals = jnp.sum(tief, axis=2, keepdims=True)
        run = stat(0.0)
        for j in range(nch):
            write_bias(j, s[j], tie[j], within[j] + run)
            run = run + totals[j]
    else:
        def to_bias(j, run):
            s = sc_ref[j]
            tie, tief = ties(s)
            before = jnp.dot(prefix, tief.astype(BF16), preferred_element_type=F32) + run
            write_bias(j, s, tie, before)
            return run + jnp.sum(tief, axis=0, keepdims=True)

        lax.fori_loop(0, nch, to_bias, stat(0.0))


def _dsa_kernel(qi_ref, w_ref, kid_ref, qa_ref, ka_ref, va_ref, o_ref, sc_ref, *, ksel):
    tq, kc = qi_ref.shape[-1], va_ref.shape[-1]
    it = pl.program_id(1)
    top = lax.broadcasted_iota(jnp.int32, (LANES, tq), 0) < HEAD_DIM
    key = lax.broadcasted_iota(jnp.int32, (kc, tq), 0)
    qry = lax.broadcasted_iota(jnp.int32, (kc, tq), 1)

    qm = []
    for p in range(N_IDX_HEADS // 2):
        qm.extend(_head_halves(qi_ref[0, p, 0], top))
    w = w_ref[0]

    def score_chunk(j, carry, diag_off):
        off = pl.multiple_of(j * kc, kc)
        keys = kid_ref[0, pl.ds(off, kc), :]
        sc = jnp.zeros((kc, tq), F32)
        for h in range(N_IDX_HEADS):
            lg = jnp.dot(keys, qm[h], preferred_element_type=F32)
            sc = sc + jnp.maximum(lg, 0.0) * w[h:h + 1, :]
        if diag_off is not None:
            sc = jnp.where(key + diag_off <= qry, sc, -jnp.inf)
        sc_ref[j] = sc
        return carry

    _causal_loop(it, tq, kc, score_chunk, 0)
    nch = (it + 1) * (tq // kc)
    nvalid = (it * tq + 1 + lax.broadcasted_iota(jnp.int32, (1, tq), 1)).astype(F32)
    _topk_to_bias(sc_ref, nch, nvalid, ksel, keys_axis=0)

    qs = [_head_halves(qa_ref[0, p, 0], top) for p in range(_PAIRS)]

    def step(j, carries):
        off = pl.multiple_of(j * kc, kc)
        bias = sc_ref[j]
        return tuple(
            _pair_step_t(qs[p][0], qs[p][1], ka_ref[0, pl.ds(off, kc), p * LANES:(p + 1) * LANES], va_ref[0, p, j],
                         bias, bias, carries[p], top)
            for p in range(_PAIRS))

    carries = lax.fori_loop(0, nch, step, tuple(_pair_init_t(tq) for _ in range(_PAIRS)))
    for p in range(_PAIRS):
        o_ref[0, p, 0] = _pair_finish_t(carries[p], top).astype(o_ref.dtype)


def _dsa_prompt(qi_t, w_t, kid, qa_t, ka, va_t, ksel):
    bsz, _, nq, _, tq = qa_t.shape
    nk, kc = va_t.shape[2], va_t.shape[4]
    t = ka.shape[1]
    q_spec = pl.BlockSpec((1, _PAIRS, 1, LANES, tq), lambda b, i: (b, 0, i, 0, 0))
    return pl.pallas_call(
        functools.partial(_dsa_kernel, ksel=ksel),
        grid=(bsz, nq),
        in_specs=[q_spec,
                  pl.BlockSpec((1, N_IDX_HEADS, tq), lambda b, i: (b, 0, i)),
                  pl.BlockSpec((1, t, LANES), lambda b, i: (b, 0, 0)),
                  q_spec,
                  pl.BlockSpec((1, t, _GW), lambda b, i: (b, 0, 0)),
                  pl.BlockSpec((1, _PAIRS, nk, LANES, kc), lambda b, i: (b, 0, 0, 0, 0))],
        out_specs=q_spec,
        out_shape=jax.ShapeDtypeStruct(qa_t.shape, BF16),
        scratch_shapes=[pltpu.VMEM((nk, kc, tq), F32)],
        compiler_params=_cparams(2),
        name="dsa_prompt",
    )(qi_t, w_t, kid, qa_t, ka, va_t)


def _sample_idx_kernel(pt_ref, qi_ref, w_ref, knew_ref, *rest, pages_per_step, ksel):
    page_refs = rest[:pages_per_step]
    o_ref = rest[pages_per_step]
    sc_ref = rest[pages_per_step + 1]
    s = pl.program_id(1)
    nsteps = pl.num_programs(1)
    npg = sc_ref.shape[0] - 1
    tn = sc_ref.shape[1]
    qi = qi_ref[0]
    w = w_ref[0]

    def scores(keys_t):
        lg = jnp.dot(qi, keys_t, preferred_element_type=F32)
        z = jnp.maximum(lg, 0.0) * w
        out = z[0:tn]
        for h in range(1, N_IDX_HEADS):
            out = out + z[h * tn:(h + 1) * tn]
        return out

    for i, pr in enumerate(page_refs):
        sc_ref[s * pages_per_step + i] = scores(pr[0].astype(BF16))

    @pl.when(s == nsteps - 1)
    def _():
        r = lax.broadcasted_iota(jnp.int32, (tn, PAGE), 0)
        c = lax.broadcasted_iota(jnp.int32, (tn, PAGE), 1)
        sc_ref[npg] = jnp.where(c <= r, scores(knew_ref[0]), -jnp.inf)
        nvalid = (npg * PAGE + 1 + lax.broadcasted_iota(jnp.int32, (tn, 1), 0)).astype(F32)
        _topk_to_bias(sc_ref, npg + 1, nvalid, ksel, keys_axis=1)
        o_ref[0] = sc_ref[...]


def _sample_indexer(page_table, qi_rows, w_rows, ki_new_pad, kidx_pool, pages_per_step, ksel):
    db, npg = page_table.shape
    tn = qi_rows.shape[1] // N_IDX_HEADS
    nsteps = npg // pages_per_step

    def page_spec(i):
        return pl.BlockSpec((1, D_IDX, PAGE), lambda b, s, pt: (pt[b, s * pages_per_step + i], 0, 0))

    grid_spec = pltpu.PrefetchScalarGridSpec(
        num_scalar_prefetch=1,
        grid=(db, nsteps),
        in_specs=[pl.BlockSpec((1,) + qi_rows.shape[1:], lambda b, s, pt: (b, 0, 0)),
                  pl.BlockSpec((1,) + w_rows.shape[1:], lambda b, s, pt: (b, 0, 0)),
                  pl.BlockSpec((1, D_IDX, PAGE), lambda b, s, pt: (b, 0, 0))]
                 + [page_spec(i) for i in range(pages_per_step)],
        out_specs=pl.BlockSpec((1, npg + 1, tn, PAGE), lambda b, s, pt: (b, 0, 0, 0)),
        scratch_shapes=[pltpu.VMEM((npg + 1, tn, PAGE), F32)],
    )
    return pl.pallas_call(
        functools.partial(_sample_idx_kernel, pages_per_step=pages_per_step, ksel=ksel),
        grid_spec=grid_spec,
        out_shape=jax.ShapeDtypeStruct((db, npg + 1, tn, PAGE), F32),
        compiler_params=_cparams(2),
        name="sample_indexer",
    )(page_table, qi_rows, w_rows, ki_new_pad, *([kidx_pool] * pages_per_step))


def _sample_attn_kernel(pt_ref, qa_ref, qb_ref, bias_ref, biasn_ref, lfn_ref,
                        kan_ref, van_ref, kbn_ref, vbn_ref, *rest, pps):
    lft_refs, ka_refs, va_refs, kb_refs, vb_refs = (rest[i * pps:(i + 1) * pps] for i in range(5))
    oa_ref, ob_ref, ma, la, acca, mb, lb, accb, run = rest[5 * pps:]
    p = pl.program_id(1)
    nsteps = pl.num_programs(1)
    rows = qa_ref.shape[1]
    tn = rows // N_HEADS

    @pl.when(p == 0)
    def _():
        for m in (ma, mb):
            m[...] = jnp.full(m.shape, -jnp.inf, F32)
        for z in (la, lb, acca, accb, run):
            z[...] = jnp.zeros(z.shape, F32)

    r = lax.broadcasted_iota(jnp.int32, (PAGE, PAGE), 0)
    c = lax.broadcasted_iota(jnp.int32, (PAGE, PAGE), 1)
    incl = jnp.where(r <= c, 1.0, 0.0).astype(BF16)

    def update(q, kts, vts, biases, m_ref, l_ref, acc_ref):
        s = [jnp.dot(q, kt, preferred_element_type=F32) + b for kt, b in zip(kts, biases)]
        smax = functools.reduce(jnp.maximum, s)
        m_old = m_ref[...]
        m_new = jnp.maximum(m_old, jnp.max(smax, axis=1, keepdims=True))
        a = jnp.exp(m_old - m_new)
        ps = [jnp.exp(x - m_new) for x in s]
        l_ref[...] = a * l_ref[...] + jnp.sum(functools.reduce(jnp.add, ps), axis=1, keepdims=True)
        pv = [lax.dot_general(pr.astype(BF16), vt, _NT, preferred_element_type=F32) for pr, vt in zip(ps, vts)]
        acc_ref[...] = a * acc_ref[...] + functools.reduce(jnp.add, pv)
        m_ref[...] = m_new

    def head_rows(x):
        return jnp.concatenate([jnp.broadcast_to(x[h:h + 1], (tn, x.shape[1])) for h in range(N_HEADS)], axis=0)

    def tile_rows(x):
        return jnp.concatenate([x] * N_HEADS, axis=0)

    def forget_biases(lf_ts):
        out, run_v = [], run[...]
        for lf_t in lf_ts:
            cum = _dot3(head_rows(lf_t), incl, left=False) + run_v
            run_v = cum[:, PAGE - 1:PAGE]
            out.append(-cum)
        run[...] = run_v
        return out

    cast = lambda refs: [x[0].astype(BF16) for x in refs]
    update(qa_ref[0], cast(ka_refs), cast(va_refs), [tile_rows(bias_ref[0, i]) for i in range(pps)], ma, la, acca)
    update(qb_ref[0], cast(kb_refs), cast(vb_refs), forget_biases([x[0] for x in lft_refs]), mb, lb, accb)

    @pl.when(p == nsteps - 1)
    def _():
        update(qa_ref[0], [kan_ref[0]], [van_ref[0]], [tile_rows(biasn_ref[0, 0])], ma, la, acca)
        qrow = lax.broadcasted_iota(jnp.int32, (rows, PAGE), 0) & (tn - 1)
        key = lax.broadcasted_iota(jnp.int32, (rows, PAGE), 1)
        fb = jnp.where(key <= qrow, forget_biases([lfn_ref[0]])[0], NEG)
        update(qb_ref[0], [kbn_ref[0]], [vbn_ref[0]], [fb], mb, lb, accb)
        lane = lax.broadcasted_iota(jnp.int32, (tn, _GW), 1)
        for l_ref, acc_ref, o_ref in ((la, acca, oa_ref), (lb, accb, ob_ref)):
            full = acc_ref[...] / l_ref[...]
            out = jnp.zeros((tn, _GW), F32)
            for h in range(N_HEADS):
                mine = jnp.logical_and(lane >= h * HEAD_DIM, lane < (h + 1) * HEAD_DIM)
                out = jnp.where(mine, full[h * tn:(h + 1) * tn], out)
            o_ref[0] = out.astype(o_ref.dtype)


def _sample_attn(page_table, qa_bd, qb_bd, bias, lft_pool, lfn, kan, van, kbn, vbn,
                 ka_pool, va_pool, kb_pool, vb_pool, pps):
    db, npg = page_table.shape
    rows = qa_bd.shape[1]
    tn = rows // N_HEADS
    per_b = lambda shp: pl.BlockSpec((1,) + shp, lambda b, p, pt: (b,) + (0,) * len(shp))

    def paged(shp):
        return [pl.BlockSpec((1,) + shp, lambda b, p, pt, i=i: (pt[b, p * pps + i],) + (0,) * len(shp))
                for i in range(pps)]

    grid_spec = pltpu.PrefetchScalarGridSpec(
        num_scalar_prefetch=1,
        grid=(db, npg // pps),
        in_specs=[per_b((rows, _GW)), per_b((rows, _GW)),
                  pl.BlockSpec((1, pps, tn, PAGE), lambda b, p, pt: (b, p, 0, 0)),
                  pl.BlockSpec((1, 1, tn, PAGE), lambda b, p, pt: (b, npg, 0, 0)),
                  per_b((N_HEADS, PAGE)),
                  per_b((_GW, PAGE)), per_b((_GW, PAGE)), per_b((_GW, PAGE)), per_b((_GW, PAGE))]
                 + paged((N_HEADS, PAGE)) + paged((_GW, PAGE)) + paged((_GW, PAGE))
                 + paged((_GW, PAGE)) + paged((_GW, PAGE)),
        out_specs=(per_b((tn, _GW)), per_b((tn, _GW))),
        scratch_shapes=[pltpu.VMEM((rows, 1), F32), pltpu.VMEM((rows, 1), F32), pltpu.VMEM((rows, _GW), F32),
                        pltpu.VMEM((rows, 1), F32), pltpu.VMEM((rows, 1), F32), pltpu.VMEM((rows, _GW), F32),
                        pltpu.VMEM((rows, 1), F32)],
    )
    o = jax.ShapeDtypeStruct((db, tn, _GW), BF16)
    rep = lambda a: [a] * pps
    return pl.pallas_call(
        functools.partial(_sample_attn_kernel, pps=pps),
        grid_spec=grid_spec,
        out_shape=(o, o),
        compiler_params=_cparams(2),
        name="sample_attn",
    )(page_table, qa_bd, qb_bd, bias, bias, lfn, kan, van, kbn, vbn,
      *rep(lft_pool), *rep(ka_pool), *rep(va_pool), *rep(kb_pool), *rep(vb_pool))


def _out_ln_kernel(a_ref, b_ref, x_ref, g_ref, wa_ref, wb_ref, lg_ref, lb_ref, o_ref, *, alpha):
    mix = (jnp.dot(a_ref[...], wa_ref[...], preferred_element_type=F32)
           + jnp.dot(b_ref[...], wb_ref[...], preferred_element_type=F32))
    z = alpha * x_ref[...] + (1.0 + g_ref[0]) * mix
    o_ref[...] = _ln(z, lg_ref[...], lb_ref[...])


def _out_ln(a, b, x, gate, w_out, ln_g, ln_b, alpha, tm):
    rows, d = x.shape
    nt = rows // tm
    ka = a.shape[1]
    wa, wb = w_out[:ka].astype(BF16), w_out[ka:].astype(BF16)
    const = _const_spec
    lg, lb = ln_g.reshape(1, d), ln_b.reshape(1, d)
    return pl.pallas_call(
        functools.partial(_out_ln_kernel, alpha=alpha),
        grid=(nt,),
        in_specs=[pl.BlockSpec((tm, ka), lambda i: (i, 0)), pl.BlockSpec((tm, b.shape[1]), lambda i: (i, 0)),
                  pl.BlockSpec((tm, d), lambda i: (i, 0)),
                  _mod_spec(gate, tm, nt),
                  const(wa), const(wb), const(lg), const(lb)],
        out_specs=pl.BlockSpec((tm, d), lambda i: (i, 0)),
        out_shape=jax.ShapeDtypeStruct((rows, d), F32),
        compiler_params=_cparams(1),
        name="out_ln",
    )(a, b, x, gate, wa, wb, lg, lb)


_FF_CW = 2816


def _ffn_kernel(x_ref, sc_ref, sh_ref, g_ref, prev_ref, wg_ref, wv_ref, cw_ref, wd_ref, lg_ref, lb_ref,
                o_ref, up_ref, bufg, bufv, carry, *, alpha, nseq, tiles_per_seq):
    tm, d = x_ref.shape
    ts = tm // nseq
    nch, _, cw = wg_ref.shape
    dff = nch * cw
    x = x_ref[...]
    h = (x * (1.0 + sc_ref[0]) + sh_ref[0]).astype(BF16)
    chained = tiles_per_seq > 1

    if chained:
        @pl.when(pl.program_id(0) % tiles_per_seq == 0)
        def _():
            carry[...] = prev_ref[0]

    acc = jnp.zeros((tm, d), F32)
    for c in range(nch):
        cwr = cw_ref[c]
        ys = []
        for half, (w_ref, buf) in enumerate(((wg_ref, bufg), (wv_ref, bufv))):
            lo = half * dff + c * cw
            u = jnp.dot(h, w_ref[c], preferred_element_type=F32)
            if chained:
                buf[0, 0:SUBLANES, :] = carry[:, lo:lo + cw]
                carry[:, lo:lo + cw] = u[tm - SUBLANES:, :]
                up_ref[:, lo:lo + cw] = u[tm - SUBLANES:, :]
            else:
                buf[:, 0:SUBLANES, :] = prev_ref[:, :, lo:lo + cw]
                up_ref[:, lo:lo + cw] = u
            buf[:, SUBLANES:SUBLANES + ts, :] = u.reshape(nseq, ts, cw)
            um1 = buf[:, SUBLANES - 1:SUBLANES - 1 + ts, :].reshape(tm, cw)
            um2 = buf[:, SUBLANES - 2:SUBLANES - 2 + ts, :].reshape(tm, cw)
            k0 = 4 * half
            y = cwr[k0 + 3:k0 + 4, :] + cwr[k0:k0 + 1, :] * um2
            y = y + cwr[k0 + 1:k0 + 2, :] * um1
            ys.append(y + cwr[k0 + 2:k0 + 3, :] * u)
        act = (_gelu(ys[0]) * ys[1]).astype(BF16)
        acc = acc + jnp.dot(act, wd_ref[c], preferred_element_type=F32)

    z = alpha * x + (1.0 + g_ref[0]) * acc
    o_ref[...] = _ln(z, lg_ref[...], lb_ref[...])


def _conv_ffn(x, sc, sh, gate, prev8, w_up, w_conv, b_conv, w_down, ln_g, ln_b, alpha, tm, seq_len):
    rows, d = x.shape
    dff = w_down.shape[0]
    nch = dff // _FF_CW
    nt = rows // tm
    if seq_len >= tm:
        nseq, tiles_per_seq = 1, seq_len // tm
        prev_spec = pl.BlockSpec((1, SUBLANES, 2 * dff), lambda i: (i // tiles_per_seq, 0, 0))
        up_rows, up_spec = nt * SUBLANES, pl.BlockSpec((SUBLANES, 2 * dff), lambda i: (i, 0))
    else:
        nseq, tiles_per_seq = tm // seq_len, 1
        prev_spec = pl.BlockSpec((nseq, SUBLANES, 2 * dff), lambda i: (i, 0, 0))
        up_rows, up_spec = rows, pl.BlockSpec((tm, 2 * dff), lambda i: (i, 0))
    ts = tm // nseq
    chunked = lambda w: w.reshape(w.shape[0], nch, _FF_CW).transpose(1, 0, 2)
    wg = chunked(w_up[:, :dff]).astype(BF16)
    wv = chunked(w_up[:, dff:]).astype(BF16)
    wd = w_down.reshape(nch, _FF_CW, d).astype(BF16)
    conv_rows = jnp.concatenate([w_conv[:, :dff], b_conv[None, :dff], w_conv[:, dff:], b_conv[None, dff:]], axis=0)
    cwr = chunked(conv_rows)
    lg, lb = ln_g.reshape(1, d), ln_b.reshape(1, d)
    const = _const_spec
    mod_spec = lambda m: _mod_spec(m, tm, nt)
    return pl.pallas_call(
        functools.partial(_ffn_kernel, alpha=alpha, nseq=nseq, tiles_per_seq=tiles_per_seq),
        grid=(nt,),
        in_specs=[pl.BlockSpec((tm, d), lambda i: (i, 0)), mod_spec(sc), mod_spec(sh), mod_spec(gate), prev_spec,
                  const(wg), const(wv), const(cwr), const(wd), const(lg), const(lb)],
        out_specs=(pl.BlockSpec((tm, d), lambda i: (i, 0)), up_spec),
        out_shape=(jax.ShapeDtypeStruct((rows, d), F32), jax.ShapeDtypeStruct((up_rows, 2 * dff), F32)),
        scratch_shapes=[pltpu.VMEM((nseq, ts + SUBLANES, _FF_CW), F32),
                        pltpu.VMEM((nseq, ts + SUBLANES, _FF_CW), F32),
                        pltpu.VMEM((SUBLANES, 2 * dff), F32)],
        compiler_params=_cparams(1),
        name="conv_ffn",
    )(x, sc, sh, gate, prev8, wg, wv, cwr, wd, lg, lb)


def _gmlp_kernel(x_ref, sc_ref, sh_ref, g_ref, wv_ref, wu_ref, lvg_ref, lvb_ref, wmix_ref, bs_ref, wo_ref,
                 lg_ref, lb_ref, o_ref, *maybe_v_ref, alpha):
    ng = wmix_ref.shape[0]
    gw = wu_ref.shape[1] // ng
    x = x_ref[...]
    h = (x * (1.0 + sc_ref[0]) + sh_ref[0]).astype(BF16)
    v = _ln(_gelu(jnp.dot(h, wv_ref[...], preferred_element_type=F32)), lvg_ref[...], lvb_ref[...])
    if maybe_v_ref:
        maybe_v_ref[0][...] = v
    vb = v.astype(BF16)
    bs = bs_ref[...]
    u = _gelu(jnp.dot(h, wu_ref[...], preferred_element_type=F32))
    mixed = jnp.concatenate(
        [jnp.dot(wmix_ref[g], vb[:, g * gw:(g + 1) * gw], preferred_element_type=F32) + bs[:, g:g + 1]
         for g in range(ng)], axis=1)
    acc = jnp.dot((u * mixed).astype(BF16), wo_ref[...], preferred_element_type=F32)
    z = alpha * x + (1.0 + g_ref[0]) * acc
    o_ref[...] = _ln(z, lg_ref[...], lb_ref[...])


def _chunk_gmlp(x, sc, sh, gate, w_in, lnv_g, lnv_b, w_spatial, b_spatial, w_out, ln_g, ln_b,
                alpha, tm, chunk_len, emit_v):
    rows, d = x.shape
    dcg = w_out.shape[0]
    ng = w_spatial.shape[0]
    gw = dcg // ng
    nt = rows // tm
    wu = w_in[:, :dcg].astype(BF16)
    wv = w_in[:, dcg:].astype(BF16)
    wo = w_out.astype(BF16)
    tri = (jnp.arange(chunk_len)[:, None] >= jnp.arange(chunk_len)[None, :]).astype(w_spatial.dtype)
    wc = w_spatial[:, :chunk_len, :chunk_len] * tri[None]
    eye = jnp.eye(tm // chunk_len, dtype=w_spatial.dtype)
    wmix = jnp.einsum("ab,gts->gatbs", eye, wc).reshape(ng, tm, tm).astype(BF16)
    bs = jnp.tile(b_spatial[:, :chunk_len].T, (tm // chunk_len, 1))
    bs = jnp.pad(bs, ((0, 0), (0, LANES - ng)))
    lvg, lvb = lnv_g.reshape(1, dcg), lnv_b.reshape(1, dcg)
    lg, lb = ln_g.reshape(1, d), ln_b.reshape(1, d)
    const = _const_spec
    mod_spec = lambda m: _mod_spec(m, tm, nt)
    row_spec = lambda w: pl.BlockSpec((tm, w), lambda i: (i, 0))
    out_specs, out_shape = [row_spec(d)], [jax.ShapeDtypeStruct((rows, d), F32)]
    if emit_v:
        out_specs.append(row_spec(dcg))
        out_shape.append(jax.ShapeDtypeStruct((rows, dcg), F32))
    return pl.pallas_call(
        functools.partial(_gmlp_kernel, alpha=alpha),
        grid=(nt,),
        in_specs=[row_spec(d), mod_spec(sc), mod_spec(sh), mod_spec(gate), const(wv), const(wu), const(lvg),
                  const(lvb), const(wmix), const(bs), const(wo), const(lg), const(lb)],
        out_specs=tuple(out_specs),
        out_shape=tuple(out_shape),
        compiler_params=_cparams(1),
        name="chunk_gmlp",
    )(x, sc, sh, gate, wv, wu, lvg, lvb, wmix, bs, wo, lg, lb)


def kernel(x_prompt, x_sample, cache_a_k, cache_a_v, cache_a_kidx, cache_b_k, cache_b_v, cache_b_logf,
           state_ffn_conv, page_table, c_prompt, c_sample, w_mod, b_mod, ln1_g, ln1_b, ln2_g, ln2_b,
           w_in_ab, b_forget, w_out_ab, w_in_c, lnv_g, lnv_b, w_spatial, b_spatial, w_out_c,
           w_up, w_conv, b_conv, w_down):
    bsz, t, d = x_prompt.shape
    db, tn, _ = x_sample.shape
    depth = w_mod.shape[0]
    dff2 = w_up.shape[2]
    npg = page_table.shape[1]
    past = npg * PAGE
    alpha = (2 * depth) ** 0.25
    rows_p, rows_s = bsz * t, db * tn
    tm_p = 256
    tm_ffn = 512
    tq = 512

    nc = bsz + db
    c_all = jnp.concatenate([c_prompt, c_sample, jnp.zeros((-nc % SUBLANES, d), F32)], axis=0)
    mod = _adaln_mod(c_all, w_mod, b_mod)

    xp = x_prompt.reshape(rows_p, d)
    xs = x_sample.reshape(rows_s, d)
    cos_p, sin_p = _rope_tables(jnp.arange(t, dtype=jnp.int32))
    cos_s, sin_s = _rope_tables(jnp.tile(past + jnp.arange(tn, dtype=jnp.int32), db))

    outs_p = {k: [] for k in ("ak", "av", "aki", "bk", "bv", "blf", "conv")}
    outs_s = {k: [] for k in ("ak", "av", "aki", "bk", "bv", "blf", "conv", "cv")}

    for i in range(depth):
        j = i // 2
        mp = mod[i, :bsz].reshape(bsz, 6, 1, d)
        ms = jnp.repeat(mod[i, bsz:nc].reshape(db, 6, d), tn, axis=0).reshape(1, rows_s, 6, d)
        sh1p, sc1p, g1p, sh2p, sc2p, g2p = (mp[:, k] for k in range(6))
        sh1s, sc1s, g1s, sh2s, sc2s, g2s = (ms[:, :, k] for k in range(6))

        if i % 2 == 0:
            w_packed = _pack_w_ab(w_in_ab[j])
            bf_row = jnp.zeros((1, LANES), F32).at[0, _TAIL_F:_TAIL_F + N_HEADS].set(b_forget[j])
            w_out = w_out_ab[j]

            (qa_t, kaf_t, kab, vaf_t, va_t, qi_t, qb_t, kbf_t, kbb, vbf_t, vb_t, tail, kid) = _proj_ab(
                xp, sc1p, sh1p, cos_p, sin_p, w_packed, bf_row, tq, transposed_batch=bsz)
            r3 = lambda a: a.reshape(bsz, t, a.shape[-1])
            tail3 = r3(tail)
            ckb = _logf_cumsum(tail3, min(t, 512))
            b_out = _unchunk_t(_fox_prompt(qb_t, r3(kbb), vb_t, ckb, 2))
            w_t = tail3[:, :, _TAIL_WI:_TAIL_WI + N_IDX_HEADS].transpose(0, 2, 1)
            a_out = _unchunk_t(_dsa_prompt(qi_t, w_t, r3(kid), qa_t, r3(kab), va_t, min(TOPK, t // 4)))
            xp = _out_ln(a_out.reshape(rows_p, _GW), b_out.reshape(rows_p, _GW), xp, g1p, w_out,
                         ln1_g[i], ln1_b[i], alpha, tm_p)
            heads = lambda a: a.reshape(bsz, N_HEADS, HEAD_DIM, t).transpose(0, 3, 1, 2)
            outs_p["ak"].append(heads(kaf_t)); outs_p["av"].append(heads(vaf_t))
            outs_p["aki"].append(tail3[:, :, :D_IDX])
            outs_p["bk"].append(heads(kbf_t)); outs_p["bv"].append(heads(vbf_t))
            outs_p["blf"].append(tail3[:, :, _TAIL_F:_TAIL_F + N_HEADS])

            (qa, kaf, kab, vaf, vab, qi, qb, kbf, kbb, vbf, vbb, tail, kid) = _proj_ab(
                xs, sc1s, sh1s, cos_s, sin_s, w_packed, bf_row, rows_s)
            s3 = lambda a: a.reshape(db, tn, a.shape[-1])
            tail3 = s3(tail)
            hq = lambda a: a.reshape(db, tn, N_HEADS, HEAD_DIM).transpose(0, 2, 1, 3)
            qi_rows = hq(qi).reshape(db, N_IDX_HEADS * tn, D_IDX)
            w_hq = tail3[:, :, _TAIL_WI:_TAIL_WI + N_IDX_HEADS].transpose(0, 2, 1).reshape(db, N_IDX_HEADS * tn, 1)
            w_rows = jnp.broadcast_to(w_hq, (db, N_IDX_HEADS * tn, LANES))
            keys_t = lambda a: jnp.pad(a.transpose(0, 2, 1), ((0, 0), (0, 0), (0, PAGE - tn)))
            ki_new = keys_t(s3(kid)[:, :, :D_IDX])
            bias = _sample_indexer(page_table, qi_rows, w_rows, ki_new, cache_a_kidx[j].transpose(0, 2, 1),
                                   pages_per_step=min(npg, 16), ksel=min(TOPK, (past + tn) // 4))
            eye = jnp.eye(N_HEADS, dtype=BF16)
            bd = lambda a: jnp.einsum("bhqd,hg->bhqgd", hq(a), eye).reshape(db, N_HEADS * tn, _GW)
            lft_pool = cache_b_logf[j].transpose(0, 2, 1)
            lfn = jnp.pad(tail3[:, :, _TAIL_F:_TAIL_F + N_HEADS].transpose(0, 2, 1),
                          ((0, 0), (0, 0), (0, PAGE - tn)))
            pool = lambda cch: cch[j].transpose(0, 2, 3, 1).reshape(cch.shape[1], _GW, PAGE)
            a_out, b_out = _sample_attn(page_table, bd(qa), bd(qb), bias, lft_pool, lfn,
                                        keys_t(s3(kab)), keys_t(s3(vab)), keys_t(s3(kbb)), keys_t(s3(vbb)),
                                        pool(cache_a_k), pool(cache_a_v), pool(cache_b_k), pool(cache_b_v),
                                        pps=min(npg, 8))
            xs = _out_ln(a_out.reshape(rows_s, _GW), b_out.reshape(rows_s, _GW), xs, g1s, w_out,
                         ln1_g[i], ln1_b[i], alpha, rows_s)
            heads = lambda a: a.reshape(db, tn, N_HEADS, HEAD_DIM)
            outs_s["ak"].append(heads(kaf)); outs_s["av"].append(heads(vaf))
            outs_s["aki"].append(tail3[:, :, :D_IDX])
            outs_s["bk"].append(heads(kbf)); outs_s["bv"].append(heads(vbf))
            outs_s["blf"].append(tail3[:, :, _TAIL_F:_TAIL_F + N_HEADS])
        else:
            args = (w_in_c[j], lnv_g[j], lnv_b[j], w_spatial[j], b_spatial[j], w_out_c[j], ln1_g[i], ln1_b[i], alpha)
            (xp,) = _chunk_gmlp(xp, sc1p, sh1p, g1p, *args, tm_p, CHUNK, False)
            xs, cv = _chunk_gmlp(xs, sc1s, sh1s, g1s, *args, rows_s, tn, True)
            outs_s["cv"].append(cv.reshape(db, tn, -1))

        ffn_w = (w_up[i], w_conv[i], b_conv[i], w_down[i], ln2_g[i], ln2_b[i], alpha)
        xp, up_p = _conv_ffn(xp, sc2p, sh2p, g2p, jnp.zeros((bsz, SUBLANES, dff2), F32), *ffn_w, tm_ffn, t)
        outs_p["conv"].append(up_p.reshape(bsz, t // tm_ffn, SUBLANES, dff2)[:, -1, SUBLANES - (CONV_W - 1):])
        prev_s = jnp.pad(state_ffn_conv[i], ((0, 0), (SUBLANES - (CONV_W - 1), 0), (0, 0)))
        xs, up_s = _conv_ffn(xs, sc2s, sh2s, g2s, prev_s, *ffn_w, min(rows_s, 8 * tn), tn)
        outs_s["conv"].append(up_s.reshape(db, tn, dff2)[:, tn - (CONV_W - 1):])

    st = jnp.stack
    return (xp.reshape(bsz, t, d), xs.reshape(db, tn, d),
            st(outs_p["ak"]), st(outs_p["av"]), st(outs_p["aki"]), st(outs_p["bk"]), st(outs_p["bv"]),
            st(outs_p["blf"]), st(outs_p["conv"]),
            st(outs_s["ak"]), st(outs_s["av"]), st(outs_s["aki"]), st(outs_s["bk"]), st(outs_s["bv"]),
            st(outs_s["blf"]), st(outs_s["conv"]), st(outs_s["cv"]))
```

```python
import functools

import jax
import jax.numpy as jnp
from jax import lax
from jax.experimental import pallas as pl
from jax.experimental.pallas import tpu as pltpu

F32 = jnp.float32
BF16 = jnp.bfloat16

HEAD_DIM = 64
N_HEADS = 8
N_IDX_HEADS = 8
D_IDX = 64
TOPK = 256
PAGE = 128
CHUNK = 128
N_GROUPS_C = 8
CONV_W = 3
LN_EPS = 1e-5
ROPE_THETA = 10000.0
NEG = -1e30
LOG2E = 1.4426950408889634

LANES = 128
SUBLANES = 8
VMEM_LIMIT = 48 * 1024 * 1024

_NT = (((1,), (1,)), ((), ()))


def _cparams(n_axes):
    return pltpu.CompilerParams(dimension_semantics=("arbitrary",) * n_axes,
                                vmem_limit_bytes=VMEM_LIMIT)


def _const_spec(arr):
    return pl.BlockSpec(arr.shape, lambda i: (0,) * arr.ndim, pipeline_mode=pl.Buffered(1))


def _mod_spec(m, tm, nt):
    d = m.shape[2]
    if m.shape[1] == 1:
        tiles_per_group = nt // m.shape[0]
        return pl.BlockSpec((1, 1, d), lambda i: (i // tiles_per_group, 0, 0))
    return pl.BlockSpec((1, tm, d), lambda i: (0, i, 0))


def _ln(z, g, b):
    mu = jnp.mean(z, axis=-1, keepdims=True)
    zc = z - mu
    var = jnp.mean(zc * zc, axis=-1, keepdims=True)
    return zc * lax.rsqrt(var + LN_EPS) * g + b


def _gelu(x):
    return 0.5 * x * (1.0 + lax.erf(x * (2.0 ** -0.5)))


def _split3(x):
    hi = x.astype(BF16)
    r = x - hi.astype(F32)
    mid = r.astype(BF16)
    lo = (r - mid.astype(F32)).astype(BF16)
    return hi, mid, lo


def _dot3(x, m, left):
    parts = _split3(x)
    out = None
    for p in parts:
        d = (jnp.dot(m, p, preferred_element_type=F32) if left
             else jnp.dot(p, m, preferred_element_type=F32))
        out = d if out is None else out + d
    return out


def _mod_kernel(c_ref, w_ref, b_ref, o_ref):
    c = c_ref[...]
    a = (c * jax.nn.sigmoid(c)).astype(BF16)
    o_ref[0] = jnp.dot(a, w_ref[0].astype(BF16), preferred_element_type=F32) + b_ref[0]


def _adaln_mod(c_all, w_mod, b_mod):
    depth, d, n6 = w_mod.shape
    rows = c_all.shape[0]
    tn = n6 // 4
    return pl.pallas_call(
        _mod_kernel,
        grid=(depth, n6 // tn),
        in_specs=[pl.BlockSpec((rows, d), lambda l, j: (0, 0)),
                  pl.BlockSpec((1, d, tn), lambda l, j: (l, 0, j)),
                  pl.BlockSpec((1, 1, tn), lambda l, j: (l, 0, j))],
        out_specs=pl.BlockSpec((1, rows, tn), lambda l, j: (l, 0, j)),
        out_shape=jax.ShapeDtypeStruct((depth, rows, n6), F32),
        compiler_params=_cparams(2),
        name="adaln_mod",
    )(c_all, w_mod, b_mod.reshape(depth, 1, n6))


_G_QA, _G_KA, _G_VA, _G_QI, _G_QB, _G_KB, _G_VB = range(7)
_GW = N_HEADS * HEAD_DIM
_TAIL_OFF = 7 * _GW
_TAIL_WI = D_IDX
_TAIL_F = D_IDX + N_IDX_HEADS


def _pack_w_ab(w):
    sizes = (_GW, _GW, _GW, N_IDX_HEADS * D_IDX, D_IDX, N_IDX_HEADS, _GW, _GW, _GW, N_HEADS)
    offs, acc = [], 0
    for s in sizes[:-1]:
        acc += s
        offs.append(acc)
    qa, ka, va, qi, ki, wi, qb, kb, vb, fb = jnp.split(w, offs, axis=1)
    pad = jnp.zeros((w.shape[0], LANES - D_IDX - N_IDX_HEADS - N_HEADS), w.dtype)
    return jnp.concatenate([qa, ka, va, qi, qb, kb, vb, ki, wi, fb, pad], axis=1).astype(BF16)


def _rope_tables(pos):
    half = HEAD_DIM // 2
    inv = ROPE_THETA ** (-jnp.arange(half, dtype=F32) / half)
    ang = pos.astype(F32)[:, None] * inv[None, :]
    cos, sin = jnp.cos(ang), jnp.sin(ang)
    return (jnp.concatenate([cos] * 4, axis=1),
            jnp.concatenate([-sin, sin, -sin, sin], axis=1))


def _proj_ab_kernel(x_ref, sc_ref, sh_ref, cos_ref, sin_ref, w_ref, bf_ref,
                    qa_ref, kaf_ref, kab_ref, vaf_ref, vab_ref, qi_ref,
                    qb_ref, kbf_ref, kbb_ref, vbf_ref, vbb_ref, tail_ref, kid_ref, *, transposed):
    tm = x_ref.shape[0]

    def rows(val, ref):
        ref[...] = val.astype(ref.dtype)

    def emit(val, *refs):
        if not transposed:
            for ref in refs:
                rows(val, ref)
            return
        val_t = val.T
        for ref in refs:
            if len(ref.shape) == 5:
                ref[0, :, 0] = val_t.reshape(ref.shape[1], LANES, tm).astype(ref.dtype)
            else:
                ref[0] = val_t.astype(ref.dtype)

    h = (x_ref[...] * (1.0 + sc_ref[0]) + sh_ref[0]).astype(BF16)
    c1, s1 = cos_ref[...], sin_ref[...]
    c4 = jnp.concatenate([c1] * 4, axis=1)
    s4 = jnp.concatenate([s1] * 4, axis=1)
    lane4 = lax.broadcasted_iota(jnp.int32, (tm, _GW), 1)
    first4 = (lane4 & (HEAD_DIM - 1)) < HEAD_DIM // 2
    q_scale = HEAD_DIM ** -0.5 * (LOG2E if transposed else 1.0)

    def proj(g):
        return jnp.dot(h, w_ref[:, g * _GW:(g + 1) * _GW], preferred_element_type=F32)

    def rope(p):
        rot = jnp.where(first4, pltpu.roll(p, _GW - HEAD_DIM // 2, 1), pltpu.roll(p, HEAD_DIM // 2, 1))
        return p * c4 + rot * s4

    emit(rope(proj(_G_QA)) * q_scale, qa_ref)
    ka = rope(proj(_G_KA))
    emit(ka, kaf_ref)
    rows(ka, kab_ref)
    emit(proj(_G_VA), vaf_ref, vab_ref)
    emit(rope(proj(_G_QI)) * (D_IDX ** -0.5), qi_ref)
    emit(proj(_G_QB) * q_scale, qb_ref)
    kb = proj(_G_KB)
    emit(kb, kbf_ref)
    rows(kb, kbb_ref)
    emit(proj(_G_VB), vbf_ref, vbb_ref)

    t = jnp.dot(h, w_ref[:, _TAIL_OFF:_TAIL_OFF + LANES], preferred_element_type=F32)
    lane = lax.broadcasted_iota(jnp.int32, (tm, LANES), 1)
    first = (lane & (HEAD_DIM - 1)) < HEAD_DIM // 2
    rot = jnp.where(first, pltpu.roll(t, LANES - HEAD_DIM // 2, 1), pltpu.roll(t, HEAD_DIM // 2, 1))
    roped = t * c1 + rot * s1
    z = t + bf_ref[...]
    logf = jnp.minimum(z, 0.0) - jnp.log1p(jnp.exp(-jnp.abs(z)))
    tail = jnp.where(lane < D_IDX, roped,
                     jnp.where(lane < _TAIL_F, t * (N_IDX_HEADS ** -0.5),
                               jnp.where(lane < _TAIL_F + N_HEADS, logf, 0.0)))
    tail_ref[...] = tail
    kid_ref[...] = jnp.where(lane < D_IDX, roped, pltpu.roll(roped, D_IDX, 1)).astype(BF16)


def _proj_ab(x, sc, sh, cos_t, sin_t, w_packed, bf_row, tm, transposed_batch=None):
    rows, d = x.shape
    nt = rows // tm
    t_tiles = cos_t.shape[0] // tm
    mod_spec = _mod_spec(sc, tm, nt)
    rope_spec = pl.BlockSpec((tm, LANES), lambda i: (i % t_tiles, 0))
    narrow = lambda: pl.BlockSpec((tm, LANES), lambda i: (i, 0))
    row_spec = pl.BlockSpec((tm, _GW), lambda i: (i, 0))
    row_f, row_b = jax.ShapeDtypeStruct((rows, _GW), F32), jax.ShapeDtypeStruct((rows, _GW), BF16)
    if transposed_batch is None:
        spec_f = spec_c = row_spec
        shape_f, shape_c = row_f, row_b
    else:
        bsz = transposed_batch
        tpb = nt // bsz
        spec_f = pl.BlockSpec((1, _GW, tm), lambda i: (i // tpb, 0, i % tpb))
        spec_c = pl.BlockSpec((1, _PAIRS, 1, LANES, tm), lambda i: (i // tpb, 0, i % tpb, 0, 0))
        shape_f = jax.ShapeDtypeStruct((bsz, _GW, tpb * tm), F32)
        shape_c = jax.ShapeDtypeStruct((bsz, _PAIRS, tpb, LANES, tm), BF16)
    specs = [spec_c, spec_f, row_spec, spec_f, spec_c, spec_c, spec_c, spec_f, row_spec, spec_f, spec_c]
    shapes = [shape_c, shape_f, row_b, shape_f, shape_c, shape_c, shape_c, shape_f, row_b, shape_f, shape_c]
    out_shape = tuple(shapes) + (jax.ShapeDtypeStruct((rows, LANES), F32), jax.ShapeDtypeStruct((rows, LANES), BF16))
    return pl.pallas_call(
        functools.partial(_proj_ab_kernel, transposed=transposed_batch is not None),
        grid=(nt,),
        in_specs=[pl.BlockSpec((tm, d), lambda i: (i, 0)), mod_spec, mod_spec, rope_spec, rope_spec,
                  _const_spec(w_packed), _const_spec(bf_row)],
        out_specs=tuple(specs + [narrow(), narrow()]),
        out_shape=out_shape,
        compiler_params=_cparams(1),
        name="proj_ab",
    )(x, sc, sh, cos_t, sin_t, w_packed, bf_row)


def _cumsum_kernel(t_ref, o_ref, carry):
    @pl.when(pl.program_id(1) == 0)
    def _():
        carry[...] = jnp.zeros(carry.shape, F32)

    r = lax.broadcasted_iota(jnp.int32, (LANES, LANES), 0)
    c = lax.broadcasted_iota(jnp.int32, (LANES, LANES), 1)
    tri = jnp.where(c <= r, 1.0, 0.0).astype(BF16)
    run = carry[...]
    for k in range(t_ref.shape[1] // LANES):
        x = t_ref[0, k * LANES:(k + 1) * LANES, :]
        cum = _dot3(x, tri, left=True) + run
        run = cum[LANES - 1:LANES, :]
        for h in range(N_HEADS):
            o_ref[0, h, k * LANES:(k + 1) * LANES, :] = jnp.broadcast_to(
                cum[:, _TAIL_F + h:_TAIL_F + h + 1], (LANES, LANES))
    carry[...] = run


def _logf_cumsum(tail, tc):
    bsz, t, _ = tail.shape
    return pl.pallas_call(
        _cumsum_kernel,
        grid=(bsz, t // tc),
        in_specs=[pl.BlockSpec((1, tc, LANES), lambda b, i: (b, i, 0))],
        out_specs=pl.BlockSpec((1, N_HEADS, tc, LANES), lambda b, i: (b, 0, i, 0)),
        out_shape=jax.ShapeDtypeStruct((bsz, N_HEADS, t, LANES), F32),
        scratch_shapes=[pltpu.VMEM((1, LANES), F32)],
        compiler_params=_cparams(2),
        name="logf_cumsum",
    )(tail)


_PAIRS = N_HEADS // 2


def _chunk_t(a, width):
    bsz, t, _ = a.shape
    return a.reshape(bsz, t // width, width, _PAIRS, LANES).transpose(0, 3, 1, 4, 2)


def _unchunk_t(a):
    bsz, _, n, _, width = a.shape
    return a.transpose(0, 2, 4, 1, 3).reshape(bsz, n * width, _GW)


def _head_halves(x_t, top):
    zero = jnp.zeros_like(x_t)
    return jnp.where(top, x_t, zero), jnp.where(top, zero, x_t)


def _pair_step_t(q0, q1, kc, v_t, b0, b1, carry, top):
    m0, l0, m1, l1, acc = carry
    s0 = jnp.dot(kc, q0, preferred_element_type=F32) + b0
    s1 = jnp.dot(kc, q1, preferred_element_type=F32) + b1
    n0 = jnp.maximum(m0, jnp.max(s0, axis=0, keepdims=True))
    n1 = jnp.maximum(m1, jnp.max(s1, axis=0, keepdims=True))
    a0, a1 = jnp.exp2(m0 - n0), jnp.exp2(m1 - n1)
    p0, p1 = jnp.exp2(s0 - n0), jnp.exp2(s1 - n1)
    l0 = a0 * l0 + jnp.sum(p0, axis=0, keepdims=True)
    l1 = a1 * l1 + jnp.sum(p1, axis=0, keepdims=True)
    pv0 = jnp.dot(v_t, p0.astype(BF16), preferred_element_type=F32)
    pv1 = jnp.dot(v_t, p1.astype(BF16), preferred_element_type=F32)
    acc = jnp.where(top, a0 * acc + pv0, a1 * acc + pv1)
    return n0, l0, n1, l1, acc


def _pair_init_t(tq):
    row = lambda v: jnp.full((1, tq), v, F32)
    return row(-jnp.inf), row(0.0), row(-jnp.inf), row(0.0), jnp.zeros((LANES, tq), F32)


def _pair_finish_t(carry, top):
    _, l0, _, l1, acc = carry
    return acc / jnp.where(top, l0, l1)


def _causal_loop(it, tq, kc, step, init):
    per_tile = tq // kc
    carry = lax.fori_loop(0, it * per_tile, lambda j, c: step(j, c, None), init)
    for d in range(per_tile):
        carry = step(it * per_tile + d, carry, d * kc)
    return carry


def _fox_kernel(q_ref, k_ref, v_ref, ck_ref, o_ref):
    npair, tq, kc = q_ref.shape[1], q_ref.shape[-1], v_ref.shape[-1]
    it = pl.program_id(2)
    top = lax.broadcasted_iota(jnp.int32, (LANES, tq), 0) < HEAD_DIM
    qs = [_head_halves(q_ref[0, p, 0], top) for p in range(npair)]
    key = lax.broadcasted_iota(jnp.int32, (kc, tq), 0)
    qry = lax.broadcasted_iota(jnp.int32, (kc, tq), 1)
    wide = lambda x: jnp.concatenate([x] * (tq // LANES), axis=1)

    def step(j, carries, diag_off):
        off = pl.multiple_of(j * kc, kc)
        causal = None if diag_off is None else key + diag_off <= qry
        out = []
        for p in range(npair):
            b0 = wide(ck_ref[0, 2 * p, pl.ds(off, kc), :] * -LOG2E)
            b1 = wide(ck_ref[0, 2 * p + 1, pl.ds(off, kc), :] * -LOG2E)
            if causal is not None:
                b0, b1 = jnp.where(causal, b0, NEG), jnp.where(causal, b1, NEG)
            out.append(_pair_step_t(qs[p][0], qs[p][1], k_ref[0, pl.ds(off, kc), p * LANES:(p + 1) * LANES],
                                    v_ref[0, p, j], b0, b1, carries[p], top))
        return tuple(out)

    carries = _causal_loop(it, tq, kc, step, tuple(_pair_init_t(tq) for _ in range(npair)))
    for p in range(npair):
        o_ref[0, p, 0] = _pair_finish_t(carries[p], top).astype(o_ref.dtype)


def _fox_prompt(q_t, kb, v_t, ckb, pairs_per_step):
    bsz, _, nq, _, tq = q_t.shape
    nk, kc = v_t.shape[2], v_t.shape[4]
    t = kb.shape[1]
    pp = pairs_per_step
    return pl.pallas_call(
        _fox_kernel,
        grid=(bsz, _PAIRS // pp, nq),
        in_specs=[pl.BlockSpec((1, pp, 1, LANES, tq), lambda b, g, i: (b, g, i, 0, 0)),
                  pl.BlockSpec((1, t, pp * LANES), lambda b, g, i: (b, 0, g)),
                  pl.BlockSpec((1, pp, nk, LANES, kc), lambda b, g, i: (b, g, 0, 0, 0)),
                  pl.BlockSpec((1, 2 * pp, t, LANES), lambda b, g, i: (b, g, 0, 0))],
        out_specs=pl.BlockSpec((1, pp, 1, LANES, tq), lambda b, g, i: (b, g, i, 0, 0)),
        out_shape=jax.ShapeDtypeStruct(q_t.shape, BF16),
        compiler_params=_cparams(3),
        name="fox_prompt",
    )(q_t, kb, v_t, ckb)


def _topk_to_bias(sc_ref, nch, nvalid, ksel, keys_axis, n_bisect=16):
    kc = sc_ref.shape[1 + keys_axis]
    kf = float(ksel)
    ninf = -jnp.inf
    stat = lambda v: jnp.full(nvalid.shape, v, F32)

    if keys_axis == 1:
        def over_keys(elem, red, combine, init):
            return red(red(elem(sc_ref[...]), axis=0), axis=1, keepdims=True)
    else:
        def over_keys(elem, red, combine, init):
            return lax.fori_loop(
                0, nch, lambda j, a: combine(a, red(elem(sc_ref[j]), axis=0, keepdims=True)), stat(init))

    def count(t, strict):
        hit = (lambda x: jnp.where(x > t, 1.0, 0.0)) if strict else (lambda x: jnp.where(x >= t, 1.0, 0.0))
        return over_keys(hit, jnp.sum, jnp.add, 0.0)

    def count_ge(t):
        return count(t, False)

    def max_below(h):
        return over_keys(lambda x: jnp.where(x < h, x, ninf), jnp.max, jnp.maximum, ninf)

    def min_valid():
        return over_keys(lambda x: jnp.where(x == ninf, jnp.inf, x), jnp.min, jnp.minimum, jnp.inf)

    small = nvalid <= kf
    lo = min_valid()
    v = max_below(stat(jnp.inf))
    ok = count_ge(v) >= kf
    res = jnp.where(jnp.logical_or(small, ok), 1.0, 0.0)
    thr = jnp.where(small, ninf, v)
    hi = jnp.where(res > 0.0, jnp.inf, v)

    def bisect(_, st):
        lo, hi = st
        mid = 0.5 * (lo + hi)
        ge = count_ge(mid) >= kf
        return jnp.where(ge, mid, lo), jnp.where(ge, hi, mid)

    lo, hi = lax.fori_loop(0, n_bisect, bisect, (lo, hi))

    def snap_cond(st):
        return jnp.min(st[2]) < 1.0

    def snap(st):
        thr, hi, res = st
        v = max_below(hi)
        ok = count_ge(v) >= kf
        open_ = res < 1.0
        new = jnp.logical_and(open_, ok)
        drop = jnp.logical_and(open_, jnp.logical_not(ok))
        return jnp.where(new, v, thr), jnp.where(drop, v, hi), jnp.where(ok, 1.0, res)

    thr, hi, res = lax.while_loop(snap_cond, snap, (thr, hi, res))

    need = kf - count(thr, True)
    r = lax.broadcasted_iota(jnp.int32, (kc, kc), 0)
    c = lax.broadcasted_iota(jnp.int32, (kc, kc), 1)
    prefix = jnp.where((r < c) if keys_axis == 1 else (c < r), 1.0, 0.0).astype(BF16)

    def ties(s):
        tie = jnp.logical_and(s == thr, s > ninf)
        return tie, jnp.where(tie, 1.0, 0.0)

    def write_bias(j, s, tie, before):
        sel = jnp.logical_or(s > thr, jnp.logical_and(tie, before < need))
        sc_ref[j] = jnp.where(sel, 0.0, NEG)

    if keys_axis == 1:
        s = sc_ref[...]
        tie, tief = ties(s)
        rows = s.shape[1]
        within = jnp.dot(tief.reshape(nch * rows, kc).astype(BF16), prefix,
                         preferred_element_type=F32).reshape(nch, rows, kc)
        totals = jnp.sum(tief, axis=2, keepdims=True)
        run = stat(0.0)
        for j in range(nch):
            write_bias(j, s[j], tie[j], within[j] + run)
            run = run + totals[j]
    else:
        def to_bias(j, run):
            s = sc_ref[j]
            tie, tief = ties(s)
            before = jnp.dot(prefix, tief.astype(BF16), preferred_element_type=F32) + run
            write_bias(j, s, tie, before)
            return run + jnp.sum(tief, axis=0, keepdims=True)

        lax.fori_loop(0, nch, to_bias, stat(0.0))


def _dsa_kernel(qi_ref, w_ref, kid_ref, qa_ref, ka_ref, va_ref, o_ref, sc_ref, *, ksel):
    tq, kc = qi_ref.shape[-1], va_ref.shape[-1]
    it = pl.program_id(1)
    top = lax.broadcasted_iota(jnp.int32, (LANES, tq), 0) < HEAD_DIM
    key = lax.broadcasted_iota(jnp.int32, (kc, tq), 0)
    qry = lax.broadcasted_iota(jnp.int32, (kc, tq), 1)

    qm = []
    for p in range(N_IDX_HEADS // 2):
        qm.extend(_head_halves(qi_ref[0, p, 0], top))
    w = w_ref[0]

    def score_chunk(j, carry, diag_off):
        off = pl.multiple_of(j * kc, kc)
        keys = kid_ref[0, pl.ds(off, kc), :]
        sc = jnp.zeros((kc, tq), F32)
        for h in range(N_IDX_HEADS):
            lg = jnp.dot(keys, qm[h], preferred_element_type=F32)
            sc = sc + jnp.maximum(lg, 0.0) * w[h:h + 1, :]
        if diag_off is not None:
            sc = jnp.where(key + diag_off <= qry, sc, -jnp.inf)
        sc_ref[j] = sc
        return carry

    _causal_loop(it, tq, kc, score_chunk, 0)
    nch = (it + 1) * (tq // kc)
    nvalid = (it * tq + 1 + lax.broadcasted_iota(jnp.int32, (1, tq), 1)).astype(F32)
    _topk_to_bias(sc_ref, nch, nvalid, ksel, keys_axis=0)

    qs = [_head_halves(qa_ref[0, p, 0], top) for p in range(_PAIRS)]

    def step(j, carries):
        off = pl.multiple_of(j * kc, kc)
        bias = sc_ref[j]
        return tuple(
            _pair_step_t(qs[p][0], qs[p][1], ka_ref[0, pl.ds(off, kc), p * LANES:(p + 1) * LANES], va_ref[0, p, j],
                         bias, bias, carries[p], top)
            for p in range(_PAIRS))

    carries = lax.fori_loop(0, nch, step, tuple(_pair_init_t(tq) for _ in range(_PAIRS)))
    for p in range(_PAIRS):
        o_ref[0, p, 0] = _pair_finish_t(carries[p], top).astype(o_ref.dtype)


def _dsa_prompt(qi_t, w_t, kid, qa_t, ka, va_t, ksel):
    bsz, _, nq, _, tq = qa_t.shape
    nk, kc = va_t.shape[2], va_t.shape[4]
    t = ka.shape[1]
    q_spec = pl.BlockSpec((1, _PAIRS, 1, LANES, tq), lambda b, i: (b, 0, i, 0, 0))
    return pl.pallas_call(
        functools.partial(_dsa_kernel, ksel=ksel),
        grid=(bsz, nq),
        in_specs=[q_spec,
                  pl.BlockSpec((1, N_IDX_HEADS, tq), lambda b, i: (b, 0, i)),
                  pl.BlockSpec((1, t, LANES), lambda b, i: (b, 0, 0)),
                  q_spec,
                  pl.BlockSpec((1, t, _GW), lambda b, i: (b, 0, 0)),
                  pl.BlockSpec((1, _PAIRS, nk, LANES, kc), lambda b, i: (b, 0, 0, 0, 0))],
        out_specs=q_spec,
        out_shape=jax.ShapeDtypeStruct(qa_t.shape, BF16),
        scratch_shapes=[pltpu.VMEM((nk, kc, tq), F32)],
        compiler_params=_cparams(2),
        name="dsa_prompt",
    )(qi_t, w_t, kid, qa_t, ka, va_t)


def _sample_idx_kernel(pt_ref, qi_ref, w_ref, knew_ref, *rest, pages_per_step):
    page_refs = rest[:pages_per_step]
    o_ref = rest[pages_per_step]
    s = pl.program_id(1)
    nsteps = pl.num_programs(1)
    npg, tn = o_ref.shape[1] - 1, o_ref.shape[2]
    qi = qi_ref[0]
    w = w_ref[0]

    def scores(keys_t):
        lg = jnp.dot(qi, keys_t, preferred_element_type=F32)
        z = jnp.maximum(lg, 0.0) * w
        out = z[0:tn]
        for h in range(1, N_IDX_HEADS):
            out = out + z[h * tn:(h + 1) * tn]
        return out

    for i, pr in enumerate(page_refs):
        o_ref[0, s * pages_per_step + i] = scores(pr[0].astype(BF16))

    @pl.when(s == nsteps - 1)
    def _():
        r = lax.broadcasted_iota(jnp.int32, (tn, PAGE), 0)
        c = lax.broadcasted_iota(jnp.int32, (tn, PAGE), 1)
        o_ref[0, npg] = jnp.where(c <= r, scores(knew_ref[0]), -jnp.inf)


def _sample_indexer(page_table, qi_rows, w_rows, ki_new_pad, kidx_pool, pages_per_step):
    db, npg = page_table.shape
    tn = qi_rows.shape[1] // N_IDX_HEADS
    nsteps = npg // pages_per_step

    def page_spec(i):
        return pl.BlockSpec((1, D_IDX, PAGE), lambda b, s, pt: (pt[b, s * pages_per_step + i], 0, 0))

    grid_spec = pltpu.PrefetchScalarGridSpec(
        num_scalar_prefetch=1,
        grid=(db, nsteps),
        in_specs=[pl.BlockSpec((1,) + qi_rows.shape[1:], lambda b, s, pt: (b, 0, 0)),
                  pl.BlockSpec((1,) + w_rows.shape[1:], lambda b, s, pt: (b, 0, 0)),
                  pl.BlockSpec((1, D_IDX, PAGE), lambda b, s, pt: (b, 0, 0))]
                 + [page_spec(i) for i in range(pages_per_step)],
        out_specs=pl.BlockSpec((1, npg + 1, tn, PAGE), lambda b, s, pt: (b, 0, 0, 0)),
    )
    return pl.pallas_call(
        functools.partial(_sample_idx_kernel, pages_per_step=pages_per_step),
        grid_spec=grid_spec,
        out_shape=jax.ShapeDtypeStruct((db, npg + 1, tn, PAGE), F32),
        compiler_params=_cparams(2),
        name="sample_indexer",
    )(page_table, qi_rows, w_rows, ki_new_pad, *([kidx_pool] * pages_per_step))


def _sample_topk_kernel(s_ref, o_ref, *, ksel, tn, past):
    rows = s_ref.shape[1]
    o_ref[...] = s_ref[...]
    q = lax.broadcasted_iota(jnp.int32, (rows, 1), 0) & (tn - 1)
    _topk_to_bias(o_ref, o_ref.shape[0], (past + 1 + q).astype(F32), ksel, keys_axis=1)


def _sample_topk(scores, ksel, row_block):
    db, nch, tn, _ = scores.shape
    rows = db * tn
    rb = min(rows, row_block)
    by_chunk = scores.transpose(1, 0, 2, 3).reshape(nch, rows, PAGE)
    bias = pl.pallas_call(
        functools.partial(_sample_topk_kernel, ksel=ksel, tn=tn, past=(nch - 1) * PAGE),
        grid=(rows // rb,),
        in_specs=[pl.BlockSpec((nch, rb, PAGE), lambda i: (0, i, 0))],
        out_specs=pl.BlockSpec((nch, rb, PAGE), lambda i: (0, i, 0)),
        out_shape=jax.ShapeDtypeStruct((nch, rows, PAGE), F32),
        compiler_params=_cparams(1),
        name="sample_topk",
    )(by_chunk)
    return bias.reshape(nch, db, tn, PAGE).transpose(1, 0, 2, 3)


def _sample_attn_kernel(pt_ref, qa_ref, qb_ref, bias_ref, biasn_ref, lfn_ref,
                        kan_ref, van_ref, kbn_ref, vbn_ref, *rest, pps):
    lft_refs, ka_refs, va_refs, kb_refs, vb_refs = (rest[i * pps:(i + 1) * pps] for i in range(5))
    oa_ref, ob_ref, ma, la, acca, mb, lb, accb, run = rest[5 * pps:]
    p = pl.program_id(1)
    nsteps = pl.num_programs(1)
    rows = qa_ref.shape[1]
    tn = rows // N_HEADS

    @pl.when(p == 0)
    def _():
        for m in (ma, mb):
            m[...] = jnp.full(m.shape, -jnp.inf, F32)
        for z in (la, lb, acca, accb, run):
            z[...] = jnp.zeros(z.shape, F32)

    r = lax.broadcasted_iota(jnp.int32, (PAGE, PAGE), 0)
    c = lax.broadcasted_iota(jnp.int32, (PAGE, PAGE), 1)
    incl = jnp.where(r <= c, 1.0, 0.0).astype(BF16)

    def update(q, kts, vts, biases, m_ref, l_ref, acc_ref):
        s = [jnp.dot(q, kt, preferred_element_type=F32) + b for kt, b in zip(kts, biases)]
        smax = functools.reduce(jnp.maximum, s)
        m_old = m_ref[...]
        m_new = jnp.maximum(m_old, jnp.max(smax, axis=1, keepdims=True))
        a = jnp.exp(m_old - m_new)
        ps = [jnp.exp(x - m_new) for x in s]
        l_ref[...] = a * l_ref[...] + jnp.sum(functools.reduce(jnp.add, ps), axis=1, keepdims=True)
        pv = [lax.dot_general(pr.astype(BF16), vt, _NT, preferred_element_type=F32) for pr, vt in zip(ps, vts)]
        acc_ref[...] = a * acc_ref[...] + functools.reduce(jnp.add, pv)
        m_ref[...] = m_new

    def head_rows(x):
        return jnp.concatenate([jnp.broadcast_to(x[h:h + 1], (tn, x.shape[1])) for h in range(N_HEADS)], axis=0)

    def tile_rows(x):
        return jnp.concatenate([x] * N_HEADS, axis=0)

    def forget_biases(lf_ts):
        out, run_v = [], run[...]
        for lf_t in lf_ts:
            cum = _dot3(head_rows(lf_t), incl, left=False) + run_v
            run_v = cum[:, PAGE - 1:PAGE]
            out.append(-cum)
        run[...] = run_v
        return out

    cast = lambda refs: [x[0].astype(BF16) for x in refs]
    update(qa_ref[0], cast(ka_refs), cast(va_refs), [tile_rows(bias_ref[0, i]) for i in range(pps)], ma, la, acca)
    update(qb_ref[0], cast(kb_refs), cast(vb_refs), forget_biases([x[0] for x in lft_refs]), mb, lb, accb)

    @pl.when(p == nsteps - 1)
    def _():
        update(qa_ref[0], [kan_ref[0]], [van_ref[0]], [tile_rows(biasn_ref[0, 0])], ma, la, acca)
        qrow = lax.broadcasted_iota(jnp.int32, (rows, PAGE), 0) & (tn - 1)
        key = lax.broadcasted_iota(jnp.int32, (rows, PAGE), 1)
        fb = jnp.where(key <= qrow, forget_biases([lfn_ref[0]])[0], NEG)
        update(qb_ref[0], [kbn_ref[0]], [vbn_ref[0]], [fb], mb, lb, accb)
        lane = lax.broadcasted_iota(jnp.int32, (tn, _GW), 1)
        for l_ref, acc_ref, o_ref in ((la, acca, oa_ref), (lb, accb, ob_ref)):
            full = acc_ref[...] / l_ref[...]
            out = jnp.zeros((tn, _GW), F32)
            for h in range(N_HEADS):
                mine = jnp.logical_and(lane >= h * HEAD_DIM, lane < (h + 1) * HEAD_DIM)
                out = jnp.where(mine, full[h * tn:(h + 1) * tn], out)
            o_ref[0] = out.astype(o_ref.dtype)


def _sample_attn(page_table, qa_bd, qb_bd, bias, lft_pool, lfn, kan, van, kbn, vbn,
                 ka_pool, va_pool, kb_pool, vb_pool, pps):
    db, npg = page_table.shape
    rows = qa_bd.shape[1]
    tn = rows // N_HEADS
    per_b = lambda shp: pl.BlockSpec((1,) + shp, lambda b, p, pt: (b,) + (0,) * len(shp))

    def paged(shp):
        return [pl.BlockSpec((1,) + shp, lambda b, p, pt, i=i: (pt[b, p * pps + i],) + (0,) * len(shp))
                for i in range(pps)]

    grid_spec = pltpu.PrefetchScalarGridSpec(
        num_scalar_prefetch=1,
        grid=(db, npg // pps),
        in_specs=[per_b((rows, _GW)), per_b((rows, _GW)),
                  pl.BlockSpec((1, pps, tn, PAGE), lambda b, p, pt: (b, p, 0, 0)),
                  pl.BlockSpec((1, 1, tn, PAGE), lambda b, p, pt: (b, npg, 0, 0)),
                  per_b((N_HEADS, PAGE)),
                  per_b((_GW, PAGE)), per_b((_GW, PAGE)), per_b((_GW, PAGE)), per_b((_GW, PAGE))]
                 + paged((N_HEADS, PAGE)) + paged((_GW, PAGE)) + paged((_GW, PAGE))
                 + paged((_GW, PAGE)) + paged((_GW, PAGE)),
        out_specs=(per_b((tn, _GW)), per_b((tn, _GW))),
        scratch_shapes=[pltpu.VMEM((rows, 1), F32), pltpu.VMEM((rows, 1), F32), pltpu.VMEM((rows, _GW), F32),
                        pltpu.VMEM((rows, 1), F32), pltpu.VMEM((rows, 1), F32), pltpu.VMEM((rows, _GW), F32),
                        pltpu.VMEM((rows, 1), F32)],
    )
    o = jax.ShapeDtypeStruct((db, tn, _GW), BF16)
    rep = lambda a: [a] * pps
    return pl.pallas_call(
        functools.partial(_sample_attn_kernel, pps=pps),
        grid_spec=grid_spec,
        out_shape=(o, o),
        compiler_params=_cparams(2),
        name="sample_attn",
    )(page_table, qa_bd, qb_bd, bias, bias, lfn, kan, van, kbn, vbn,
      *rep(lft_pool), *rep(ka_pool), *rep(va_pool), *rep(kb_pool), *rep(vb_pool))


def _out_ln_kernel(a_ref, b_ref, x_ref, g_ref, wa_ref, wb_ref, lg_ref, lb_ref, o_ref, *, alpha):
    mix = (jnp.dot(a_ref[...], wa_ref[...], preferred_element_type=F32)
           + jnp.dot(b_ref[...], wb_ref[...], preferred_element_type=F32))
    z = alpha * x_ref[...] + (1.0 + g_ref[0]) * mix
    o_ref[...] = _ln(z, lg_ref[...], lb_ref[...])


def _out_ln(a, b, x, gate, w_out, ln_g, ln_b, alpha, tm):
    rows, d = x.shape
    nt = rows // tm
    ka = a.shape[1]
    wa, wb = w_out[:ka].astype(BF16), w_out[ka:].astype(BF16)
    const = _const_spec
    lg, lb = ln_g.reshape(1, d), ln_b.reshape(1, d)
    return pl.pallas_call(
        functools.partial(_out_ln_kernel, alpha=alpha),
        grid=(nt,),
        in_specs=[pl.BlockSpec((tm, ka), lambda i: (i, 0)), pl.BlockSpec((tm, b.shape[1]), lambda i: (i, 0)),
                  pl.BlockSpec((tm, d), lambda i: (i, 0)),
                  _mod_spec(gate, tm, nt),
                  const(wa), const(wb), const(lg), const(lb)],
        out_specs=pl.BlockSpec((tm, d), lambda i: (i, 0)),
        out_shape=jax.ShapeDtypeStruct((rows, d), F32),
        compiler_params=_cparams(1),
        name="out_ln",
    )(a, b, x, gate, wa, wb, lg, lb)


_FF_CW = 2816


def _ffn_kernel(x_ref, sc_ref, sh_ref, g_ref, prev_ref, wg_ref, wv_ref, cw_ref, wd_ref, lg_ref, lb_ref,
                o_ref, up_ref, bufg, bufv, carry, *, alpha, nseq, tiles_per_seq):
    tm, d = x_ref.shape
    ts = tm // nseq
    nch, _, cw = wg_ref.shape
    dff = nch * cw
    x = x_ref[...]
    h = (x * (1.0 + sc_ref[0]) + sh_ref[0]).astype(BF16)
    chained = tiles_per_seq > 1

    if chained:
        @pl.when(pl.program_id(0) % tiles_per_seq == 0)
        def _():
            carry[...] = prev_ref[0]

    acc = jnp.zeros((tm, d), F32)
    for c in range(nch):
        cwr = cw_ref[c]
        ys = []
        for half, (w_ref, buf) in enumerate(((wg_ref, bufg), (wv_ref, bufv))):
            lo = half * dff + c * cw
            u = jnp.dot(h, w_ref[c], preferred_element_type=F32)
            if chained:
                buf[0, 0:SUBLANES, :] = carry[:, lo:lo + cw]
                carry[:, lo:lo + cw] = u[tm - SUBLANES:, :]
                up_ref[:, lo:lo + cw] = u[tm - SUBLANES:, :]
            else:
                buf[:, 0:SUBLANES, :] = prev_ref[:, :, lo:lo + cw]
                up_ref[:, lo:lo + cw] = u
            buf[:, SUBLANES:SUBLANES + ts, :] = u.reshape(nseq, ts, cw)
            um1 = buf[:, SUBLANES - 1:SUBLANES - 1 + ts, :].reshape(tm, cw)
            um2 = buf[:, SUBLANES - 2:SUBLANES - 2 + ts, :].reshape(tm, cw)
            k0 = 4 * half
            y = cwr[k0 + 3:k0 + 4, :] + cwr[k0:k0 + 1, :] * um2
            y = y + cwr[k0 + 1:k0 + 2, :] * um1
            ys.append(y + cwr[k0 + 2:k0 + 3, :] * u)
        act = (_gelu(ys[0]) * ys[1]).astype(BF16)
        acc = acc + jnp.dot(act, wd_ref[c], preferred_element_type=F32)

    z = alpha * x + (1.0 + g_ref[0]) * acc
    o_ref[...] = _ln(z, lg_ref[...], lb_ref[...])


def _conv_ffn(x, sc, sh, gate, prev8, w_up, w_conv, b_conv, w_down, ln_g, ln_b, alpha, tm, seq_len):
    rows, d = x.shape
    dff = w_down.shape[0]
    nch = dff // _FF_CW
    nt = rows // tm
    if seq_len >= tm:
        nseq, tiles_per_seq = 1, seq_len // tm
        prev_spec = pl.BlockSpec((1, SUBLANES, 2 * dff), lambda i: (i // tiles_per_seq, 0, 0))
        up_rows, up_spec = nt * SUBLANES, pl.BlockSpec((SUBLANES, 2 * dff), lambda i: (i, 0))
    else:
        nseq, tiles_per_seq = tm // seq_len, 1
        prev_spec = pl.BlockSpec((nseq, SUBLANES, 2 * dff), lambda i: (i, 0, 0))
        up_rows, up_spec = rows, pl.BlockSpec((tm, 2 * dff), lambda i: (i, 0))
    ts = tm // nseq
    chunked = lambda w: w.reshape(w.shape[0], nch, _FF_CW).transpose(1, 0, 2)
    wg = chunked(w_up[:, :dff]).astype(BF16)
    wv = chunked(w_up[:, dff:]).astype(BF16)
    wd = w_down.reshape(nch, _FF_CW, d).astype(BF16)
    conv_rows = jnp.concatenate([w_conv[:, :dff], b_conv[None, :dff], w_conv[:, dff:], b_conv[None, dff:]], axis=0)
    cwr = chunked(conv_rows)
    lg, lb = ln_g.reshape(1, d), ln_b.reshape(1, d)
    const = _const_spec
    mod_spec = lambda m: _mod_spec(m, tm, nt)
    return pl.pallas_call(
        functools.partial(_ffn_kernel, alpha=alpha, nseq=nseq, tiles_per_seq=tiles_per_seq),
        grid=(nt,),
        in_specs=[pl.BlockSpec((tm, d), lambda i: (i, 0)), mod_spec(sc), mod_spec(sh), mod_spec(gate), prev_spec,
                  const(wg), const(wv), const(cwr), const(wd), const(lg), const(lb)],
        out_specs=(pl.BlockSpec((tm, d), lambda i: (i, 0)), up_spec),
        out_shape=(jax.ShapeDtypeStruct((rows, d), F32), jax.ShapeDtypeStruct((up_rows, 2 * dff), F32)),
        scratch_shapes=[pltpu.VMEM((nseq, ts + SUBLANES, _FF_CW), F32),
                        pltpu.VMEM((nseq, ts + SUBLANES, _FF_CW), F32),
                        pltpu.VMEM((SUBLANES, 2 * dff), F32)],
        compiler_params=_cparams(1),
        name="conv_ffn",
    )(x, sc, sh, gate, prev8, wg, wv, cwr, wd, lg, lb)


def _gmlp_kernel(x_ref, sc_ref, sh_ref, g_ref, wv_ref, wu_ref, lvg_ref, lvb_ref, wmix_ref, bs_ref, wo_ref,
                 lg_ref, lb_ref, o_ref, *maybe_v_ref, alpha):
    ng = wmix_ref.shape[0]
    gw = wu_ref.shape[1] // ng
    x = x_ref[...]
    h = (x * (1.0 + sc_ref[0]) + sh_ref[0]).astype(BF16)
    v = _ln(_gelu(jnp.dot(h, wv_ref[...], preferred_element_type=F32)), lvg_ref[...], lvb_ref[...])
    if maybe_v_ref:
        maybe_v_ref[0][...] = v
    vb = v.astype(BF16)
    bs = bs_ref[...]
    u = _gelu(jnp.dot(h, wu_ref[...], preferred_element_type=F32))
    mixed = jnp.concatenate(
        [jnp.dot(wmix_ref[g], vb[:, g * gw:(g + 1) * gw], preferred_element_type=F32) + bs[:, g:g + 1]
         for g in range(ng)], axis=1)
    acc = jnp.dot((u * mixed).astype(BF16), wo_ref[...], preferred_element_type=F32)
    z = alpha * x + (1.0 + g_ref[0]) * acc
    o_ref[...] = _ln(z, lg_ref[...], lb_ref[...])


def _chunk_gmlp(x, sc, sh, gate, w_in, lnv_g, lnv_b, w_spatial, b_spatial, w_out, ln_g, ln_b,
                alpha, tm, chunk_len, emit_v):
    rows, d = x.shape
    dcg = w_out.shape[0]
    ng = w_spatial.shape[0]
    gw = dcg // ng
    nt = rows // tm
    wu = w_in[:, :dcg].astype(BF16)
    wv = w_in[:, dcg:].astype(BF16)
    wo = w_out.astype(BF16)
    tri = (jnp.arange(chunk_len)[:, None] >= jnp.arange(chunk_len)[None, :]).astype(w_spatial.dtype)
    wc = w_spatial[:, :chunk_len, :chunk_len] * tri[None]
    eye = jnp.eye(tm // chunk_len, dtype=w_spatial.dtype)
    wmix = jnp.einsum("ab,gts->gatbs", eye, wc).reshape(ng, tm, tm).astype(BF16)
    bs = jnp.tile(b_spatial[:, :chunk_len].T, (tm // chunk_len, 1))
    bs = jnp.pad(bs, ((0, 0), (0, LANES - ng)))
    lvg, lvb = lnv_g.reshape(1, dcg), lnv_b.reshape(1, dcg)
    lg, lb = ln_g.reshape(1, d), ln_b.reshape(1, d)
    const = _const_spec
    mod_spec = lambda m: _mod_spec(m, tm, nt)
    row_spec = lambda w: pl.BlockSpec((tm, w), lambda i: (i, 0))
    out_specs, out_shape = [row_spec(d)], [jax.ShapeDtypeStruct((rows, d), F32)]
    if emit_v:
        out_specs.append(row_spec(dcg))
        out_shape.append(jax.ShapeDtypeStruct((rows, dcg), F32))
    return pl.pallas_call(
        functools.partial(_gmlp_kernel, alpha=alpha),
        grid=(nt,),
        in_specs=[row_spec(d), mod_spec(sc), mod_spec(sh), mod_spec(gate), const(wv), const(wu), const(lvg),
                  const(lvb), const(wmix), const(bs), const(wo), const(lg), const(lb)],
        out_specs=tuple(out_specs),
        out_shape=tuple(out_shape),
        compiler_params=_cparams(1),
        name="chunk_gmlp",
    )(x, sc, sh, gate, wv, wu, lvg, lvb, wmix, bs, wo, lg, lb)


def kernel(x_prompt, x_sample, cache_a_k, cache_a_v, cache_a_kidx, cache_b_k, cache_b_v, cache_b_logf,
           state_ffn_conv, page_table, c_prompt, c_sample, w_mod, b_mod, ln1_g, ln1_b, ln2_g, ln2_b,
           w_in_ab, b_forget, w_out_ab, w_in_c, lnv_g, lnv_b, w_spatial, b_spatial, w_out_c,
           w_up, w_conv, b_conv, w_down):
    bsz, t, d = x_prompt.shape
    db, tn, _ = x_sample.shape
    depth = w_mod.shape[0]
    dff2 = w_up.shape[2]
    npg = page_table.shape[1]
    past = npg * PAGE
    alpha = (2 * depth) ** 0.25
    rows_p, rows_s = bsz * t, db * tn
    tm_p = 256
    tm_ffn = 512
    tq = 512

    nc = bsz + db
    c_all = jnp.concatenate([c_prompt, c_sample, jnp.zeros((-nc % SUBLANES, d), F32)], axis=0)
    mod = _adaln_mod(c_all, w_mod, b_mod)

    xp = x_prompt.reshape(rows_p, d)
    xs = x_sample.reshape(rows_s, d)
    cos_p, sin_p = _rope_tables(jnp.arange(t, dtype=jnp.int32))
    cos_s, sin_s = _rope_tables(jnp.tile(past + jnp.arange(tn, dtype=jnp.int32), db))

    outs_p = {k: [] for k in ("ak", "av", "aki", "bk", "bv", "blf", "conv")}
    outs_s = {k: [] for k in ("ak", "av", "aki", "bk", "bv", "blf", "conv", "cv")}

    for i in range(depth):
        j = i // 2
        mp = mod[i, :bsz].reshape(bsz, 6, 1, d)
        ms = jnp.repeat(mod[i, bsz:nc].reshape(db, 6, d), tn, axis=0).reshape(1, rows_s, 6, d)
        sh1p, sc1p, g1p, sh2p, sc2p, g2p = (mp[:, k] for k in range(6))
        sh1s, sc1s, g1s, sh2s, sc2s, g2s = (ms[:, :, k] for k in range(6))

        if i % 2 == 0:
            w_packed = _pack_w_ab(w_in_ab[j])
            bf_row = jnp.zeros((1, LANES), F32).at[0, _TAIL_F:_TAIL_F + N_HEADS].set(b_forget[j])
            w_out = w_out_ab[j]

            (qa_t, kaf_t, kab, vaf_t, va_t, qi_t, qb_t, kbf_t, kbb, vbf_t, vb_t, tail, kid) = _proj_ab(
                xp, sc1p, sh1p, cos_p, sin_p, w_packed, bf_row, tq, transposed_batch=bsz)
            r3 = lambda a: a.reshape(bsz, t, a.shape[-1])
            tail3 = r3(tail)
            ckb = _logf_cumsum(tail3, min(t, 512))
            b_out = _unchunk_t(_fox_prompt(qb_t, r3(kbb), vb_t, ckb, 2))
            w_t = tail3[:, :, _TAIL_WI:_TAIL_WI + N_IDX_HEADS].transpose(0, 2, 1)
            a_out = _unchunk_t(_dsa_prompt(qi_t, w_t, r3(kid), qa_t, r3(kab), va_t, min(TOPK, t // 4)))
            xp = _out_ln(a_out.reshape(rows_p, _GW), b_out.reshape(rows_p, _GW), xp, g1p, w_out,
                         ln1_g[i], ln1_b[i], alpha, tm_p)
            heads = lambda a: a.reshape(bsz, N_HEADS, HEAD_DIM, t).transpose(0, 3, 1, 2)
            outs_p["ak"].append(heads(kaf_t)); outs_p["av"].append(heads(vaf_t))
            outs_p["aki"].append(tail3[:, :, :D_IDX])
            outs_p["bk"].append(heads(kbf_t)); outs_p["bv"].append(heads(vbf_t))
            outs_p["blf"].append(tail3[:, :, _TAIL_F:_TAIL_F + N_HEADS])

            (qa, kaf, kab, vaf, vab, qi, qb, kbf, kbb, vbf, vbb, tail, kid) = _proj_ab(
                xs, sc1s, sh1s, cos_s, sin_s, w_packed, bf_row, rows_s)
            s3 = lambda a: a.reshape(db, tn, a.shape[-1])
            tail3 = s3(tail)
            hq = lambda a: a.reshape(db, tn, N_HEADS, HEAD_DIM).transpose(0, 2, 1, 3)
            qi_rows = hq(qi).reshape(db, N_IDX_HEADS * tn, D_IDX)
            w_hq = tail3[:, :, _TAIL_WI:_TAIL_WI + N_IDX_HEADS].transpose(0, 2, 1).reshape(db, N_IDX_HEADS * tn, 1)
            w_rows = jnp.broadcast_to(w_hq, (db, N_IDX_HEADS * tn, LANES))
            keys_t = lambda a: jnp.pad(a.transpose(0, 2, 1), ((0, 0), (0, 0), (0, PAGE - tn)))
            ki_new = keys_t(s3(kid)[:, :, :D_IDX])
            scores = _sample_indexer(page_table, qi_rows, w_rows, ki_new, cache_a_kidx[j].transpose(0, 2, 1),
                                     pages_per_step=min(npg, 16))
            bias = _sample_topk(scores, min(TOPK, (past + tn) // 4), row_block=128)
            eye = jnp.eye(N_HEADS, dtype=BF16)
            bd = lambda a: jnp.einsum("bhqd,hg->bhqgd", hq(a), eye).reshape(db, N_HEADS * tn, _GW)
            lft_pool = cache_b_logf[j].transpose(0, 2, 1)
            lfn = jnp.pad(tail3[:, :, _TAIL_F:_TAIL_F + N_HEADS].transpose(0, 2, 1),
                          ((0, 0), (0, 0), (0, PAGE - tn)))
            pool = lambda cch: cch[j].transpose(0, 2, 3, 1).reshape(cch.shape[1], _GW, PAGE)
            a_out, b_out = _sample_attn(page_table, bd(qa), bd(qb), bias, lft_pool, lfn,
                                        keys_t(s3(kab)), keys_t(s3(vab)), keys_t(s3(kbb)), keys_t(s3(vbb)),
                                        pool(cache_a_k), pool(cache_a_v), pool(cache_b_k), pool(cache_b_v),
                                        pps=min(npg, 16))
            xs = _out_ln(a_out.reshape(rows_s, _GW), b_out.reshape(rows_s, _GW), xs, g1s, w_out,
                         ln1_g[i], ln1_b[i], alpha, rows_s)
            heads = lambda a: a.reshape(db, tn, N_HEADS, HEAD_DIM)
            outs_s["ak"].append(heads(kaf)); outs_s["av"].append(heads(vaf))
            outs_s["aki"].append(tail3[:, :, :D_IDX])
            outs_s["bk"].append(heads(kbf)); outs_s["bv"].append(heads(vbf))
            outs_s["blf"].append(tail3[:, :, _TAIL_F:_TAIL_F + N_HEADS])
        else:
            args = (w_in_c[j], lnv_g[j], lnv_b[j], w_spatial[j], b_spatial[j], w_out_c[j], ln1_g[i], ln1_b[i], alpha)
            (xp,) = _chunk_gmlp(xp, sc1p, sh1p, g1p, *args, tm_p, CHUNK, False)
            xs, cv = _chunk_gmlp(xs, sc1s, sh1s, g1s, *args, rows_s, tn, True)
            outs_s["cv"].append(cv.reshape(db, tn, -1))

        ffn_w = (w_up[i], w_conv[i], b_conv[i], w_down[i], ln2_g[i], ln2_b[i], alpha)
        xp, up_p = _conv_ffn(xp, sc2p, sh2p, g2p, jnp.zeros((bsz, SUBLANES, dff2), F32), *ffn_w, tm_ffn, t)
        outs_p["conv"].append(up_p.reshape(bsz, t // tm_ffn, SUBLANES, dff2)[:, -1, SUBLANES - (CONV_W - 1):])
        prev_s = jnp.pad(state_ffn_conv[i], ((0, 0), (SUBLANES - (CONV_W - 1), 0), (0, 0)))
        xs, up_s = _conv_ffn(xs, sc2s, sh2s, g2s, prev_s, *ffn_w, min(rows_s, 8 * tn), tn)
        outs_s["conv"].append(up_s.reshape(db, tn, dff2)[:, tn - (CONV_W - 1):])

    st = jnp.stack
    return (xp.reshape(bsz, t, d), xs.reshape(db, tn, d),
            st(outs_p["ak"]), st(outs_p["av"]), st(outs_p["aki"]), st(outs_p["bk"]), st(outs_p["bv"]),
            st(outs_p["blf"]), st(outs_p["conv"]),
            st(outs_s["ak"]), st(outs_s["av"]), st(outs_s["aki"]), st(outs_s["bk"]), st(outs_s["bv"]),
            st(outs_s["blf"]), st(outs_s["conv"]), st(outs_s["cv"]))
```

```python
import functools

import jax
import jax.numpy as jnp
from jax import lax
from jax.experimental import pallas as pl
from jax.experimental.pallas import tpu as pltpu

F32 = jnp.float32
BF16 = jnp.bfloat16

HEAD_DIM = 64
N_HEADS = 8
N_IDX_HEADS = 8
D_IDX = 64
TOPK = 256
PAGE = 128
CHUNK = 128
N_GROUPS_C = 8
CONV_W = 3
LN_EPS = 1e-5
ROPE_THETA = 10000.0
NEG = -1e30
LOG2E = 1.4426950408889634

LANES = 128
SUBLANES = 8
VMEM_LIMIT = 48 * 1024 * 1024

_NT = (((1,), (1,)), ((), ()))


def _cparams(n_axes):
    return pltpu.CompilerParams(dimension_semantics=("arbitrary",) * n_axes,
                                vmem_limit_bytes=VMEM_LIMIT)


def _const_spec(arr):
    return pl.BlockSpec(arr.shape, lambda i: (0,) * arr.ndim, pipeline_mode=pl.Buffered(1))


def _mod_spec(m, tm, nt):
    d = m.shape[2]
    if m.shape[1] == 1:
        tiles_per_group = nt // m.shape[0]
        return pl.BlockSpec((1, 1, d), lambda i: (i // tiles_per_group, 0, 0))
    return pl.BlockSpec((1, tm, d), lambda i: (0, i, 0))


def _ln(z, g, b):
    mu = jnp.mean(z, axis=-1, keepdims=True)
    zc = z - mu
    var = jnp.mean(zc * zc, axis=-1, keepdims=True)
    return zc * lax.rsqrt(var + LN_EPS) * g + b


def _gelu(x):
    return 0.5 * x * (1.0 + lax.erf(x * (2.0 ** -0.5)))


def _split3(x):
    hi = x.astype(BF16)
    r = x - hi.astype(F32)
    mid = r.astype(BF16)
    lo = (r - mid.astype(F32)).astype(BF16)
    return hi, mid, lo


def _dot3(x, m, left):
    parts = _split3(x)
    out = None
    for p in parts:
        d = (jnp.dot(m, p, preferred_element_type=F32) if left
             else jnp.dot(p, m, preferred_element_type=F32))
        out = d if out is None else out + d
    return out


def _mod_kernel(c_ref, w_ref, b_ref, o_ref):
    c = c_ref[...]
    a = (c * jax.nn.sigmoid(c)).astype(BF16)
    o_ref[0] = jnp.dot(a, w_ref[0].astype(BF16), preferred_element_type=F32) + b_ref[0]


def _adaln_mod(c_all, w_mod, b_mod):
    depth, d, n6 = w_mod.shape
    rows = c_all.shape[0]
    tn = n6 // 4
    return pl.pallas_call(
        _mod_kernel,
        grid=(depth, n6 // tn),
        in_specs=[pl.BlockSpec((rows, d), lambda l, j: (0, 0)),
                  pl.BlockSpec((1, d, tn), lambda l, j: (l, 0, j)),
                  pl.BlockSpec((1, 1, tn), lambda l, j: (l, 0, j))],
        out_specs=pl.BlockSpec((1, rows, tn), lambda l, j: (l, 0, j)),
        out_shape=jax.ShapeDtypeStruct((depth, rows, n6), F32),
        compiler_params=_cparams(2),
        name="adaln_mod",
    )(c_all, w_mod, b_mod.reshape(depth, 1, n6))


_G_QA, _G_KA, _G_VA, _G_QI, _G_QB, _G_KB, _G_VB = range(7)
_GW = N_HEADS * HEAD_DIM
_TAIL_OFF = 7 * _GW
_TAIL_WI = D_IDX
_TAIL_F = D_IDX + N_IDX_HEADS


def _pack_w_ab(w):
    sizes = (_GW, _GW, _GW, N_IDX_HEADS * D_IDX, D_IDX, N_IDX_HEADS, _GW, _GW, _GW, N_HEADS)
    offs, acc = [], 0
    for s in sizes[:-1]:
        acc += s
        offs.append(acc)
    qa, ka, va, qi, ki, wi, qb, kb, vb, fb = jnp.split(w, offs, axis=1)
    pad = jnp.zeros((w.shape[0], LANES - D_IDX - N_IDX_HEADS - N_HEADS), w.dtype)
    return jnp.concatenate([qa, ka, va, qi, qb, kb, vb, ki, wi, fb, pad], axis=1).astype(BF16)


def _rope_tables(pos):
    half = HEAD_DIM // 2
    inv = ROPE_THETA ** (-jnp.arange(half, dtype=F32) / half)
    ang = pos.astype(F32)[:, None] * inv[None, :]
    cos, sin = jnp.cos(ang), jnp.sin(ang)
    return (jnp.concatenate([cos] * 4, axis=1),
            jnp.concatenate([-sin, sin, -sin, sin], axis=1))


def _proj_ab_kernel(x_ref, sc_ref, sh_ref, cos_ref, sin_ref, w_ref, bf_ref,
                    qa_ref, kaf_ref, kab_ref, vaf_ref, vab_ref, qi_ref,
                    qb_ref, kbf_ref, kbb_ref, vbf_ref, vbb_ref, tail_ref, kid_ref, *, transposed):
    tm = x_ref.shape[0]

    def rows(val, ref):
        ref[...] = val.astype(ref.dtype)

    def emit(val, *refs):
        if not transposed:
            for ref in refs:
                rows(val, ref)
            return
        val_t = val.T
        for ref in refs:
            if len(ref.shape) == 5:
                ref[0, :, 0] = val_t.reshape(ref.shape[1], LANES, tm).astype(ref.dtype)
            else:
                ref[0] = val_t.astype(ref.dtype)

    h = (x_ref[...] * (1.0 + sc_ref[0]) + sh_ref[0]).astype(BF16)
    c1, s1 = cos_ref[...], sin_ref[...]
    c4 = jnp.concatenate([c1] * 4, axis=1)
    s4 = jnp.concatenate([s1] * 4, axis=1)
    lane4 = lax.broadcasted_iota(jnp.int32, (tm, _GW), 1)
    first4 = (lane4 & (HEAD_DIM - 1)) < HEAD_DIM // 2
    q_scale = HEAD_DIM ** -0.5 * (LOG2E if transposed else 1.0)

    def proj(g):
        return jnp.dot(h, w_ref[:, g * _GW:(g + 1) * _GW], preferred_element_type=F32)

    def rope(p):
        rot = jnp.where(first4, pltpu.roll(p, _GW - HEAD_DIM // 2, 1), pltpu.roll(p, HEAD_DIM // 2, 1))
        return p * c4 + rot * s4

    emit(rope(proj(_G_QA)) * q_scale, qa_ref)
    ka = rope(proj(_G_KA))
    emit(ka, kaf_ref)
    rows(ka, kab_ref)
    emit(proj(_G_VA), vaf_ref, vab_ref)
    emit(rope(proj(_G_QI)) * (D_IDX ** -0.5), qi_ref)
    emit(proj(_G_QB) * q_scale, qb_ref)
    kb = proj(_G_KB)
    emit(kb, kbf_ref)
    rows(kb, kbb_ref)
    emit(proj(_G_VB), vbf_ref, vbb_ref)

    t = jnp.dot(h, w_ref[:, _TAIL_OFF:_TAIL_OFF + LANES], preferred_element_type=F32)
    lane = lax.broadcasted_iota(jnp.int32, (tm, LANES), 1)
    first = (lane & (HEAD_DIM - 1)) < HEAD_DIM // 2
    rot = jnp.where(first, pltpu.roll(t, LANES - HEAD_DIM // 2, 1), pltpu.roll(t, HEAD_DIM // 2, 1))
    roped = t * c1 + rot * s1
    z = t + bf_ref[...]
    logf = jnp.minimum(z, 0.0) - jnp.log1p(jnp.exp(-jnp.abs(z)))
    tail = jnp.where(lane < D_IDX, roped,
                     jnp.where(lane < _TAIL_F, t * (N_IDX_HEADS ** -0.5),
                               jnp.where(lane < _TAIL_F + N_HEADS, logf, 0.0)))
    tail_ref[...] = tail
    kid_ref[...] = jnp.where(lane < D_IDX, roped, pltpu.roll(roped, D_IDX, 1)).astype(BF16)


def _proj_ab(x, sc, sh, cos_t, sin_t, w_packed, bf_row, tm, transposed_batch=None):
    rows, d = x.shape
    nt = rows // tm
    t_tiles = cos_t.shape[0] // tm
    mod_spec = _mod_spec(sc, tm, nt)
    rope_spec = pl.BlockSpec((tm, LANES), lambda i: (i % t_tiles, 0))
    narrow = lambda: pl.BlockSpec((tm, LANES), lambda i: (i, 0))
    row_spec = pl.BlockSpec((tm, _GW), lambda i: (i, 0))
    row_f, row_b = jax.ShapeDtypeStruct((rows, _GW), F32), jax.ShapeDtypeStruct((rows, _GW), BF16)
    if transposed_batch is None:
        spec_f = spec_c = row_spec
        shape_f, shape_c = row_f, row_b
    else:
        bsz = transposed_batch
        tpb = nt // bsz
        spec_f = pl.BlockSpec((1, _GW, tm), lambda i: (i // tpb, 0, i % tpb))
        spec_c = pl.BlockSpec((1, _PAIRS, 1, LANES, tm), lambda i: (i // tpb, 0, i % tpb, 0, 0))
        shape_f = jax.ShapeDtypeStruct((bsz, _GW, tpb * tm), F32)
        shape_c = jax.ShapeDtypeStruct((bsz, _PAIRS, tpb, LANES, tm), BF16)
    specs = [spec_c, spec_f, row_spec, spec_f, spec_c, spec_c, spec_c, spec_f, row_spec, spec_f, spec_c]
    shapes = [shape_c, shape_f, row_b, shape_f, shape_c, shape_c, shape_c, shape_f, row_b, shape_f, shape_c]
    out_shape = tuple(shapes) + (jax.ShapeDtypeStruct((rows, LANES), F32), jax.ShapeDtypeStruct((rows, LANES), BF16))
    return pl.pallas_call(
        functools.partial(_proj_ab_kernel, transposed=transposed_batch is not None),
        grid=(nt,),
        in_specs=[pl.BlockSpec((tm, d), lambda i: (i, 0)), mod_spec, mod_spec, rope_spec, rope_spec,
                  _const_spec(w_packed), _const_spec(bf_row)],
        out_specs=tuple(specs + [narrow(), narrow()]),
        out_shape=out_shape,
        compiler_params=_cparams(1),
        name="proj_ab",
    )(x, sc, sh, cos_t, sin_t, w_packed, bf_row)


def _cumsum_kernel(t_ref, o_ref, carry):
    @pl.when(pl.program_id(1) == 0)
    def _():
        carry[...] = jnp.zeros(carry.shape, F32)

    r = lax.broadcasted_iota(jnp.int32, (LANES, LANES), 0)
    c = lax.broadcasted_iota(jnp.int32, (LANES, LANES), 1)
    tri = jnp.where(c <= r, 1.0, 0.0).astype(BF16)
    run = carry[...]
    for k in range(t_ref.shape[1] // LANES):
        x = t_ref[0, k * LANES:(k + 1) * LANES, :]
        cum = _dot3(x, tri, left=True) + run
        run = cum[LANES - 1:LANES, :]
        for h in range(N_HEADS):
            o_ref[0, h, k * LANES:(k + 1) * LANES, :] = jnp.broadcast_to(
                cum[:, _TAIL_F + h:_TAIL_F + h + 1], (LANES, LANES))
    carry[...] = run


def _logf_cumsum(tail, tc):
    bsz, t, _ = tail.shape
    return pl.pallas_call(
        _cumsum_kernel,
        grid=(bsz, t // tc),
        in_specs=[pl.BlockSpec((1, tc, LANES), lambda b, i: (b, i, 0))],
        out_specs=pl.BlockSpec((1, N_HEADS, tc, LANES), lambda b, i: (b, 0, i, 0)),
        out_shape=jax.ShapeDtypeStruct((bsz, N_HEADS, t, LANES), F32),
        scratch_shapes=[pltpu.VMEM((1, LANES), F32)],
        compiler_params=_cparams(2),
        name="logf_cumsum",
    )(tail)


_PAIRS = N_HEADS // 2


def _chunk_t(a, width):
    bsz, t, _ = a.shape
    return a.reshape(bsz, t // width, width, _PAIRS, LANES).transpose(0, 3, 1, 4, 2)


def _unchunk_t(a):
    bsz, _, n, _, width = a.shape
    return a.transpose(0, 2, 4, 1, 3).reshape(bsz, n * width, _GW)


def _head_halves(x_t, top):
    zero = jnp.zeros_like(x_t)
    return jnp.where(top, x_t, zero), jnp.where(top, zero, x_t)


def _pair_step_t(q0, q1, kc, v_t, b0, b1, carry, top):
    m0, l0, m1, l1, acc = carry
    s0 = jnp.dot(kc, q0, preferred_element_type=F32) + b0
    s1 = jnp.dot(kc, q1, preferred_element_type=F32) + b1
    n0 = jnp.maximum(m0, jnp.max(s0, axis=0, keepdims=True))
    n1 = jnp.maximum(m1, jnp.max(s1, axis=0, keepdims=True))
    a0, a1 = jnp.exp2(m0 - n0), jnp.exp2(m1 - n1)
    p0, p1 = jnp.exp2(s0 - n0), jnp.exp2(s1 - n1)
    l0 = a0 * l0 + jnp.sum(p0, axis=0, keepdims=True)
    l1 = a1 * l1 + jnp.sum(p1, axis=0, keepdims=True)
    pv0 = jnp.dot(v_t, p0.astype(BF16), preferred_element_type=F32)
    pv1 = jnp.dot(v_t, p1.astype(BF16), preferred_element_type=F32)
    acc = jnp.where(top, a0 * acc + pv0, a1 * acc + pv1)
    return n0, l0, n1, l1, acc


def _pair_init_t(tq):
    row = lambda v: jnp.full((1, tq), v, F32)
    return row(-jnp.inf), row(0.0), row(-jnp.inf), row(0.0), jnp.zeros((LANES, tq), F32)


def _pair_finish_t(carry, top):
    _, l0, _, l1, acc = carry
    return acc / jnp.where(top, l0, l1)


def _causal_loop(it, tq, kc, step, init):
    per_tile = tq // kc
    carry = lax.fori_loop(0, it * per_tile, lambda j, c: step(j, c, None), init)
    for d in range(per_tile):
        carry = step(it * per_tile + d, carry, d * kc)
    return carry


def _fox_kernel(q_ref, k_ref, v_ref, ck_ref, o_ref):
    npair, tq, kc = q_ref.shape[1], q_ref.shape[-1], v_ref.shape[-1]
    it = pl.program_id(2)
    top = lax.broadcasted_iota(jnp.int32, (LANES, tq), 0) < HEAD_DIM
    qs = [_head_halves(q_ref[0, p, 0], top) for p in range(npair)]
    key = lax.broadcasted_iota(jnp.int32, (kc, tq), 0)
    qry = lax.broadcasted_iota(jnp.int32, (kc, tq), 1)
    wide = lambda x: jnp.concatenate([x] * (tq // LANES), axis=1)

    def step(j, carries, diag_off):
        off = pl.multiple_of(j * kc, kc)
        causal = None if diag_off is None else key + diag_off <= qry
        out = []
        for p in range(npair):
            b0 = wide(ck_ref[0, 2 * p, pl.ds(off, kc), :] * -LOG2E)
            b1 = wide(ck_ref[0, 2 * p + 1, pl.ds(off, kc), :] * -LOG2E)
            if causal is not None:
                b0, b1 = jnp.where(causal, b0, NEG), jnp.where(causal, b1, NEG)
            out.append(_pair_step_t(qs[p][0], qs[p][1], k_ref[0, pl.ds(off, kc), p * LANES:(p + 1) * LANES],
                                    v_ref[0, p, j], b0, b1, carries[p], top))
        return tuple(out)

    carries = _causal_loop(it, tq, kc, step, tuple(_pair_init_t(tq) for _ in range(npair)))
    for p in range(npair):
        o_ref[0, p, 0] = _pair_finish_t(carries[p], top).astype(o_ref.dtype)


def _fox_prompt(q_t, kb, v_t, ckb, pairs_per_step):
    bsz, _, nq, _, tq = q_t.shape
    nk, kc = v_t.shape[2], v_t.shape[4]
    t = kb.shape[1]
    pp = pairs_per_step
    return pl.pallas_call(
        _fox_kernel,
        grid=(bsz, _PAIRS // pp, nq),
        in_specs=[pl.BlockSpec((1, pp, 1, LANES, tq), lambda b, g, i: (b, g, i, 0, 0)),
                  pl.BlockSpec((1, t, pp * LANES), lambda b, g, i: (b, 0, g)),
                  pl.BlockSpec((1, pp, nk, LANES, kc), lambda b, g, i: (b, g, 0, 0, 0)),
                  pl.BlockSpec((1, 2 * pp, t, LANES), lambda b, g, i: (b, g, 0, 0))],
        out_specs=pl.BlockSpec((1, pp, 1, LANES, tq), lambda b, g, i: (b, g, i, 0, 0)),
        out_shape=jax.ShapeDtypeStruct(q_t.shape, BF16),
        compiler_params=_cparams(3),
        name="fox_prompt",
    )(q_t, kb, v_t, ckb)


def _topk_to_bias(sc_ref, nch, nvalid, ksel, keys_axis, bounds=None, n_bisect=16):
    kc = sc_ref.shape[1 + keys_axis]
    kf = float(ksel)
    ninf = -jnp.inf
    stat = lambda v: jnp.full(nvalid.shape, v, F32)

    if keys_axis == 1:
        def over_keys(elem, red, combine, init):
            return red(red(elem(sc_ref[...]), axis=0), axis=1, keepdims=True)
    else:
        def over_keys(elem, red, combine, init):
            return lax.fori_loop(
                0, nch, lambda j, a: combine(a, red(elem(sc_ref[j]), axis=0, keepdims=True)), stat(init))

    def count(t, strict):
        hit = (lambda x: jnp.where(x > t, 1.0, 0.0)) if strict else (lambda x: jnp.where(x >= t, 1.0, 0.0))
        return over_keys(hit, jnp.sum, jnp.add, 0.0)

    def count_ge(t):
        return count(t, False)

    def max_below(h):
        return over_keys(lambda x: jnp.where(x < h, x, ninf), jnp.max, jnp.maximum, ninf)

    def min_valid():
        return over_keys(lambda x: jnp.where(x == ninf, jnp.inf, x), jnp.min, jnp.minimum, jnp.inf)

    small = nvalid <= kf
    if bounds is None:
        lo = min_valid()
        v = max_below(stat(jnp.inf))
        ok = count_ge(v) >= kf
        res = jnp.where(jnp.logical_or(small, ok), 1.0, 0.0)
        thr = jnp.where(small, ninf, v)
        hi = jnp.where(res > 0.0, jnp.inf, v)
    else:
        lo, top = bounds
        res = jnp.where(small, 1.0, 0.0)
        thr = stat(ninf)
        hi = jnp.where(small, jnp.inf, top + (jnp.abs(top) * 1e-6 + 1e-30))

    def bisect(_, st):
        lo, hi = st
        mid = 0.5 * (lo + hi)
        ge = count_ge(mid) >= kf
        return jnp.where(ge, mid, lo), jnp.where(ge, hi, mid)

    lo, hi = lax.fori_loop(0, n_bisect, bisect, (lo, hi))

    def snap_cond(st):
        return jnp.min(st[2]) < 1.0

    def snap(st):
        thr, hi, res = st
        v = max_below(hi)
        ok = count_ge(v) >= kf
        open_ = res < 1.0
        new = jnp.logical_and(open_, ok)
        drop = jnp.logical_and(open_, jnp.logical_not(ok))
        return jnp.where(new, v, thr), jnp.where(drop, v, hi), jnp.where(ok, 1.0, res)

    thr, hi, res = lax.while_loop(snap_cond, snap, (thr, hi, res))

    need = kf - count(thr, True)
    r = lax.broadcasted_iota(jnp.int32, (kc, kc), 0)
    c = lax.broadcasted_iota(jnp.int32, (kc, kc), 1)
    prefix = jnp.where((r < c) if keys_axis == 1 else (c < r), 1.0, 0.0).astype(BF16)

    def ties(s):
        tie = jnp.logical_and(s == thr, s > ninf)
        return tie, jnp.where(tie, 1.0, 0.0)

    def write_bias(j, s, tie, before):
        sel = jnp.logical_or(s > thr, jnp.logical_and(tie, before < need))
        sc_ref[j] = jnp.where(sel, 0.0, NEG)

    if keys_axis == 1:
        s = sc_ref[...]
        tie, tief = ties(s)
        rows = s.shape[1]
        within = jnp.dot(tief.reshape(nch * rows, kc).astype(BF16), prefix,
                         preferred_element_type=F32).reshape(nch, rows, kc)
        totals = jnp.sum(tief, axis=2, keepdims=True)
        run = stat(0.0)
        for j in range(nch):
            write_bias(j, s[j], tie[j], within[j] + run)
            run = run + totals[j]
    else:
        def to_bias(j, run):
            s = sc_ref[j]
            tie, tief = ties(s)
            before = jnp.dot(prefix, tief.astype(BF16), preferred_element_type=F32) + run
            write_bias(j, s, tie, before)
            return run + jnp.sum(tief, axis=0, keepdims=True)

        lax.fori_loop(0, nch, to_bias, stat(0.0))


def _dsa_kernel(qi_ref, w_ref, kid_ref, qa_ref, ka_ref, va_ref, o_ref, sc_ref, *, ksel):
    tq, kc = qi_ref.shape[-1], va_ref.shape[-1]
    it = pl.program_id(1)
    top = lax.broadcasted_iota(jnp.int32, (LANES, tq), 0) < HEAD_DIM
    key = lax.broadcasted_iota(jnp.int32, (kc, tq), 0)
    qry = lax.broadcasted_iota(jnp.int32, (kc, tq), 1)

    qm = []
    for p in range(N_IDX_HEADS // 2):
        qm.extend(_head_halves(qi_ref[0, p, 0], top))
    w = w_ref[0]

    def score_chunk(j, carry, diag_off):
        off = pl.multiple_of(j * kc, kc)
        keys = kid_ref[0, pl.ds(off, kc), :]
        sc = jnp.zeros((kc, tq), F32)
        for h in range(N_IDX_HEADS):
            lg = jnp.dot(keys, qm[h], preferred_element_type=F32)
            sc = sc + jnp.maximum(lg, 0.0) * w[h:h + 1, :]
        lo = jnp.minimum(carry[0], jnp.min(sc, axis=0, keepdims=True))
        if diag_off is not None:
            sc = jnp.where(key + diag_off <= qry, sc, -jnp.inf)
        sc_ref[j] = sc
        return lo, jnp.maximum(carry[1], jnp.max(sc, axis=0, keepdims=True))

    bounds = _causal_loop(it, tq, kc, score_chunk,
                          (jnp.full((1, tq), jnp.inf, F32), jnp.full((1, tq), -jnp.inf, F32)))
    nch = (it + 1) * (tq // kc)
    nvalid = (it * tq + 1 + lax.broadcasted_iota(jnp.int32, (1, tq), 1)).astype(F32)
    _topk_to_bias(sc_ref, nch, nvalid, ksel, keys_axis=0, bounds=bounds)

    qs = [_head_halves(qa_ref[0, p, 0], top) for p in range(_PAIRS)]

    def step(j, carries):
        off = pl.multiple_of(j * kc, kc)
        bias = sc_ref[j]
        return tuple(
            _pair_step_t(qs[p][0], qs[p][1], ka_ref[0, pl.ds(off, kc), p * LANES:(p + 1) * LANES], va_ref[0, p, j],
                         bias, bias, carries[p], top)
            for p in range(_PAIRS))

    carries = lax.fori_loop(0, nch, step, tuple(_pair_init_t(tq) for _ in range(_PAIRS)))
    for p in range(_PAIRS):
        o_ref[0, p, 0] = _pair_finish_t(carries[p], top).astype(o_ref.dtype)


def _dsa_prompt(qi_t, w_t, kid, qa_t, ka, va_t, ksel):
    bsz, _, nq, _, tq = qa_t.shape
    nk, kc = va_t.shape[2], va_t.shape[4]
    t = ka.shape[1]
    q_spec = pl.BlockSpec((1, _PAIRS, 1, LANES, tq), lambda b, i: (b, 0, i, 0, 0))
    return pl.pallas_call(
        functools.partial(_dsa_kernel, ksel=ksel),
        grid=(bsz, nq),
        in_specs=[q_spec,
                  pl.BlockSpec((1, N_IDX_HEADS, tq), lambda b, i: (b, 0, i)),
                  pl.BlockSpec((1, t, LANES), lambda b, i: (b, 0, 0)),
                  q_spec,
                  pl.BlockSpec((1, t, _GW), lambda b, i: (b, 0, 0)),
                  pl.BlockSpec((1, _PAIRS, nk, LANES, kc), lambda b, i: (b, 0, 0, 0, 0))],
        out_specs=q_spec,
        out_shape=jax.ShapeDtypeStruct(qa_t.shape, BF16),
        scratch_shapes=[pltpu.VMEM((nk, kc, tq), F32)],
        compiler_params=_cparams(2),
        name="dsa_prompt",
    )(qi_t, w_t, kid, qa_t, ka, va_t)


def _sample_idx_kernel(pt_ref, qi_ref, w_ref, knew_ref, *rest, pages_per_step):
    page_refs = rest[:pages_per_step]
    o_ref = rest[pages_per_step]
    s = pl.program_id(1)
    nsteps = pl.num_programs(1)
    npg, tn = o_ref.shape[1] - 1, o_ref.shape[2]
    qi = qi_ref[0]
    w = w_ref[0]

    def scores(keys_t):
        lg = jnp.dot(qi, keys_t, preferred_element_type=F32)
        z = jnp.maximum(lg, 0.0) * w
        out = z[0:tn]
        for h in range(1, N_IDX_HEADS):
            out = out + z[h * tn:(h + 1) * tn]
        return out

    for i, pr in enumerate(page_refs):
        o_ref[0, s * pages_per_step + i] = scores(pr[0].astype(BF16))

    @pl.when(s == nsteps - 1)
    def _():
        r = lax.broadcasted_iota(jnp.int32, (tn, PAGE), 0)
        c = lax.broadcasted_iota(jnp.int32, (tn, PAGE), 1)
        o_ref[0, npg] = jnp.where(c <= r, scores(knew_ref[0]), -jnp.inf)


def _sample_indexer(page_table, qi_rows, w_rows, ki_new_pad, kidx_pool, pages_per_step):
    db, npg = page_table.shape
    tn = qi_rows.shape[1] // N_IDX_HEADS
    nsteps = npg // pages_per_step

    def page_spec(i):
        return pl.BlockSpec((1, D_IDX, PAGE), lambda b, s, pt: (pt[b, s * pages_per_step + i], 0, 0))

    grid_spec = pltpu.PrefetchScalarGridSpec(
        num_scalar_prefetch=1,
        grid=(db, nsteps),
        in_specs=[pl.BlockSpec((1,) + qi_rows.shape[1:], lambda b, s, pt: (b, 0, 0)),
                  pl.BlockSpec((1,) + w_rows.shape[1:], lambda b, s, pt: (b, 0, 0)),
                  pl.BlockSpec((1, D_IDX, PAGE), lambda b, s, pt: (b, 0, 0))]
                 + [page_spec(i) for i in range(pages_per_step)],
        out_specs=pl.BlockSpec((1, npg + 1, tn, PAGE), lambda b, s, pt: (b, 0, 0, 0)),
    )
    return pl.pallas_call(
        functools.partial(_sample_idx_kernel, pages_per_step=pages_per_step),
        grid_spec=grid_spec,
        out_shape=jax.ShapeDtypeStruct((db, npg + 1, tn, PAGE), F32),
        compiler_params=_cparams(2),
        name="sample_indexer",
    )(page_table, qi_rows, w_rows, ki_new_pad, *([kidx_pool] * pages_per_step))


def _sample_topk_kernel(s_ref, o_ref, *, ksel, tn, past):
    rows = s_ref.shape[1]
    o_ref[...] = s_ref[...]
    q = lax.broadcasted_iota(jnp.int32, (rows, 1), 0) & (tn - 1)
    _topk_to_bias(o_ref, o_ref.shape[0], (past + 1 + q).astype(F32), ksel, keys_axis=1)


def _sample_topk(scores, ksel, row_block):
    db, nch, tn, _ = scores.shape
    rows = db * tn
    rb = min(rows, row_block)
    by_chunk = scores.transpose(1, 0, 2, 3).reshape(nch, rows, PAGE)
    bias = pl.pallas_call(
        functools.partial(_sample_topk_kernel, ksel=ksel, tn=tn, past=(nch - 1) * PAGE),
        grid=(rows // rb,),
        in_specs=[pl.BlockSpec((nch, rb, PAGE), lambda i: (0, i, 0))],
        out_specs=pl.BlockSpec((nch, rb, PAGE), lambda i: (0, i, 0)),
        out_shape=jax.ShapeDtypeStruct((nch, rows, PAGE), F32),
        compiler_params=_cparams(1),
        name="sample_topk",
    )(by_chunk)
    return bias.reshape(nch, db, tn, PAGE).transpose(1, 0, 2, 3)


def _sample_attn_kernel(pt_ref, qa_ref, qb_ref, bias_ref, biasn_ref, lfn_ref,
                        kan_ref, van_ref, kbn_ref, vbn_ref, *rest, pps):
    lft_refs, ka_refs, va_refs, kb_refs, vb_refs = (rest[i * pps:(i + 1) * pps] for i in range(5))
    oa_ref, ob_ref, ma, la, acca, mb, lb, accb, run = rest[5 * pps:]
    p = pl.program_id(1)
    nsteps = pl.num_programs(1)
    rows = qa_ref.shape[1]
    tn = rows // N_HEADS

    @pl.when(p == 0)
    def _():
        for m in (ma, mb):
            m[...] = jnp.full(m.shape, -jnp.inf, F32)
        for z in (la, lb, acca, accb, run):
            z[...] = jnp.zeros(z.shape, F32)

    r = lax.broadcasted_iota(jnp.int32, (PAGE, PAGE), 0)
    c = lax.broadcasted_iota(jnp.int32, (PAGE, PAGE), 1)
    incl = jnp.where(r <= c, 1.0, 0.0).astype(BF16)

    def update(q, kts, vts, biases, m_ref, l_ref, acc_ref):
        s = [jnp.dot(q, kt, preferred_element_type=F32) + b for kt, b in zip(kts, biases)]
        smax = functools.reduce(jnp.maximum, s)
        m_old = m_ref[...]
        m_new = jnp.maximum(m_old, jnp.max(smax, axis=1, keepdims=True))
        a = jnp.exp(m_old - m_new)
        ps = [jnp.exp(x - m_new) for x in s]
        l_ref[...] = a * l_ref[...] + jnp.sum(functools.reduce(jnp.add, ps), axis=1, keepdims=True)
        pv = [lax.dot_general(pr.astype(BF16), vt, _NT, preferred_element_type=F32) for pr, vt in zip(ps, vts)]
        acc_ref[...] = a * acc_ref[...] + functools.reduce(jnp.add, pv)
        m_ref[...] = m_new

    def head_rows(x):
        return jnp.concatenate([jnp.broadcast_to(x[h:h + 1], (tn, x.shape[1])) for h in range(N_HEADS)], axis=0)

    def tile_rows(x):
        return jnp.concatenate([x] * N_HEADS, axis=0)

    def forget_biases(lf_ts):
        out, run_v = [], run[...]
        for lf_t in lf_ts:
            cum = _dot3(head_rows(lf_t), incl, left=False) + run_v
            run_v = cum[:, PAGE - 1:PAGE]
            out.append(-cum)
        run[...] = run_v
        return out

    cast = lambda refs: [x[0].astype(BF16) for x in refs]
    update(qa_ref[0], cast(ka_refs), cast(va_refs), [tile_rows(bias_ref[0, i]) for i in range(pps)], ma, la, acca)
    update(qb_ref[0], cast(kb_refs), cast(vb_refs), forget_biases([x[0] for x in lft_refs]), mb, lb, accb)

    @pl.when(p == nsteps - 1)
    def _():
        update(qa_ref[0], [kan_ref[0]], [van_ref[0]], [tile_rows(biasn_ref[0, 0])], ma, la, acca)
        qrow = lax.broadcasted_iota(jnp.int32, (rows, PAGE), 0) & (tn - 1)
        key = lax.broadcasted_iota(jnp.int32, (rows, PAGE), 1)
        fb = jnp.where(key <= qrow, forget_biases([lfn_ref[0]])[0], NEG)
        update(qb_ref[0], [kbn_ref[0]], [vbn_ref[0]], [fb], mb, lb, accb)
        lane = lax.broadcasted_iota(jnp.int32, (tn, _GW), 1)
        for l_ref, acc_ref, o_ref in ((la, acca, oa_ref), (lb, accb, ob_ref)):
            full = acc_ref[...] / l_ref[...]
            out = jnp.zeros((tn, _GW), F32)
            for h in range(N_HEADS):
                mine = jnp.logical_and(lane >= h * HEAD_DIM, lane < (h + 1) * HEAD_DIM)
                out = jnp.where(mine, full[h * tn:(h + 1) * tn], out)
            o_ref[0] = out.astype(o_ref.dtype)


def _sample_attn(page_table, qa_bd, qb_bd, bias, lft_pool, lfn, kan, van, kbn, vbn,
                 ka_pool, va_pool, kb_pool, vb_pool, pps):
    db, npg = page_table.shape
    rows = qa_bd.shape[1]
    tn = rows // N_HEADS
    per_b = lambda shp: pl.BlockSpec((1,) + shp, lambda b, p, pt: (b,) + (0,) * len(shp))

    def paged(shp):
        return [pl.BlockSpec((1,) + shp, lambda b, p, pt, i=i: (pt[b, p * pps + i],) + (0,) * len(shp))
                for i in range(pps)]

    grid_spec = pltpu.PrefetchScalarGridSpec(
        num_scalar_prefetch=1,
        grid=(db, npg // pps),
        in_specs=[per_b((rows, _GW)), per_b((rows, _GW)),
                  pl.BlockSpec((1, pps, tn, PAGE), lambda b, p, pt: (b, p, 0, 0)),
                  pl.BlockSpec((1, 1, tn, PAGE), lambda b, p, pt: (b, npg, 0, 0)),
                  per_b((N_HEADS, PAGE)),
                  per_b((_GW, PAGE)), per_b((_GW, PAGE)), per_b((_GW, PAGE)), per_b((_GW, PAGE))]
                 + paged((N_HEADS, PAGE)) + paged((_GW, PAGE)) + paged((_GW, PAGE))
                 + paged((_GW, PAGE)) + paged((_GW, PAGE)),
        out_specs=(per_b((tn, _GW)), per_b((tn, _GW))),
        scratch_shapes=[pltpu.VMEM((rows, 1), F32), pltpu.VMEM((rows, 1), F32), pltpu.VMEM((rows, _GW), F32),
                        pltpu.VMEM((rows, 1), F32), pltpu.VMEM((rows, 1), F32), pltpu.VMEM((rows, _GW), F32),
                        pltpu.VMEM((rows, 1), F32)],
    )
    o = jax.ShapeDtypeStruct((db, tn, _GW), BF16)
    rep = lambda a: [a] * pps
    return pl.pallas_call(
        functools.partial(_sample_attn_kernel, pps=pps),
        grid_spec=grid_spec,
        out_shape=(o, o),
        compiler_params=_cparams(2),
        name="sample_attn",
    )(page_table, qa_bd, qb_bd, bias, bias, lfn, kan, van, kbn, vbn,
      *rep(lft_pool), *rep(ka_pool), *rep(va_pool), *rep(kb_pool), *rep(vb_pool))


def _out_ln_kernel(a_ref, b_ref, x_ref, g_ref, wa_ref, wb_ref, lg_ref, lb_ref, o_ref, *, alpha):
    mix = (jnp.dot(a_ref[...], wa_ref[...], preferred_element_type=F32)
           + jnp.dot(b_ref[...], wb_ref[...], preferred_element_type=F32))
    z = alpha * x_ref[...] + (1.0 + g_ref[0]) * mix
    o_ref[...] = _ln(z, lg_ref[...], lb_ref[...])


def _out_ln(a, b, x, gate, w_out, ln_g, ln_b, alpha, tm):
    rows, d = x.shape
    nt = rows // tm
    ka = a.shape[1]
    wa, wb = w_out[:ka].astype(BF16), w_out[ka:].astype(BF16)
    const = _const_spec
    lg, lb = ln_g.reshape(1, d), ln_b.reshape(1, d)
    return pl.pallas_call(
        functools.partial(_out_ln_kernel, alpha=alpha),
        grid=(nt,),
        in_specs=[pl.BlockSpec((tm, ka), lambda i: (i, 0)), pl.BlockSpec((tm, b.shape[1]), lambda i: (i, 0)),
                  pl.BlockSpec((tm, d), lambda i: (i, 0)),
                  _mod_spec(gate, tm, nt),
                  const(wa), const(wb), const(lg), const(lb)],
        out_specs=pl.BlockSpec((tm, d), lambda i: (i, 0)),
        out_shape=jax.ShapeDtypeStruct((rows, d), F32),
        compiler_params=_cparams(1),
        name="out_ln",
    )(a, b, x, gate, wa, wb, lg, lb)


_FF_CW = 2816


def _ffn_kernel(x_ref, sc_ref, sh_ref, g_ref, prev_ref, wg_ref, wv_ref, cw_ref, wd_ref, lg_ref, lb_ref,
                o_ref, up_ref, bufg, bufv, carry, *, alpha, nseq, tiles_per_seq):
    tm, d = x_ref.shape
    ts = tm // nseq
    nch, _, cw = wg_ref.shape
    dff = nch * cw
    x = x_ref[...]
    h = (x * (1.0 + sc_ref[0]) + sh_ref[0]).astype(BF16)
    chained = tiles_per_seq > 1

    if chained:
        @pl.when(pl.program_id(0) % tiles_per_seq == 0)
        def _():
            carry[...] = prev_ref[0]

    acc = jnp.zeros((tm, d), F32)
    for c in range(nch):
        cwr = cw_ref[c]
        ys = []
        for half, (w_ref, buf) in enumerate(((wg_ref, bufg), (wv_ref, bufv))):
            lo = half * dff + c * cw
            u = jnp.dot(h, w_ref[c], preferred_element_type=F32)
            if chained:
                buf[0, 0:SUBLANES, :] = carry[:, lo:lo + cw]
                carry[:, lo:lo + cw] = u[tm - SUBLANES:, :]
                up_ref[:, lo:lo + cw] = u[tm - SUBLANES:, :]
            else:
                buf[:, 0:SUBLANES, :] = prev_ref[:, :, lo:lo + cw]
                up_ref[:, lo:lo + cw] = u
            buf[:, SUBLANES:SUBLANES + ts, :] = u.reshape(nseq, ts, cw)
            um1 = buf[:, SUBLANES - 1:SUBLANES - 1 + ts, :].reshape(tm, cw)
            um2 = buf[:, SUBLANES - 2:SUBLANES - 2 + ts, :].reshape(tm, cw)
            k0 = 4 * half
            y = cwr[k0 + 3:k0 + 4, :] + cwr[k0:k0 + 1, :] * um2
            y = y + cwr[k0 + 1:k0 + 2, :] * um1
            ys.append(y + cwr[k0 + 2:k0 + 3, :] * u)
        act = (_gelu(ys[0]) * ys[1]).astype(BF16)
        acc = acc + jnp.dot(act, wd_ref[c], preferred_element_type=F32)

    z = alpha * x + (1.0 + g_ref[0]) * acc
    o_ref[...] = _ln(z, lg_ref[...], lb_ref[...])


def _conv_ffn(x, sc, sh, gate, prev8, w_up, w_conv, b_conv, w_down, ln_g, ln_b, alpha, tm, seq_len):
    rows, d = x.shape
    dff = w_down.shape[0]
    nch = dff // _FF_CW
    nt = rows // tm
    if seq_len >= tm:
        nseq, tiles_per_seq = 1, seq_len // tm
        prev_spec = pl.BlockSpec((1, SUBLANES, 2 * dff), lambda i: (i // tiles_per_seq, 0, 0))
        up_rows, up_spec = nt * SUBLANES, pl.BlockSpec((SUBLANES, 2 * dff), lambda i: (i, 0))
    else:
        nseq, tiles_per_seq = tm // seq_len, 1
        prev_spec = pl.BlockSpec((nseq, SUBLANES, 2 * dff), lambda i: (i, 0, 0))
        up_rows, up_spec = rows, pl.BlockSpec((tm, 2 * dff), lambda i: (i, 0))
    ts = tm // nseq
    chunked = lambda w: w.reshape(w.shape[0], nch, _FF_CW).transpose(1, 0, 2)
    wg = chunked(w_up[:, :dff]).astype(BF16)
    wv = chunked(w_up[:, dff:]).astype(BF16)
    wd = w_down.reshape(nch, _FF_CW, d).astype(BF16)
    conv_rows = jnp.concatenate([w_conv[:, :dff], b_conv[None, :dff], w_conv[:, dff:], b_conv[None, dff:]], axis=0)
    cwr = chunked(conv_rows)
    lg, lb = ln_g.reshape(1, d), ln_b.reshape(1, d)
    const = _const_spec
    mod_spec = lambda m: _mod_spec(m, tm, nt)
    return pl.pallas_call(
        functools.partial(_ffn_kernel, alpha=alpha, nseq=nseq, tiles_per_seq=tiles_per_seq),
        grid=(nt,),
        in_specs=[pl.BlockSpec((tm, d), lambda i: (i, 0)), mod_spec(sc), mod_spec(sh), mod_spec(gate), prev_spec,
                  const(wg), const(wv), const(cwr), const(wd), const(lg), const(lb)],
        out_specs=(pl.BlockSpec((tm, d), lambda i: (i, 0)), up_spec),
        out_shape=(jax.ShapeDtypeStruct((rows, d), F32), jax.ShapeDtypeStruct((up_rows, 2 * dff), F32)),
        scratch_shapes=[pltpu.VMEM((nseq, ts + SUBLANES, _FF_CW), F32),
                        pltpu.VMEM((nseq, ts + SUBLANES, _FF_CW), F32),
                        pltpu.VMEM((SUBLANES, 2 * dff), F32)],
        compiler_params=_cparams(1),
        name="conv_ffn",
    )(x, sc, sh, gate, prev8, wg, wv, cwr, wd, lg, lb)


def _gmlp_kernel(x_ref, sc_ref, sh_ref, g_ref, wv_ref, wu_ref, lvg_ref, lvb_ref, wmix_ref, bs_ref, wo_ref,
                 lg_ref, lb_ref, o_ref, *maybe_v_ref, alpha):
    ng = wmix_ref.shape[0]
    gw = wu_ref.shape[1] // ng
    x = x_ref[...]
    h = (x * (1.0 + sc_ref[0]) + sh_ref[0]).astype(BF16)
    v = _ln(_gelu(jnp.dot(h, wv_ref[...], preferred_element_type=F32)), lvg_ref[...], lvb_ref[...])
    if maybe_v_ref:
        maybe_v_ref[0][...] = v
    vb = v.astype(BF16)
    bs = bs_ref[...]
    u = _gelu(jnp.dot(h, wu_ref[...], preferred_element_type=F32))
    mixed = jnp.concatenate(
        [jnp.dot(wmix_ref[g], vb[:, g * gw:(g + 1) * gw], preferred_element_type=F32) + bs[:, g:g + 1]
         for g in range(ng)], axis=1)
    acc = jnp.dot((u * mixed).astype(BF16), wo_ref[...], preferred_element_type=F32)
    z = alpha * x + (1.0 + g_ref[0]) * acc
    o_ref[...] = _ln(z, lg_ref[...], lb_ref[...])


def _chunk_gmlp(x, sc, sh, gate, w_in, lnv_g, lnv_b, w_spatial, b_spatial, w_out, ln_g, ln_b,
                alpha, tm, chunk_len, emit_v):
    rows, d = x.shape
    dcg = w_out.shape[0]
    ng = w_spatial.shape[0]
    gw = dcg // ng
    nt = rows // tm
    wu = w_in[:, :dcg].astype(BF16)
    wv = w_in[:, dcg:].astype(BF16)
    wo = w_out.astype(BF16)
    tri = (jnp.arange(chunk_len)[:, None] >= jnp.arange(chunk_len)[None, :]).astype(w_spatial.dtype)
    wc = w_spatial[:, :chunk_len, :chunk_len] * tri[None]
    eye = jnp.eye(tm // chunk_len, dtype=w_spatial.dtype)
    wmix = jnp.einsum("ab,gts->gatbs", eye, wc).reshape(ng, tm, tm).astype(BF16)
    bs = jnp.tile(b_spatial[:, :chunk_len].T, (tm // chunk_len, 1))
    bs = jnp.pad(bs, ((0, 0), (0, LANES - ng)))
    lvg, lvb = lnv_g.reshape(1, dcg), lnv_b.reshape(1, dcg)
    lg, lb = ln_g.reshape(1, d), ln_b.reshape(1, d)
    const = _const_spec
    mod_spec = lambda m: _mod_spec(m, tm, nt)
    row_spec = lambda w: pl.BlockSpec((tm, w), lambda i: (i, 0))
    out_specs, out_shape = [row_spec(d)], [jax.ShapeDtypeStruct((rows, d), F32)]
    if emit_v:
        out_specs.append(row_spec(dcg))
        out_shape.append(jax.ShapeDtypeStruct((rows, dcg), F32))
    return pl.pallas_call(
        functools.partial(_gmlp_kernel, alpha=alpha),
        grid=(nt,),
        in_specs=[row_spec(d), mod_spec(sc), mod_spec(sh), mod_spec(gate), const(wv), const(wu), const(lvg),
                  const(lvb), const(wmix), const(bs), const(wo), const(lg), const(lb)],
        out_specs=tuple(out_specs),
        out_shape=tuple(out_shape),
        compiler_params=_cparams(1),
        name="chunk_gmlp",
    )(x, sc, sh, gate, wv, wu, lvg, lvb, wmix, bs, wo, lg, lb)


def kernel(x_prompt, x_sample, cache_a_k, cache_a_v, cache_a_kidx, cache_b_k, cache_b_v, cache_b_logf,
           state_ffn_conv, page_table, c_prompt, c_sample, w_mod, b_mod, ln1_g, ln1_b, ln2_g, ln2_b,
           w_in_ab, b_forget, w_out_ab, w_in_c, lnv_g, lnv_b, w_spatial, b_spatial, w_out_c,
           w_up, w_conv, b_conv, w_down):
    bsz, t, d = x_prompt.shape
    db, tn, _ = x_sample.shape
    depth = w_mod.shape[0]
    dff2 = w_up.shape[2]
    npg = page_table.shape[1]
    past = npg * PAGE
    alpha = (2 * depth) ** 0.25
    rows_p, rows_s = bsz * t, db * tn
    tm_p = 256
    tm_ffn = 512
    tq = 512

    nc = bsz + db
    c_all = jnp.concatenate([c_prompt, c_sample, jnp.zeros((-nc % SUBLANES, d), F32)], axis=0)
    mod = _adaln_mod(c_all, w_mod, b_mod)

    xp = x_prompt.reshape(rows_p, d)
    xs = x_sample.reshape(rows_s, d)
    cos_p, sin_p = _rope_tables(jnp.arange(t, dtype=jnp.int32))
    cos_s, sin_s = _rope_tables(jnp.tile(past + jnp.arange(tn, dtype=jnp.int32), db))

    outs_p = {k: [] for k in ("ak", "av", "aki", "bk", "bv", "blf", "conv")}
    outs_s = {k: [] for k in ("ak", "av", "aki", "bk", "bv", "blf", "conv", "cv")}

    for i in range(depth):
        j = i // 2
        mp = mod[i, :bsz].reshape(bsz, 6, 1, d)
        ms = jnp.repeat(mod[i, bsz:nc].reshape(db, 6, d), tn, axis=0).reshape(1, rows_s, 6, d)
        sh1p, sc1p, g1p, sh2p, sc2p, g2p = (mp[:, k] for k in range(6))
        sh1s, sc1s, g1s, sh2s, sc2s, g2s = (ms[:, :, k] for k in range(6))

        if i % 2 == 0:
            w_packed = _pack_w_ab(w_in_ab[j])
            bf_row = jnp.zeros((1, LANES), F32).at[0, _TAIL_F:_TAIL_F + N_HEADS].set(b_forget[j])
            w_out = w_out_ab[j]

            (qa_t, kaf_t, kab, vaf_t, va_t, qi_t, qb_t, kbf_t, kbb, vbf_t, vb_t, tail, kid) = _proj_ab(
                xp, sc1p, sh1p, cos_p, sin_p, w_packed, bf_row, tq, transposed_batch=bsz)
            r3 = lambda a: a.reshape(bsz, t, a.shape[-1])
            tail3 = r3(tail)
            ckb = _logf_cumsum(tail3, min(t, 512))
            b_out = _unchunk_t(_fox_prompt(qb_t, r3(kbb), vb_t, ckb, 2))
            w_t = tail3[:, :, _TAIL_WI:_TAIL_WI + N_IDX_HEADS].transpose(0, 2, 1)
            a_out = _unchunk_t(_dsa_prompt(qi_t, w_t, r3(kid), qa_t, r3(kab), va_t, min(TOPK, t // 4)))
            xp = _out_ln(a_out.reshape(rows_p, _GW), b_out.reshape(rows_p, _GW), xp, g1p, w_out,
                         ln1_g[i], ln1_b[i], alpha, tm_p)
            heads = lambda a: a.reshape(bsz, N_HEADS, HEAD_DIM, t).transpose(0, 3, 1, 2)
            outs_p["ak"].append(heads(kaf_t)); outs_p["av"].append(heads(vaf_t))
            outs_p["aki"].append(tail3[:, :, :D_IDX])
            outs_p["bk"].append(heads(kbf_t)); outs_p["bv"].append(heads(vbf_t))
            outs_p["blf"].append(tail3[:, :, _TAIL_F:_TAIL_F + N_HEADS])

            (qa, kaf, kab, vaf, vab, qi, qb, kbf, kbb, vbf, vbb, tail, kid) = _proj_ab(
                xs, sc1s, sh1s, cos_s, sin_s, w_packed, bf_row, rows_s)
            s3 = lambda a: a.reshape(db, tn, a.shape[-1])
            tail3 = s3(tail)
            hq = lambda a: a.reshape(db, tn, N_HEADS, HEAD_DIM).transpose(0, 2, 1, 3)
            qi_rows = hq(qi).reshape(db, N_IDX_HEADS * tn, D_IDX)
            w_hq = tail3[:, :, _TAIL_WI:_TAIL_WI + N_IDX_HEADS].transpose(0, 2, 1).reshape(db, N_IDX_HEADS * tn, 1)
            w_rows = jnp.broadcast_to(w_hq, (db, N_IDX_HEADS * tn, LANES))
            keys_t = lambda a: jnp.pad(a.transpose(0, 2, 1), ((0, 0), (0, 0), (0, PAGE - tn)))
            ki_new = keys_t(s3(kid)[:, :, :D_IDX])
            scores = _sample_indexer(page_table, qi_rows, w_rows, ki_new, cache_a_kidx[j].transpose(0, 2, 1),
                                     pages_per_step=min(npg, 64))
            bias = _sample_topk(scores, min(TOPK, (past + tn) // 4), row_block=128)
            eye = jnp.eye(N_HEADS, dtype=BF16)
            bd = lambda a: jnp.einsum("bhqd,hg->bhqgd", hq(a), eye).reshape(db, N_HEADS * tn, _GW)
            lft_pool = cache_b_logf[j].transpose(0, 2, 1)
            lfn = jnp.pad(tail3[:, :, _TAIL_F:_TAIL_F + N_HEADS].transpose(0, 2, 1),
                          ((0, 0), (0, 0), (0, PAGE - tn)))
            pool = lambda cch: cch[j].transpose(0, 2, 3, 1).reshape(cch.shape[1], _GW, PAGE)
            a_out, b_out = _sample_attn(page_table, bd(qa), bd(qb), bias, lft_pool, lfn,
                                        keys_t(s3(kab)), keys_t(s3(vab)), keys_t(s3(kbb)), keys_t(s3(vbb)),
                                        pool(cache_a_k), pool(cache_a_v), pool(cache_b_k), pool(cache_b_v),
                                        pps=min(npg, 16))
            xs = _out_ln(a_out.reshape(rows_s, _GW), b_out.reshape(rows_s, _GW), xs, g1s, w_out,
                         ln1_g[i], ln1_b[i], alpha, rows_s)
            heads = lambda a: a.reshape(db, tn, N_HEADS, HEAD_DIM)
            outs_s["ak"].append(heads(kaf)); outs_s["av"].append(heads(vaf))
            outs_s["aki"].append(tail3[:, :, :D_IDX])
            outs_s["bk"].append(heads(kbf)); outs_s["bv"].append(heads(vbf))
            outs_s["blf"].append(tail3[:, :, _TAIL_F:_TAIL_F + N_HEADS])
        else:
            args = (w_in_c[j], lnv_g[j], lnv_b[j], w_spatial[j], b_spatial[j], w_out_c[j], ln1_g[i], ln1_b[i], alpha)
            (xp,) = _chunk_gmlp(xp, sc1p, sh1p, g1p, *args, tm_p, CHUNK, False)
            xs, cv = _chunk_gmlp(xs, sc1s, sh1s, g1s, *args, rows_s, tn, True)
            outs_s["cv"].append(cv.reshape(db, tn, -1))

        ffn_w = (w_up[i], w_conv[i], b_conv[i], w_down[i], ln2_g[i], ln2_b[i], alpha)
        xp, up_p = _conv_ffn(xp, sc2p, sh2p, g2p, jnp.zeros((bsz, SUBLANES, dff2), F32), *ffn_w, tm_ffn, t)
        outs_p["conv"].append(up_p.reshape(bsz, t // tm_ffn, SUBLANES, dff2)[:, -1, SUBLANES - (CONV_W - 1):])
        prev_s = jnp.pad(state_ffn_conv[i], ((0, 0), (SUBLANES - (CONV_W - 1), 0), (0, 0)))
        xs, up_s = _conv_ffn(xs, sc2s, sh2s, g2s, prev_s, *ffn_w, min(rows_s, 8 * tn), tn)
        outs_s["conv"].append(up_s.reshape(db, tn, dff2)[:, tn - (CONV_W - 1):])

    st = jnp.stack
    return (xp.reshape(bsz, t, d), xs.reshape(db, tn, d),
            st(outs_p["ak"]), st(outs_p["av"]), st(outs_p["aki"]), st(outs_p["bk"]), st(outs_p["bv"]),
            st(outs_p["blf"]), st(outs_p["conv"]),
            st(outs_s["ak"]), st(outs_s["av"]), st(outs_s["aki"]), st(outs_s["bk"]), st(outs_s["bv"]),
            st(outs_s["blf"]), st(outs_s["conv"]), st(outs_s["cv"]))
```

```python
import functools

import jax
import jax.numpy as jnp
from jax import lax
from jax.experimental import pallas as pl
from jax.experimental.pallas import tpu as pltpu

F32 = jnp.float32
BF16 = jnp.bfloat16

HEAD_DIM = 64
N_HEADS = 8
N_IDX_HEADS = 8
D_IDX = 64
TOPK = 256
PAGE = 128
CHUNK = 128
N_GROUPS_C = 8
CONV_W = 3
LN_EPS = 1e-5
ROPE_THETA = 10000.0
NEG = -1e30
LOG2E = 1.4426950408889634

LANES = 128
SUBLANES = 8
VMEM_LIMIT = 48 * 1024 * 1024

_NT = (((1,), (1,)), ((), ()))


def _cparams(n_axes):
    return pltpu.CompilerParams(dimension_semantics=("arbitrary",) * n_axes,
                                vmem_limit_bytes=VMEM_LIMIT)


def _const_spec(arr):
    return pl.BlockSpec(arr.shape, lambda i: (0,) * arr.ndim, pipeline_mode=pl.Buffered(1))


def _mod_spec(m, tm, nt):
    d = m.shape[2]
    if m.shape[1] == 1:
        tiles_per_group = nt // m.shape[0]
        return pl.BlockSpec((1, 1, d), lambda i: (i // tiles_per_group, 0, 0))
    return pl.BlockSpec((1, tm, d), lambda i: (0, i, 0))


def _ln(z, g, b):
    mu = jnp.mean(z, axis=-1, keepdims=True)
    zc = z - mu
    var = jnp.mean(zc * zc, axis=-1, keepdims=True)
    return zc * lax.rsqrt(var + LN_EPS) * g + b


def _gelu(x):
    return 0.5 * x * (1.0 + lax.erf(x * (2.0 ** -0.5)))


def _split3(x):
    hi = x.astype(BF16)
    r = x - hi.astype(F32)
    mid = r.astype(BF16)
    lo = (r - mid.astype(F32)).astype(BF16)
    return hi, mid, lo


def _dot3(x, m, left):
    parts = _split3(x)
    out = None
    for p in parts:
        d = (jnp.dot(m, p, preferred_element_type=F32) if left
             else jnp.dot(p, m, preferred_element_type=F32))
        out = d if out is None else out + d
    return out


def _mod_kernel(c_ref, w_ref, b_ref, o_ref):
    c = c_ref[...]
    a = (c * jax.nn.sigmoid(c)).astype(BF16)
    o_ref[0] = jnp.dot(a, w_ref[0].astype(BF16), preferred_element_type=F32) + b_ref[0]


def _adaln_mod(c_all, w_mod, b_mod):
    depth, d, n6 = w_mod.shape
    rows = c_all.shape[0]
    tn = n6 // 4
    return pl.pallas_call(
        _mod_kernel,
        grid=(depth, n6 // tn),
        in_specs=[pl.BlockSpec((rows, d), lambda l, j: (0, 0)),
                  pl.BlockSpec((1, d, tn), lambda l, j: (l, 0, j)),
                  pl.BlockSpec((1, 1, tn), lambda l, j: (l, 0, j))],
        out_specs=pl.BlockSpec((1, rows, tn), lambda l, j: (l, 0, j)),
        out_shape=jax.ShapeDtypeStruct((depth, rows, n6), F32),
        compiler_params=_cparams(2),
        name="adaln_mod",
    )(c_all, w_mod, b_mod.reshape(depth, 1, n6))


_G_QA, _G_KA, _G_VA, _G_QI, _G_QB, _G_KB, _G_VB = range(7)
_GW = N_HEADS * HEAD_DIM
_TAIL_OFF = 7 * _GW
_TAIL_WI = D_IDX
_TAIL_F = D_IDX + N_IDX_HEADS


def _pack_w_ab(w):
    sizes = (_GW, _GW, _GW, N_IDX_HEADS * D_IDX, D_IDX, N_IDX_HEADS, _GW, _GW, _GW, N_HEADS)
    offs, acc = [], 0
    for s in sizes[:-1]:
        acc += s
        offs.append(acc)
    qa, ka, va, qi, ki, wi, qb, kb, vb, fb = jnp.split(w, offs, axis=1)
    pad = jnp.zeros((w.shape[0], LANES - D_IDX - N_IDX_HEADS - N_HEADS), w.dtype)
    return jnp.concatenate([qa, ka, va, qi, qb, kb, vb, ki, wi, fb, pad], axis=1).astype(BF16)


def _rope_tables(pos):
    half = HEAD_DIM // 2
    inv = ROPE_THETA ** (-jnp.arange(half, dtype=F32) / half)
    ang = pos.astype(F32)[:, None] * inv[None, :]
    cos, sin = jnp.cos(ang), jnp.sin(ang)
    return (jnp.concatenate([cos] * 4, axis=1),
            jnp.concatenate([-sin, sin, -sin, sin], axis=1))


def _proj_ab_kernel(x_ref, sc_ref, sh_ref, cos_ref, sin_ref, w_ref, bf_ref,
                    qa_ref, kaf_ref, kab_ref, vaf_ref, vab_ref, qi_ref,
                    qb_ref, kbf_ref, kbb_ref, vbf_ref, vbb_ref, tail_ref, kid_ref, *, transposed):
    tm = x_ref.shape[0]

    def rows(val, ref):
        ref[...] = val.astype(ref.dtype)

    def emit(val, *refs):
        if not transposed:
            for ref in refs:
                rows(val, ref)
            return
        val_t = val.T
        for ref in refs:
            if len(ref.shape) == 5:
                ref[0, :, 0] = val_t.reshape(ref.shape[1], LANES, tm).astype(ref.dtype)
            else:
                ref[0] = val_t.astype(ref.dtype)

    h = (x_ref[...] * (1.0 + sc_ref[0]) + sh_ref[0]).astype(BF16)
    c1, s1 = cos_ref[...], sin_ref[...]
    c4 = jnp.concatenate([c1] * 4, axis=1)
    s4 = jnp.concatenate([s1] * 4, axis=1)
    lane4 = lax.broadcasted_iota(jnp.int32, (tm, _GW), 1)
    first4 = (lane4 & (HEAD_DIM - 1)) < HEAD_DIM // 2
    q_scale = HEAD_DIM ** -0.5 * (LOG2E if transposed else 1.0)

    def proj(g):
        return jnp.dot(h, w_ref[:, g * _GW:(g + 1) * _GW], preferred_element_type=F32)

    def rope(p):
        rot = jnp.where(first4, pltpu.roll(p, _GW - HEAD_DIM // 2, 1), pltpu.roll(p, HEAD_DIM // 2, 1))
        return p * c4 + rot * s4

    emit(rope(proj(_G_QA)) * q_scale, qa_ref)
    ka = rope(proj(_G_KA))
    emit(ka, kaf_ref)
    rows(ka, kab_ref)
    emit(proj(_G_VA), vaf_ref, vab_ref)
    emit(rope(proj(_G_QI)) * (D_IDX ** -0.5), qi_ref)
    emit(proj(_G_QB) * q_scale, qb_ref)
    kb = proj(_G_KB)
    emit(kb, kbf_ref)
    rows(kb, kbb_ref)
    emit(proj(_G_VB), vbf_ref, vbb_ref)

    t = jnp.dot(h, w_ref[:, _TAIL_OFF:_TAIL_OFF + LANES], preferred_element_type=F32)
    lane = lax.broadcasted_iota(jnp.int32, (tm, LANES), 1)
    first = (lane & (HEAD_DIM - 1)) < HEAD_DIM // 2
    rot = jnp.where(first, pltpu.roll(t, LANES - HEAD_DIM // 2, 1), pltpu.roll(t, HEAD_DIM // 2, 1))
    roped = t * c1 + rot * s1
    z = t + bf_ref[...]
    logf = jnp.minimum(z, 0.0) - jnp.log1p(jnp.exp(-jnp.abs(z)))
    tail = jnp.where(lane < D_IDX, roped,
                     jnp.where(lane < _TAIL_F, t * (N_IDX_HEADS ** -0.5),
                               jnp.where(lane < _TAIL_F + N_HEADS, logf, 0.0)))
    tail_ref[...] = tail
    kid_ref[...] = jnp.where(lane < D_IDX, roped, pltpu.roll(roped, D_IDX, 1)).astype(BF16)


def _proj_ab(x, sc, sh, cos_t, sin_t, w_packed, bf_row, tm, transposed_batch=None):
    rows, d = x.shape
    nt = rows // tm
    t_tiles = cos_t.shape[0] // tm
    mod_spec = _mod_spec(sc, tm, nt)
    rope_spec = pl.BlockSpec((tm, LANES), lambda i: (i % t_tiles, 0))
    narrow = lambda: pl.BlockSpec((tm, LANES), lambda i: (i, 0))
    row_spec = pl.BlockSpec((tm, _GW), lambda i: (i, 0))
    row_f, row_b = jax.ShapeDtypeStruct((rows, _GW), F32), jax.ShapeDtypeStruct((rows, _GW), BF16)
    if transposed_batch is None:
        spec_f = spec_c = row_spec
        shape_f, shape_c = row_f, row_b
    else:
        bsz = transposed_batch
        tpb = nt // bsz
        spec_f = pl.BlockSpec((1, _GW, tm), lambda i: (i // tpb, 0, i % tpb))
        spec_c = pl.BlockSpec((1, _PAIRS, 1, LANES, tm), lambda i: (i // tpb, 0, i % tpb, 0, 0))
        shape_f = jax.ShapeDtypeStruct((bsz, _GW, tpb * tm), F32)
        shape_c = jax.ShapeDtypeStruct((bsz, _PAIRS, tpb, LANES, tm), BF16)
    specs = [spec_c, spec_f, row_spec, spec_f, spec_c, spec_c, spec_c, spec_f, row_spec, spec_f, spec_c]
    shapes = [shape_c, shape_f, row_b, shape_f, shape_c, shape_c, shape_c, shape_f, row_b, shape_f, shape_c]
    out_shape = tuple(shapes) + (jax.ShapeDtypeStruct((rows, LANES), F32), jax.ShapeDtypeStruct((rows, LANES), BF16))
    return pl.pallas_call(
        functools.partial(_proj_ab_kernel, transposed=transposed_batch is not None),
        grid=(nt,),
        in_specs=[pl.BlockSpec((tm, d), lambda i: (i, 0)), mod_spec, mod_spec, rope_spec, rope_spec,
                  _const_spec(w_packed), _const_spec(bf_row)],
        out_specs=tuple(specs + [narrow(), narrow()]),
        out_shape=out_shape,
        compiler_params=_cparams(1),
        name="proj_ab",
    )(x, sc, sh, cos_t, sin_t, w_packed, bf_row)


def _cumsum_kernel(t_ref, o_ref, carry):
    @pl.when(pl.program_id(1) == 0)
    def _():
        carry[...] = jnp.zeros(carry.shape, F32)

    r = lax.broadcasted_iota(jnp.int32, (LANES, LANES), 0)
    c = lax.broadcasted_iota(jnp.int32, (LANES, LANES), 1)
    tri = jnp.where(c <= r, 1.0, 0.0).astype(BF16)
    run = carry[...]
    for k in range(t_ref.shape[1] // LANES):
        x = t_ref[0, k * LANES:(k + 1) * LANES, :]
        cum = _dot3(x, tri, left=True) + run
        run = cum[LANES - 1:LANES, :]
        for h in range(N_HEADS):
            o_ref[0, h, k * LANES:(k + 1) * LANES, :] = jnp.broadcast_to(
                cum[:, _TAIL_F + h:_TAIL_F + h + 1], (LANES, LANES))
    carry[...] = run


def _logf_cumsum(tail, tc):
    bsz, t, _ = tail.shape
    return pl.pallas_call(
        _cumsum_kernel,
        grid=(bsz, t // tc),
        in_specs=[pl.BlockSpec((1, tc, LANES), lambda b, i: (b, i, 0))],
        out_specs=pl.BlockSpec((1, N_HEADS, tc, LANES), lambda b, i: (b, 0, i, 0)),
        out_shape=jax.ShapeDtypeStruct((bsz, N_HEADS, t, LANES), F32),
        scratch_shapes=[pltpu.VMEM((1, LANES), F32)],
        compiler_params=_cparams(2),
        name="logf_cumsum",
    )(tail)


_PAIRS = N_HEADS // 2


def _chunk_t(a, width):
    bsz, t, _ = a.shape
    return a.reshape(bsz, t // width, width, _PAIRS, LANES).transpose(0, 3, 1, 4, 2)


def _unchunk_t(a):
    bsz, _, n, _, width = a.shape
    return a.transpose(0, 2, 4, 1, 3).reshape(bsz, n * width, _GW)


def _head_halves(x_t, top):
    zero = jnp.zeros_like(x_t)
    return jnp.where(top, x_t, zero), jnp.where(top, zero, x_t)


def _pair_step_t(q0, q1, kc, v_t, b0, b1, carry, top):
    m0, l0, m1, l1, acc = carry
    s0 = jnp.dot(kc, q0, preferred_element_type=F32) + b0
    s1 = jnp.dot(kc, q1, preferred_element_type=F32) + b1
    n0 = jnp.maximum(m0, jnp.max(s0, axis=0, keepdims=True))
    n1 = jnp.maximum(m1, jnp.max(s1, axis=0, keepdims=True))
    a0, a1 = jnp.exp2(m0 - n0), jnp.exp2(m1 - n1)
    p0, p1 = jnp.exp2(s0 - n0), jnp.exp2(s1 - n1)
    l0 = a0 * l0 + jnp.sum(p0, axis=0, keepdims=True)
    l1 = a1 * l1 + jnp.sum(p1, axis=0, keepdims=True)
    pv0 = jnp.dot(v_t, p0.astype(BF16), preferred_element_type=F32)
    pv1 = jnp.dot(v_t, p1.astype(BF16), preferred_element_type=F32)
    acc = jnp.where(top, a0 * acc + pv0, a1 * acc + pv1)
    return n0, l0, n1, l1, acc


def _pair_init_t(tq):
    row = lambda v: jnp.full((1, tq), v, F32)
    return row(-jnp.inf), row(0.0), row(-jnp.inf), row(0.0), jnp.zeros((LANES, tq), F32)


def _pair_finish_t(carry, top):
    _, l0, _, l1, acc = carry
    return acc / jnp.where(top, l0, l1)


def _causal_loop(it, tq, kc, step, init):
    per_tile = tq // kc
    carry = lax.fori_loop(0, it * per_tile, lambda j, c: step(j, c, None), init)
    for d in range(per_tile):
        carry = step(it * per_tile + d, carry, d * kc)
    return carry


def _fox_kernel(q_ref, k_ref, v_ref, ck_ref, o_ref):
    npair, tq, kc = q_ref.shape[1], q_ref.shape[-1], v_ref.shape[-1]
    it = pl.program_id(2)
    top = lax.broadcasted_iota(jnp.int32, (LANES, tq), 0) < HEAD_DIM
    qs = [_head_halves(q_ref[0, p, 0], top) for p in range(npair)]
    key = lax.broadcasted_iota(jnp.int32, (kc, tq), 0)
    qry = lax.broadcasted_iota(jnp.int32, (kc, tq), 1)
    wide = lambda x: jnp.concatenate([x] * (tq // LANES), axis=1)

    def step(j, carries, diag_off):
        off = pl.multiple_of(j * kc, kc)
        causal = None if diag_off is None else key + diag_off <= qry
        out = []
        for p in range(npair):
            b0 = wide(ck_ref[0, 2 * p, pl.ds(off, kc), :] * -LOG2E)
            b1 = wide(ck_ref[0, 2 * p + 1, pl.ds(off, kc), :] * -LOG2E)
            if causal is not None:
                b0, b1 = jnp.where(causal, b0, NEG), jnp.where(causal, b1, NEG)
            out.append(_pair_step_t(qs[p][0], qs[p][1], k_ref[0, pl.ds(off, kc), p * LANES:(p + 1) * LANES],
                                    v_ref[0, p, j], b0, b1, carries[p], top))
        return tuple(out)

    carries = _causal_loop(it, tq, kc, step, tuple(_pair_init_t(tq) for _ in range(npair)))
    for p in range(npair):
        o_ref[0, p, 0] = _pair_finish_t(carries[p], top).astype(o_ref.dtype)


def _fox_prompt(q_t, kb, v_t, ckb, pairs_per_step):
    bsz, _, nq, _, tq = q_t.shape
    nk, kc = v_t.shape[2], v_t.shape[4]
    t = kb.shape[1]
    pp = pairs_per_step
    return pl.pallas_call(
        _fox_kernel,
        grid=(bsz, _PAIRS // pp, nq),
        in_specs=[pl.BlockSpec((1, pp, 1, LANES, tq), lambda b, g, i: (b, g, i, 0, 0)),
                  pl.BlockSpec((1, t, pp * LANES), lambda b, g, i: (b, 0, g)),
                  pl.BlockSpec((1, pp, nk, LANES, kc), lambda b, g, i: (b, g, 0, 0, 0)),
                  pl.BlockSpec((1, 2 * pp, t, LANES), lambda b, g, i: (b, g, 0, 0))],
        out_specs=pl.BlockSpec((1, pp, 1, LANES, tq), lambda b, g, i: (b, g, i, 0, 0)),
        out_shape=jax.ShapeDtypeStruct(q_t.shape, BF16),
        compiler_params=_cparams(3),
        name="fox_prompt",
    )(q_t, kb, v_t, ckb)


def _topk_to_bias(sc_ref, nch, nvalid, ksel, keys_axis, bounds=None, n_bisect=16):
    kc = sc_ref.shape[1 + keys_axis]
    kf = float(ksel)
    ninf = -jnp.inf
    stat = lambda v: jnp.full(nvalid.shape, v, F32)

    if keys_axis == 1:
        def over_keys(elem, red, combine, init):
            return red(red(elem(sc_ref[...]), axis=0), axis=1, keepdims=True)
    else:
        def over_keys(elem, red, combine, init):
            return lax.fori_loop(
                0, nch, lambda j, a: combine(a, red(elem(sc_ref[j]), axis=0, keepdims=True)), stat(init))

    def count(t, strict):
        hit = (lambda x: jnp.where(x > t, 1.0, 0.0)) if strict else (lambda x: jnp.where(x >= t, 1.0, 0.0))
        return over_keys(hit, jnp.sum, jnp.add, 0.0)

    def count_ge(t):
        return count(t, False)

    def max_below(h):
        return over_keys(lambda x: jnp.where(x < h, x, ninf), jnp.max, jnp.maximum, ninf)

    def min_valid():
        return over_keys(lambda x: jnp.where(x == ninf, jnp.inf, x), jnp.min, jnp.minimum, jnp.inf)

    small = nvalid <= kf
    if bounds is None:
        lo = min_valid()
        v = max_below(stat(jnp.inf))
        ok = count_ge(v) >= kf
        res = jnp.where(jnp.logical_or(small, ok), 1.0, 0.0)
        thr = jnp.where(small, ninf, v)
        hi = jnp.where(res > 0.0, jnp.inf, v)
    else:
        lo, top = bounds
        res = jnp.where(small, 1.0, 0.0)
        thr = stat(ninf)
        hi = jnp.where(small, jnp.inf, top + (jnp.abs(top) * 1e-6 + 1e-30))

    def bisect(_, st):
        lo, hi = st
        mid = 0.5 * (lo + hi)
        ge = count_ge(mid) >= kf
        return jnp.where(ge, mid, lo), jnp.where(ge, hi, mid)

    lo, hi = lax.fori_loop(0, n_bisect, bisect, (lo, hi))

    def snap_cond(st):
        return jnp.min(st[2]) < 1.0

    def snap(st):
        thr, hi, res = st
        v = max_below(hi)
        ok = count_ge(v) >= kf
        open_ = res < 1.0
        new = jnp.logical_and(open_, ok)
        drop = jnp.logical_and(open_, jnp.logical_not(ok))
        return jnp.where(new, v, thr), jnp.where(drop, v, hi), jnp.where(ok, 1.0, res)

    thr, hi, res = lax.while_loop(snap_cond, snap, (thr, hi, res))

    r = lax.broadcasted_iota(jnp.int32, (kc, kc), 0)
    c = lax.broadcasted_iota(jnp.int32, (kc, kc), 1)
    prefix = jnp.where((r < c) if keys_axis == 1 else (c < r), 1.0, 0.0).astype(BF16)

    def ties(s):
        tie = jnp.logical_and(s == thr, s > ninf)
        return tie, jnp.where(tie, 1.0, 0.0)

    def write_bias(j, s, tie, before, need):
        sel = jnp.logical_or(s > thr, jnp.logical_and(tie, before < need))
        sc_ref[j] = jnp.where(sel, 0.0, NEG)

    if keys_axis == 1:
        need = kf - count(thr, True)
        s = sc_ref[...]
        tie, tief = ties(s)
        rows = s.shape[1]
        within = jnp.dot(tief.reshape(nch * rows, kc).astype(BF16), prefix,
                         preferred_element_type=F32).reshape(nch, rows, kc)
        totals = jnp.sum(tief, axis=2, keepdims=True)
        run = stat(0.0)
        for j in range(nch):
            write_bias(j, s[j], tie[j], within[j] + run, need)
            run = run + totals[j]
    else:
        plain = jnp.max(count_ge(thr)) <= kf

        @pl.when(plain)
        def _():
            def all_ge(j, carry):
                sc_ref[j] = jnp.where(sc_ref[j] >= thr, 0.0, NEG)
                return carry

            lax.fori_loop(0, nch, all_ge, 0)

        @pl.when(jnp.logical_not(plain))
        def _():
            need = kf - count(thr, True)

            def to_bias(j, run):
                s = sc_ref[j]
                tie, tief = ties(s)
                before = jnp.dot(prefix, tief.astype(BF16), preferred_element_type=F32) + run
                write_bias(j, s, tie, before, need)
                return run + jnp.sum(tief, axis=0, keepdims=True)

            lax.fori_loop(0, nch, to_bias, stat(0.0))


def _dsa_kernel(qi_ref, w_ref, kid_ref, qa_ref, ka_ref, va_ref, o_ref, sc_ref, *, ksel):
    tq, kc = qi_ref.shape[-1], va_ref.shape[-1]
    it = pl.program_id(1)
    top = lax.broadcasted_iota(jnp.int32, (LANES, tq), 0) < HEAD_DIM
    key = lax.broadcasted_iota(jnp.int32, (kc, tq), 0)
    qry = lax.broadcasted_iota(jnp.int32, (kc, tq), 1)

    qm = []
    for p in range(N_IDX_HEADS // 2):
        qm.extend(_head_halves(qi_ref[0, p, 0], top))
    w = w_ref[0]

    def score_chunk(j, carry, diag_off):
        off = pl.multiple_of(j * kc, kc)
        keys = kid_ref[0, pl.ds(off, kc), :]
        sc = jnp.zeros((kc, tq), F32)
        for h in range(N_IDX_HEADS):
            lg = jnp.dot(keys, qm[h], preferred_element_type=F32)
            sc = sc + jnp.maximum(lg, 0.0) * w[h:h + 1, :]
        lo = jnp.minimum(carry[0], jnp.min(sc, axis=0, keepdims=True))
        if diag_off is not None:
            sc = jnp.where(key + diag_off <= qry, sc, -jnp.inf)
        sc_ref[j] = sc
        return lo, jnp.maximum(carry[1], jnp.max(sc, axis=0, keepdims=True))

    bounds = _causal_loop(it, tq, kc, score_chunk,
                          (jnp.full((1, tq), jnp.inf, F32), jnp.full((1, tq), -jnp.inf, F32)))
    nch = (it + 1) * (tq // kc)
    nvalid = (it * tq + 1 + lax.broadcasted_iota(jnp.int32, (1, tq), 1)).astype(F32)
    _topk_to_bias(sc_ref, nch, nvalid, ksel, keys_axis=0, bounds=bounds)

    qs = [_head_halves(qa_ref[0, p, 0], top) for p in range(_PAIRS)]

    def step(j, carries):
        off = pl.multiple_of(j * kc, kc)
        bias = sc_ref[j]
        return tuple(
            _pair_step_t(qs[p][0], qs[p][1], ka_ref[0, pl.ds(off, kc), p * LANES:(p + 1) * LANES], va_ref[0, p, j],
                         bias, bias, carries[p], top)
            for p in range(_PAIRS))

    carries = lax.fori_loop(0, nch, step, tuple(_pair_init_t(tq) for _ in range(_PAIRS)))
    for p in range(_PAIRS):
        o_ref[0, p, 0] = _pair_finish_t(carries[p], top).astype(o_ref.dtype)


def _dsa_prompt(qi_t, w_t, kid, qa_t, ka, va_t, ksel):
    bsz, _, nq, _, tq = qa_t.shape
    nk, kc = va_t.shape[2], va_t.shape[4]
    t = ka.shape[1]
    q_spec = pl.BlockSpec((1, _PAIRS, 1, LANES, tq), lambda b, i: (b, 0, i, 0, 0))
    return pl.pallas_call(
        functools.partial(_dsa_kernel, ksel=ksel),
        grid=(bsz, nq),
        in_specs=[q_spec,
                  pl.BlockSpec((1, N_IDX_HEADS, tq), lambda b, i: (b, 0, i)),
                  pl.BlockSpec((1, t, LANES), lambda b, i: (b, 0, 0)),
                  q_spec,
                  pl.BlockSpec((1, t, _GW), lambda b, i: (b, 0, 0)),
                  pl.BlockSpec((1, _PAIRS, nk, LANES, kc), lambda b, i: (b, 0, 0, 0, 0))],
        out_specs=q_spec,
        out_shape=jax.ShapeDtypeStruct(qa_t.shape, BF16),
        scratch_shapes=[pltpu.VMEM((nk, kc, tq), F32)],
        compiler_params=_cparams(2),
        name="dsa_prompt",
    )(qi_t, w_t, kid, qa_t, ka, va_t)


def _sample_idx_kernel(pt_ref, qi_ref, w_ref, knew_ref, *rest, pages_per_step):
    page_refs = rest[:pages_per_step]
    o_ref = rest[pages_per_step]
    s = pl.program_id(1)
    nsteps = pl.num_programs(1)
    npg, tn = o_ref.shape[1] - 1, o_ref.shape[2]
    qi = qi_ref[0]
    w = w_ref[0]

    def scores(keys_t):
        lg = jnp.dot(qi, keys_t, preferred_element_type=F32)
        z = jnp.maximum(lg, 0.0) * w
        out = z[0:tn]
        for h in range(1, N_IDX_HEADS):
            out = out + z[h * tn:(h + 1) * tn]
        return out

    for i, pr in enumerate(page_refs):
        o_ref[0, s * pages_per_step + i] = scores(pr[0].astype(BF16))

    @pl.when(s == nsteps - 1)
    def _():
        r = lax.broadcasted_iota(jnp.int32, (tn, PAGE), 0)
        c = lax.broadcasted_iota(jnp.int32, (tn, PAGE), 1)
        o_ref[0, npg] = jnp.where(c <= r, scores(knew_ref[0]), -jnp.inf)


def _sample_indexer(page_table, qi_rows, w_rows, ki_new_pad, kidx_pool, pages_per_step):
    db, npg = page_table.shape
    tn = qi_rows.shape[1] // N_IDX_HEADS
    nsteps = npg // pages_per_step

    def page_spec(i):
        return pl.BlockSpec((1, D_IDX, PAGE), lambda b, s, pt: (pt[b, s * pages_per_step + i], 0, 0))

    grid_spec = pltpu.PrefetchScalarGridSpec(
        num_scalar_prefetch=1,
        grid=(db, nsteps),
        in_specs=[pl.BlockSpec((1,) + qi_rows.shape[1:], lambda b, s, pt: (b, 0, 0)),
                  pl.BlockSpec((1,) + w_rows.shape[1:], lambda b, s, pt: (b, 0, 0)),
                  pl.BlockSpec((1, D_IDX, PAGE), lambda b, s, pt: (b, 0, 0))]
                 + [page_spec(i) for i in range(pages_per_step)],
        out_specs=pl.BlockSpec((1, npg + 1, tn, PAGE), lambda b, s, pt: (b, 0, 0, 0)),
    )
    return pl.pallas_call(
        functools.partial(_sample_idx_kernel, pages_per_step=pages_per_step),
        grid_spec=grid_spec,
        out_shape=jax.ShapeDtypeStruct((db, npg + 1, tn, PAGE), F32),
        compiler_params=_cparams(2),
        name="sample_indexer",
    )(page_table, qi_rows, w_rows, ki_new_pad, *([kidx_pool] * pages_per_step))


def _sample_topk_kernel(s_ref, o_ref, *, ksel, tn, past):
    rows = s_ref.shape[1]
    o_ref[...] = s_ref[...]
    q = lax.broadcasted_iota(jnp.int32, (rows, 1), 0) & (tn - 1)
    _topk_to_bias(o_ref, o_ref.shape[0], (past + 1 + q).astype(F32), ksel, keys_axis=1)


def _sample_topk(scores, ksel, row_block):
    db, nch, tn, _ = scores.shape
    rows = db * tn
    rb = min(rows, row_block)
    by_chunk = scores.transpose(1, 0, 2, 3).reshape(nch, rows, PAGE)
    bias = pl.pallas_call(
        functools.partial(_sample_topk_kernel, ksel=ksel, tn=tn, past=(nch - 1) * PAGE),
        grid=(rows // rb,),
        in_specs=[pl.BlockSpec((nch, rb, PAGE), lambda i: (0, i, 0))],
        out_specs=pl.BlockSpec((nch, rb, PAGE), lambda i: (0, i, 0)),
        out_shape=jax.ShapeDtypeStruct((nch, rows, PAGE), F32),
        compiler_params=_cparams(1),
        name="sample_topk",
    )(by_chunk)
    return bias.reshape(nch, db, tn, PAGE).transpose(1, 0, 2, 3)


def _sample_attn_kernel(pt_ref, qa_ref, qb_ref, bias_ref, biasn_ref, lfn_ref,
                        kan_ref, van_ref, kbn_ref, vbn_ref, *rest, pps):
    lft_refs, ka_refs, va_refs, kb_refs, vb_refs = (rest[i * pps:(i + 1) * pps] for i in range(5))
    oa_ref, ob_ref, ma, la, acca, mb, lb, accb, run = rest[5 * pps:]
    p = pl.program_id(1)
    nsteps = pl.num_programs(1)
    rows = qa_ref.shape[1]
    tn = rows // N_HEADS

    @pl.when(p == 0)
    def _():
        for m in (ma, mb):
            m[...] = jnp.full(m.shape, -jnp.inf, F32)
        for z in (la, lb, acca, accb, run):
            z[...] = jnp.zeros(z.shape, F32)

    r = lax.broadcasted_iota(jnp.int32, (PAGE, PAGE), 0)
    c = lax.broadcasted_iota(jnp.int32, (PAGE, PAGE), 1)
    incl = jnp.where(r <= c, 1.0, 0.0).astype(BF16)

    def update(q, kts, vts, biases, m_ref, l_ref, acc_ref):
        s = [jnp.dot(q, kt, preferred_element_type=F32) + b for kt, b in zip(kts, biases)]
        smax = functools.reduce(jnp.maximum, s)
        m_old = m_ref[...]
        m_new = jnp.maximum(m_old, jnp.max(smax, axis=1, keepdims=True))
        a = jnp.exp(m_old - m_new)
        ps = [jnp.exp(x - m_new) for x in s]
        l_ref[...] = a * l_ref[...] + jnp.sum(functools.reduce(jnp.add, ps), axis=1, keepdims=True)
        pv = [lax.dot_general(pr.astype(BF16), vt, _NT, preferred_element_type=F32) for pr, vt in zip(ps, vts)]
        acc_ref[...] = a * acc_ref[...] + functools.reduce(jnp.add, pv)
        m_ref[...] = m_new

    def head_rows(x):
        return jnp.concatenate([jnp.broadcast_to(x[h:h + 1], (tn, x.shape[1])) for h in range(N_HEADS)], axis=0)

    def tile_rows(x):
        return jnp.concatenate([x] * N_HEADS, axis=0)

    def forget_biases(lf_ts):
        out, run_v = [], run[...]
        for lf_t in lf_ts:
            cum = _dot3(head_rows(lf_t), incl, left=False) + run_v
            run_v = cum[:, PAGE - 1:PAGE]
            out.append(-cum)
        run[...] = run_v
        return out

    cast = lambda refs: [x[0].astype(BF16) for x in refs]
    update(qa_ref[0], cast(ka_refs), cast(va_refs), [tile_rows(bias_ref[0, i]) for i in range(pps)], ma, la, acca)
    update(qb_ref[0], cast(kb_refs), cast(vb_refs), forget_biases([x[0] for x in lft_refs]), mb, lb, accb)

    @pl.when(p == nsteps - 1)
    def _():
        update(qa_ref[0], [kan_ref[0]], [van_ref[0]], [tile_rows(biasn_ref[0, 0])], ma, la, acca)
        qrow = lax.broadcasted_iota(jnp.int32, (rows, PAGE), 0) & (tn - 1)
        key = lax.broadcasted_iota(jnp.int32, (rows, PAGE), 1)
        fb = jnp.where(key <= qrow, forget_biases([lfn_ref[0]])[0], NEG)
        update(qb_ref[0], [kbn_ref[0]], [vbn_ref[0]], [fb], mb, lb, accb)
        lane = lax.broadcasted_iota(jnp.int32, (tn, _GW), 1)
        for l_ref, acc_ref, o_ref in ((la, acca, oa_ref), (lb, accb, ob_ref)):
            full = acc_ref[...] / l_ref[...]
            out = jnp.zeros((tn, _GW), F32)
            for h in range(N_HEADS):
                mine = jnp.logical_and(lane >= h * HEAD_DIM, lane < (h + 1) * HEAD_DIM)
                out = jnp.where(mine, full[h * tn:(h + 1) * tn], out)
            o_ref[0] = out.astype(o_ref.dtype)


def _sample_attn(page_table, qa_bd, qb_bd, bias, lft_pool, lfn, kan, van, kbn, vbn,
                 ka_pool, va_pool, kb_pool, vb_pool, pps):
    db, npg = page_table.shape
    rows = qa_bd.shape[1]
    tn = rows // N_HEADS
    per_b = lambda shp: pl.BlockSpec((1,) + shp, lambda b, p, pt: (b,) + (0,) * len(shp))

    def paged(shp):
        return [pl.BlockSpec((1,) + shp, lambda b, p, pt, i=i: (pt[b, p * pps + i],) + (0,) * len(shp))
                for i in range(pps)]

    grid_spec = pltpu.PrefetchScalarGridSpec(
        num_scalar_prefetch=1,
        grid=(db, npg // pps),
        in_specs=[per_b((rows, _GW)), per_b((rows, _GW)),
                  pl.BlockSpec((1, pps, tn, PAGE), lambda b, p, pt: (b, p, 0, 0)),
                  pl.BlockSpec((1, 1, tn, PAGE), lambda b, p, pt: (b, npg, 0, 0)),
                  per_b((N_HEADS, PAGE)),
                  per_b((_GW, PAGE)), per_b((_GW, PAGE)), per_b((_GW, PAGE)), per_b((_GW, PAGE))]
                 + paged((N_HEADS, PAGE)) + paged((_GW, PAGE)) + paged((_GW, PAGE))
                 + paged((_GW, PAGE)) + paged((_GW, PAGE)),
        out_specs=(per_b((tn, _GW)), per_b((tn, _GW))),
        scratch_shapes=[pltpu.VMEM((rows, 1), F32), pltpu.VMEM((rows, 1), F32), pltpu.VMEM((rows, _GW), F32),
                        pltpu.VMEM((rows, 1), F32), pltpu.VMEM((rows, 1), F32), pltpu.VMEM((rows, _GW), F32),
                        pltpu.VMEM((rows, 1), F32)],
    )
    o = jax.ShapeDtypeStruct((db, tn, _GW), BF16)
    rep = lambda a: [a] * pps
    return pl.pallas_call(
        functools.partial(_sample_attn_kernel, pps=pps),
        grid_spec=grid_spec,
        out_shape=(o, o),
        compiler_params=_cparams(2),
        name="sample_attn",
    )(page_table, qa_bd, qb_bd, bias, bias, lfn, kan, van, kbn, vbn,
      *rep(lft_pool), *rep(ka_pool), *rep(va_pool), *rep(kb_pool), *rep(vb_pool))


def _out_ln_kernel(a_ref, b_ref, x_ref, g_ref, wa_ref, wb_ref, lg_ref, lb_ref, o_ref, *, alpha):
    mix = (jnp.dot(a_ref[...], wa_ref[...], preferred_element_type=F32)
           + jnp.dot(b_ref[...], wb_ref[...], preferred_element_type=F32))
    z = alpha * x_ref[...] + (1.0 + g_ref[0]) * mix
    o_ref[...] = _ln(z, lg_ref[...], lb_ref[...])


def _out_ln(a, b, x, gate, w_out, ln_g, ln_b, alpha, tm):
    rows, d = x.shape
    nt = rows // tm
    ka = a.shape[1]
    wa, wb = w_out[:ka].astype(BF16), w_out[ka:].astype(BF16)
    const = _const_spec
    lg, lb = ln_g.reshape(1, d), ln_b.reshape(1, d)
    return pl.pallas_call(
        functools.partial(_out_ln_kernel, alpha=alpha),
        grid=(nt,),
        in_specs=[pl.BlockSpec((tm, ka), lambda i: (i, 0)), pl.BlockSpec((tm, b.shape[1]), lambda i: (i, 0)),
                  pl.BlockSpec((tm, d), lambda i: (i, 0)),
                  _mod_spec(gate, tm, nt),
                  const(wa), const(wb), const(lg), const(lb)],
        out_specs=pl.BlockSpec((tm, d), lambda i: (i, 0)),
        out_shape=jax.ShapeDtypeStruct((rows, d), F32),
        compiler_params=_cparams(1),
        name="out_ln",
    )(a, b, x, gate, wa, wb, lg, lb)


_FF_CW = 2816


def _ffn_kernel(x_ref, sc_ref, sh_ref, g_ref, prev_ref, wg_ref, wv_ref, cw_ref, wd_ref, lg_ref, lb_ref,
                o_ref, up_ref, bufg, bufv, carry, *, alpha, nseq, tiles_per_seq):
    tm, d = x_ref.shape
    ts = tm // nseq
    nch, _, cw = wg_ref.shape
    dff = nch * cw
    x = x_ref[...]
    h = (x * (1.0 + sc_ref[0]) + sh_ref[0]).astype(BF16)
    chained = tiles_per_seq > 1

    if chained:
        @pl.when(pl.program_id(0) % tiles_per_seq == 0)
        def _():
            carry[...] = prev_ref[0]

    acc = jnp.zeros((tm, d), F32)
    for c in range(nch):
        cwr = cw_ref[c]
        ys = []
        for half, (w_ref, buf) in enumerate(((wg_ref, bufg), (wv_ref, bufv))):
            lo = half * dff + c * cw
            u = jnp.dot(h, w_ref[c], preferred_element_type=F32)
            if chained:
                buf[0, 0:SUBLANES, :] = carry[:, lo:lo + cw]
                carry[:, lo:lo + cw] = u[tm - SUBLANES:, :]
                up_ref[:, lo:lo + cw] = u[tm - SUBLANES:, :]
            else:
                buf[:, 0:SUBLANES, :] = prev_ref[:, :, lo:lo + cw]
                up_ref[:, lo:lo + cw] = u
            buf[:, SUBLANES:SUBLANES + ts, :] = u.reshape(nseq, ts, cw)
            um1 = buf[:, SUBLANES - 1:SUBLANES - 1 + ts, :].reshape(tm, cw)
            um2 = buf[:, SUBLANES - 2:SUBLANES - 2 + ts, :].reshape(tm, cw)
            k0 = 4 * half
            y = cwr[k0 + 3:k0 + 4, :] + cwr[k0:k0 + 1, :] * um2
            y = y + cwr[k0 + 1:k0 + 2, :] * um1
            ys.append(y + cwr[k0 + 2:k0 + 3, :] * u)
        act = (_gelu(ys[0]) * ys[1]).astype(BF16)
        acc = acc + jnp.dot(act, wd_ref[c], preferred_element_type=F32)

    z = alpha * x + (1.0 + g_ref[0]) * acc
    o_ref[...] = _ln(z, lg_ref[...], lb_ref[...])


def _conv_ffn(x, sc, sh, gate, prev8, w_up, w_conv, b_conv, w_down, ln_g, ln_b, alpha, tm, seq_len):
    rows, d = x.shape
    dff = w_down.shape[0]
    nch = dff // _FF_CW
    nt = rows // tm
    if seq_len >= tm:
        nseq, tiles_per_seq = 1, seq_len // tm
        prev_spec = pl.BlockSpec((1, SUBLANES, 2 * dff), lambda i: (i // tiles_per_seq, 0, 0))
        up_rows, up_spec = nt * SUBLANES, pl.BlockSpec((SUBLANES, 2 * dff), lambda i: (i, 0))
    else:
        nseq, tiles_per_seq = tm // seq_len, 1
        prev_spec = pl.BlockSpec((nseq, SUBLANES, 2 * dff), lambda i: (i, 0, 0))
        up_rows, up_spec = rows, pl.BlockSpec((tm, 2 * dff), lambda i: (i, 0))
    ts = tm // nseq
    chunked = lambda w: w.reshape(w.shape[0], nch, _FF_CW).transpose(1, 0, 2)
    wg = chunked(w_up[:, :dff]).astype(BF16)
    wv = chunked(w_up[:, dff:]).astype(BF16)
    wd = w_down.reshape(nch, _FF_CW, d).astype(BF16)
    conv_rows = jnp.concatenate([w_conv[:, :dff], b_conv[None, :dff], w_conv[:, dff:], b_conv[None, dff:]], axis=0)
    cwr = chunked(conv_rows)
    lg, lb = ln_g.reshape(1, d), ln_b.reshape(1, d)
    const = _const_spec
    mod_spec = lambda m: _mod_spec(m, tm, nt)
    return pl.pallas_call(
        functools.partial(_ffn_kernel, alpha=alpha, nseq=nseq, tiles_per_seq=tiles_per_seq),
        grid=(nt,),
        in_specs=[pl.BlockSpec((tm, d), lambda i: (i, 0)), mod_spec(sc), mod_spec(sh), mod_spec(gate), prev_spec,
                  const(wg), const(wv), const(cwr), const(wd), const(lg), const(lb)],
        out_specs=(pl.BlockSpec((tm, d), lambda i: (i, 0)), up_spec),
        out_shape=(jax.ShapeDtypeStruct((rows, d), F32), jax.ShapeDtypeStruct((up_rows, 2 * dff), F32)),
        scratch_shapes=[pltpu.VMEM((nseq, ts + SUBLANES, _FF_CW), F32),
                        pltpu.VMEM((nseq, ts + SUBLANES, _FF_CW), F32),
                        pltpu.VMEM((SUBLANES, 2 * dff), F32)],
        compiler_params=_cparams(1),
        name="conv_ffn",
    )(x, sc, sh, gate, prev8, wg, wv, cwr, wd, lg, lb)


def _gmlp_kernel(x_ref, sc_ref, sh_ref, g_ref, wv_ref, wu_ref, lvg_ref, lvb_ref, wmix_ref, bs_ref, wo_ref,
                 lg_ref, lb_ref, o_ref, *maybe_v_ref, alpha):
    ng = wmix_ref.shape[0]
    gw = wu_ref.shape[1] // ng
    x = x_ref[...]
    h = (x * (1.0 + sc_ref[0]) + sh_ref[0]).astype(BF16)
    v = _ln(_gelu(jnp.dot(h, wv_ref[...], preferred_element_type=F32)), lvg_ref[...], lvb_ref[...])
    if maybe_v_ref:
        maybe_v_ref[0][...] = v
    vb = v.astype(BF16)
    bs = bs_ref[...]
    u = _gelu(jnp.dot(h, wu_ref[...], preferred_element_type=F32))
    mixed = jnp.concatenate(
        [jnp.dot(wmix_ref[g], vb[:, g * gw:(g + 1) * gw], preferred_element_type=F32) + bs[:, g:g + 1]
         for g in range(ng)], axis=1)
    acc = jnp.dot((u * mixed).astype(BF16), wo_ref[...], preferred_element_type=F32)
    z = alpha * x + (1.0 + g_ref[0]) * acc
    o_ref[...] = _ln(z, lg_ref[...], lb_ref[...])


def _chunk_gmlp(x, sc, sh, gate, w_in, lnv_g, lnv_b, w_spatial, b_spatial, w_out, ln_g, ln_b,
                alpha, tm, chunk_len, emit_v):
    rows, d = x.shape
    dcg = w_out.shape[0]
    ng = w_spatial.shape[0]
    gw = dcg // ng
    nt = rows // tm
    wu = w_in[:, :dcg].astype(BF16)
    wv = w_in[:, dcg:].astype(BF16)
    wo = w_out.astype(BF16)
    tri = (jnp.arange(chunk_len)[:, None] >= jnp.arange(chunk_len)[None, :]).astype(w_spatial.dtype)
    wc = w_spatial[:, :chunk_len, :chunk_len] * tri[None]
    eye = jnp.eye(tm // chunk_len, dtype=w_spatial.dtype)
    wmix = jnp.einsum("ab,gts->gatbs", eye, wc).reshape(ng, tm, tm).astype(BF16)
    bs = jnp.tile(b_spatial[:, :chunk_len].T, (tm // chunk_len, 1))
    bs = jnp.pad(bs, ((0, 0), (0, LANES - ng)))
    lvg, lvb = lnv_g.reshape(1, dcg), lnv_b.reshape(1, dcg)
    lg, lb = ln_g.reshape(1, d), ln_b.reshape(1, d)
    const = _const_spec
    mod_spec = lambda m: _mod_spec(m, tm, nt)
    row_spec = lambda w: pl.BlockSpec((tm, w), lambda i: (i, 0))
    out_specs, out_shape = [row_spec(d)], [jax.ShapeDtypeStruct((rows, d), F32)]
    if emit_v:
        out_specs.append(row_spec(dcg))
        out_shape.append(jax.ShapeDtypeStruct((rows, dcg), F32))
    return pl.pallas_call(
        functools.partial(_gmlp_kernel, alpha=alpha),
        grid=(nt,),
        in_specs=[row_spec(d), mod_spec(sc), mod_spec(sh), mod_spec(gate), const(wv), const(wu), const(lvg),
                  const(lvb), const(wmix), const(bs), const(wo), const(lg), const(lb)],
        out_specs=tuple(out_specs),
        out_shape=tuple(out_shape),
        compiler_params=_cparams(1),
        name="chunk_gmlp",
    )(x, sc, sh, gate, wv, wu, lvg, lvb, wmix, bs, wo, lg, lb)


def kernel(x_prompt, x_sample, cache_a_k, cache_a_v, cache_a_kidx, cache_b_k, cache_b_v, cache_b_logf,
           state_ffn_conv, page_table, c_prompt, c_sample, w_mod, b_mod, ln1_g, ln1_b, ln2_g, ln2_b,
           w_in_ab, b_forget, w_out_ab, w_in_c, lnv_g, lnv_b, w_spatial, b_spatial, w_out_c,
           w_up, w_conv, b_conv, w_down):
    bsz, t, d = x_prompt.shape
    db, tn, _ = x_sample.shape
    depth = w_mod.shape[0]
    dff2 = w_up.shape[2]
    npg = page_table.shape[1]
    past = npg * PAGE
    alpha = (2 * depth) ** 0.25
    rows_p, rows_s = bsz * t, db * tn
    tm_p = 256
    tm_ffn = 512
    tq = 512

    nc = bsz + db
    c_all = jnp.concatenate([c_prompt, c_sample, jnp.zeros((-nc % SUBLANES, d), F32)], axis=0)
    mod = _adaln_mod(c_all, w_mod, b_mod)

    xp = x_prompt.reshape(rows_p, d)
    xs = x_sample.reshape(rows_s, d)
    cos_p, sin_p = _rope_tables(jnp.arange(t, dtype=jnp.int32))
    cos_s, sin_s = _rope_tables(jnp.tile(past + jnp.arange(tn, dtype=jnp.int32), db))

    outs_p = {k: [] for k in ("ak", "av", "aki", "bk", "bv", "blf", "conv")}
    outs_s = {k: [] for k in ("ak", "av", "aki", "bk", "bv", "blf", "conv", "cv")}

    for i in range(depth):
        j = i // 2
        mp = mod[i, :bsz].reshape(bsz, 6, 1, d)
        ms = jnp.repeat(mod[i, bsz:nc].reshape(db, 6, d), tn, axis=0).reshape(1, rows_s, 6, d)
        sh1p, sc1p, g1p, sh2p, sc2p, g2p = (mp[:, k] for k in range(6))
        sh1s, sc1s, g1s, sh2s, sc2s, g2s = (ms[:, :, k] for k in range(6))

        if i % 2 == 0:
            w_packed = _pack_w_ab(w_in_ab[j])
            bf_row = jnp.zeros((1, LANES), F32).at[0, _TAIL_F:_TAIL_F + N_HEADS].set(b_forget[j])
            w_out = w_out_ab[j]

            (qa_t, kaf_t, kab, vaf_t, va_t, qi_t, qb_t, kbf_t, kbb, vbf_t, vb_t, tail, kid) = _proj_ab(
                xp, sc1p, sh1p, cos_p, sin_p, w_packed, bf_row, tq, transposed_batch=bsz)
            r3 = lambda a: a.reshape(bsz, t, a.shape[-1])
            tail3 = r3(tail)
            ckb = _logf_cumsum(tail3, min(t, 512))
            b_out = _unchunk_t(_fox_prompt(qb_t, r3(kbb), vb_t, ckb, 2))
            w_t = tail3[:, :, _TAIL_WI:_TAIL_WI + N_IDX_HEADS].transpose(0, 2, 1)
            a_out = _unchunk_t(_dsa_prompt(qi_t, w_t, r3(kid), qa_t, r3(kab), va_t, min(TOPK, t // 4)))
            xp = _out_ln(a_out.reshape(rows_p, _GW), b_out.reshape(rows_p, _GW), xp, g1p, w_out,
                         ln1_g[i], ln1_b[i], alpha, tm_p)
            heads = lambda a: a.reshape(bsz, N_HEADS, HEAD_DIM, t).transpose(0, 3, 1, 2)
            outs_p["ak"].append(heads(kaf_t)); outs_p["av"].append(heads(vaf_t))
            outs_p["aki"].append(tail3[:, :, :D_IDX])
            outs_p["bk"].append(heads(kbf_t)); outs_p["bv"].append(heads(vbf_t))
            outs_p["blf"].append(tail3[:, :, _TAIL_F:_TAIL_F + N_HEADS])

            (qa, kaf, kab, vaf, vab, qi, qb, kbf, kbb, vbf, vbb, tail, kid) = _proj_ab(
                xs, sc1s, sh1s, cos_s, sin_s, w_packed, bf_row, rows_s)
            s3 = lambda a: a.reshape(db, tn, a.shape[-1])
            tail3 = s3(tail)
            hq = lambda a: a.reshape(db, tn, N_HEADS, HEAD_DIM).transpose(0, 2, 1, 3)
            qi_rows = hq(qi).reshape(db, N_IDX_HEADS * tn, D_IDX)
            w_hq = tail3[:, :, _TAIL_WI:_TAIL_WI + N_IDX_HEADS].transpose(0, 2, 1).reshape(db, N_IDX_HEADS * tn, 1)
            w_rows = jnp.broadcast_to(w_hq, (db, N_IDX_HEADS * tn, LANES))
            keys_t = lambda a: jnp.pad(a.transpose(0, 2, 1), ((0, 0), (0, 0), (0, PAGE - tn)))
            ki_new = keys_t(s3(kid)[:, :, :D_IDX])
            scores = _sample_indexer(page_table, qi_rows, w_rows, ki_new, cache_a_kidx[j].transpose(0, 2, 1),
                                     pages_per_step=min(npg, 64))
            bias = _sample_topk(scores, min(TOPK, (past + tn) // 4), row_block=128)
            eye = jnp.eye(N_HEADS, dtype=BF16)
            bd = lambda a: jnp.einsum("bhqd,hg->bhqgd", hq(a), eye).reshape(db, N_HEADS * tn, _GW)
            lft_pool = cache_b_logf[j].transpose(0, 2, 1)
            lfn = jnp.pad(tail3[:, :, _TAIL_F:_TAIL_F + N_HEADS].transpose(0, 2, 1),
                          ((0, 0), (0, 0), (0, PAGE - tn)))
            pool = lambda cch: cch[j].transpose(0, 2, 3, 1).reshape(cch.shape[1], _GW, PAGE)
            a_out, b_out = _sample_attn(page_table, bd(qa), bd(qb), bias, lft_pool, lfn,
                                        keys_t(s3(kab)), keys_t(s3(vab)), keys_t(s3(kbb)), keys_t(s3(vbb)),
                                        pool(cache_a_k), pool(cache_a_v), pool(cache_b_k), pool(cache_b_v),
                                        pps=min(npg, 16))
            xs = _out_ln(a_out.reshape(rows_s, _GW), b_out.reshape(rows_s, _GW), xs, g1s, w_out,
                         ln1_g[i], ln1_b[i], alpha, rows_s)
            heads = lambda a: a.reshape(db, tn, N_HEADS, HEAD_DIM)
            outs_s["ak"].append(heads(kaf)); outs_s["av"].append(heads(vaf))
            outs_s["aki"].append(tail3[:, :, :D_IDX])
            outs_s["bk"].append(heads(kbf)); outs_s["bv"].append(heads(vbf))
            outs_s["blf"].append(tail3[:, :, _TAIL_F:_TAIL_F + N_HEADS])
        else:
            args = (w_in_c[j], lnv_g[j], lnv_b[j], w_spatial[j], b_spatial[j], w_out_c[j], ln1_g[i], ln1_b[i], alpha)
            (xp,) = _chunk_gmlp(xp, sc1p, sh1p, g1p, *args, tm_p, CHUNK, False)
            xs, cv = _chunk_gmlp(xs, sc1s, sh1s, g1s, *args, rows_s, tn, True)
            outs_s["cv"].append(cv.reshape(db, tn, -1))

        ffn_w = (w_up[i], w_conv[i], b_conv[i], w_down[i], ln2_g[i], ln2_b[i], alpha)
        xp, up_p = _conv_ffn(xp, sc2p, sh2p, g2p, jnp.zeros((bsz, SUBLANES, dff2), F32), *ffn_w, tm_ffn, t)
        outs_p["conv"].append(up_p.reshape(bsz, t // tm_ffn, SUBLANES, dff2)[:, -1, SUBLANES - (CONV_W - 1):])
        prev_s = jnp.pad(state_ffn_conv[i], ((0, 0), (SUBLANES - (CONV_W - 1), 0), (0, 0)))
        xs, up_s = _conv_ffn(xs, sc2s, sh2s, g2s, prev_s, *ffn_w, min(rows_s, 8 * tn), tn)
        outs_s["conv"].append(up_s.reshape(db, tn, dff2)[:, tn - (CONV_W - 1):])

    st = jnp.stack
    return (xp.reshape(bsz, t, d), xs.reshape(db, tn, d),
            st(outs_p["ak"]), st(outs_p["av"]), st(outs_p["aki"]), st(outs_p["bk"]), st(outs_p["bv"]),
            st(outs_p["blf"]), st(outs_p["conv"]),
            st(outs_s["ak"]), st(outs_s["av"]), st(outs_s["aki"]), st(outs_s["bk"]), st(outs_s["bv"]),
            st(outs_s["blf"]), st(outs_s["conv"]), st(outs_s["cv"]))
```

```python
import functools

import jax
import jax.numpy as jnp
from jax import lax
from jax.experimental import pallas as pl
from jax.experimental.pallas import tpu as pltpu

F32 = jnp.float32
BF16 = jnp.bfloat16

HEAD_DIM = 64
N_HEADS = 8
N_IDX_HEADS = 8
D_IDX = 64
TOPK = 256
PAGE = 128
CHUNK = 128
N_GROUPS_C = 8
CONV_W = 3
LN_EPS = 1e-5
ROPE_THETA = 10000.0
NEG = -1e30
LOG2E = 1.4426950408889634

LANES = 128
SUBLANES = 8
VMEM_LIMIT = 48 * 1024 * 1024

_NT = (((1,), (1,)), ((), ()))


def _cparams(n_axes):
    return pltpu.CompilerParams(dimension_semantics=("arbitrary",) * n_axes,
                                vmem_limit_bytes=VMEM_LIMIT)


def _const_spec(arr):
    return pl.BlockSpec(arr.shape, lambda i: (0,) * arr.ndim, pipeline_mode=pl.Buffered(1))


def _mod_spec(m, tm, nt):
    d = m.shape[2]
    if m.shape[1] == 1:
        tiles_per_group = nt // m.shape[0]
        return pl.BlockSpec((1, 1, d), lambda i: (i // tiles_per_group, 0, 0))
    return pl.BlockSpec((1, tm, d), lambda i: (0, i, 0))


def _ln(z, g, b):
    mu = jnp.mean(z, axis=-1, keepdims=True)
    zc = z - mu
    var = jnp.mean(zc * zc, axis=-1, keepdims=True)
    return zc * lax.rsqrt(var + LN_EPS) * g + b


def _gelu(x):
    return 0.5 * x * (1.0 + lax.erf(x * (2.0 ** -0.5)))


def _split3(x):
    hi = x.astype(BF16)
    r = x - hi.astype(F32)
    mid = r.astype(BF16)
    lo = (r - mid.astype(F32)).astype(BF16)
    return hi, mid, lo


def _dot3(x, m, left):
    parts = _split3(x)
    out = None
    for p in parts:
        d = (jnp.dot(m, p, preferred_element_type=F32) if left
             else jnp.dot(p, m, preferred_element_type=F32))
        out = d if out is None else out + d
    return out


def _mod_kernel(c_ref, w_ref, b_ref, o_ref):
    c = c_ref[...]
    a = (c * jax.nn.sigmoid(c)).astype(BF16)
    o_ref[0] = jnp.dot(a, w_ref[0].astype(BF16), preferred_element_type=F32) + b_ref[0]


def _adaln_mod(c_all, w_mod, b_mod):
    depth, d, n6 = w_mod.shape
    rows = c_all.shape[0]
    tn = n6 // 4
    return pl.pallas_call(
        _mod_kernel,
        grid=(depth, n6 // tn),
        in_specs=[pl.BlockSpec((rows, d), lambda l, j: (0, 0)),
                  pl.BlockSpec((1, d, tn), lambda l, j: (l, 0, j)),
                  pl.BlockSpec((1, 1, tn), lambda l, j: (l, 0, j))],
        out_specs=pl.BlockSpec((1, rows, tn), lambda l, j: (l, 0, j)),
        out_shape=jax.ShapeDtypeStruct((depth, rows, n6), F32),
        compiler_params=_cparams(2),
        name="adaln_mod",
    )(c_all, w_mod, b_mod.reshape(depth, 1, n6))


_G_QA, _G_KA, _G_VA, _G_QI, _G_QB, _G_KB, _G_VB = range(7)
_GW = N_HEADS * HEAD_DIM
_TAIL_OFF = 7 * _GW
_TAIL_WI = D_IDX
_TAIL_F = D_IDX + N_IDX_HEADS


def _pack_w_ab(w):
    sizes = (_GW, _GW, _GW, N_IDX_HEADS * D_IDX, D_IDX, N_IDX_HEADS, _GW, _GW, _GW, N_HEADS)
    offs, acc = [], 0
    for s in sizes[:-1]:
        acc += s
        offs.append(acc)
    qa, ka, va, qi, ki, wi, qb, kb, vb, fb = jnp.split(w, offs, axis=1)
    pad = jnp.zeros((w.shape[0], LANES - D_IDX - N_IDX_HEADS - N_HEADS), w.dtype)
    return jnp.concatenate([qa, ka, va, qi, qb, kb, vb, ki, wi, fb, pad], axis=1).astype(BF16)


def _rope_tables(pos):
    half = HEAD_DIM // 2
    inv = ROPE_THETA ** (-jnp.arange(half, dtype=F32) / half)
    ang = pos.astype(F32)[:, None] * inv[None, :]
    cos, sin = jnp.cos(ang), jnp.sin(ang)
    return (jnp.concatenate([cos] * 4, axis=1),
            jnp.concatenate([-sin, sin, -sin, sin], axis=1))


def _proj_ab_kernel(x_ref, sc_ref, sh_ref, cos_ref, sin_ref, w_ref, bf_ref,
                    qa_ref, kaf_ref, kab_ref, vaf_ref, vab_ref, qi_ref,
                    qb_ref, kbf_ref, kbb_ref, vbf_ref, vbb_ref, tail_ref, kid_ref, *, transposed):
    tm = x_ref.shape[0]

    def rows(val, ref):
        ref[...] = val.astype(ref.dtype)

    def emit(val, *refs):
        if not transposed:
            for ref in refs:
                rows(val, ref)
            return
        val_t = val.T
        for ref in refs:
            if len(ref.shape) == 5:
                ref[0, :, 0] = val_t.reshape(ref.shape[1], LANES, tm).astype(ref.dtype)
            else:
                ref[0] = val_t.astype(ref.dtype)

    h = (x_ref[...] * (1.0 + sc_ref[0]) + sh_ref[0]).astype(BF16)
    c1, s1 = cos_ref[...], sin_ref[...]
    c4 = jnp.concatenate([c1] * 4, axis=1)
    s4 = jnp.concatenate([s1] * 4, axis=1)
    lane4 = lax.broadcasted_iota(jnp.int32, (tm, _GW), 1)
    first4 = (lane4 & (HEAD_DIM - 1)) < HEAD_DIM // 2
    q_scale = HEAD_DIM ** -0.5 * (LOG2E if transposed else 1.0)

    def proj(g):
        return jnp.dot(h, w_ref[:, g * _GW:(g + 1) * _GW], preferred_element_type=F32)

    def rope(p):
        rot = jnp.where(first4, pltpu.roll(p, _GW - HEAD_DIM // 2, 1), pltpu.roll(p, HEAD_DIM // 2, 1))
        return p * c4 + rot * s4

    emit(rope(proj(_G_QA)) * q_scale, qa_ref)
    ka = rope(proj(_G_KA))
    emit(ka, kaf_ref)
    rows(ka, kab_ref)
    emit(proj(_G_VA), vaf_ref, vab_ref)
    emit(rope(proj(_G_QI)) * (D_IDX ** -0.5), qi_ref)
    emit(proj(_G_QB) * q_scale, qb_ref)
    kb = proj(_G_KB)
    emit(kb, kbf_ref)
    rows(kb, kbb_ref)
    emit(proj(_G_VB), vbf_ref, vbb_ref)

    t = jnp.dot(h, w_ref[:, _TAIL_OFF:_TAIL_OFF + LANES], preferred_element_type=F32)
    lane = lax.broadcasted_iota(jnp.int32, (tm, LANES), 1)
    first = (lane & (HEAD_DIM - 1)) < HEAD_DIM // 2
    rot = jnp.where(first, pltpu.roll(t, LANES - HEAD_DIM // 2, 1), pltpu.roll(t, HEAD_DIM // 2, 1))
    roped = t * c1 + rot * s1
    z = t + bf_ref[...]
    logf = jnp.minimum(z, 0.0) - jnp.log1p(jnp.exp(-jnp.abs(z)))
    tail = jnp.where(lane < D_IDX, roped,
                     jnp.where(lane < _TAIL_F, t * (N_IDX_HEADS ** -0.5),
                               jnp.where(lane < _TAIL_F + N_HEADS, logf, 0.0)))
    tail_ref[...] = tail
    kid_ref[...] = jnp.where(lane < D_IDX, roped, pltpu.roll(roped, D_IDX, 1)).astype(BF16)


def _proj_ab(x, sc, sh, cos_t, sin_t, w_packed, bf_row, tm, transposed_batch=None):
    rows, d = x.shape
    nt = rows // tm
    t_tiles = cos_t.shape[0] // tm
    mod_spec = _mod_spec(sc, tm, nt)
    rope_spec = pl.BlockSpec((tm, LANES), lambda i: (i % t_tiles, 0))
    narrow = lambda: pl.BlockSpec((tm, LANES), lambda i: (i, 0))
    row_spec = pl.BlockSpec((tm, _GW), lambda i: (i, 0))
    row_f, row_b = jax.ShapeDtypeStruct((rows, _GW), F32), jax.ShapeDtypeStruct((rows, _GW), BF16)
    if transposed_batch is None:
        spec_f = spec_c = row_spec
        shape_f, shape_c = row_f, row_b
    else:
        bsz = transposed_batch
        tpb = nt // bsz
        spec_f = pl.BlockSpec((1, _GW, tm), lambda i: (i // tpb, 0, i % tpb))
        spec_c = pl.BlockSpec((1, _PAIRS, 1, LANES, tm), lambda i: (i // tpb, 0, i % tpb, 0, 0))
        shape_f = jax.ShapeDtypeStruct((bsz, _GW, tpb * tm), F32)
        shape_c = jax.ShapeDtypeStruct((bsz, _PAIRS, tpb, LANES, tm), BF16)
    specs = [spec_c, spec_f, row_spec, spec_f, spec_c, spec_c, spec_c, spec_f, row_spec, spec_f, spec_c]
    shapes = [shape_c, shape_f, row_b, shape_f, shape_c, shape_c, shape_c, shape_f, row_b, shape_f, shape_c]
    out_shape = tuple(shapes) + (jax.ShapeDtypeStruct((rows, LANES), F32), jax.ShapeDtypeStruct((rows, LANES), BF16))
    return pl.pallas_call(
        functools.partial(_proj_ab_kernel, transposed=transposed_batch is not None),
        grid=(nt,),
        in_specs=[pl.BlockSpec((tm, d), lambda i: (i, 0)), mod_spec, mod_spec, rope_spec, rope_spec,
                  _const_spec(w_packed), _const_spec(bf_row)],
        out_specs=tuple(specs + [narrow(), narrow()]),
        out_shape=out_shape,
        compiler_params=_cparams(1),
        name="proj_ab",
    )(x, sc, sh, cos_t, sin_t, w_packed, bf_row)


def _cumsum_kernel(t_ref, o_ref, carry):
    @pl.when(pl.program_id(1) == 0)
    def _():
        carry[...] = jnp.zeros(carry.shape, F32)

    r = lax.broadcasted_iota(jnp.int32, (LANES, LANES), 0)
    c = lax.broadcasted_iota(jnp.int32, (LANES, LANES), 1)
    tri = jnp.where(c <= r, 1.0, 0.0).astype(BF16)
    run = carry[...]
    for k in range(t_ref.shape[1] // LANES):
        x = t_ref[0, k * LANES:(k + 1) * LANES, :]
        cum = _dot3(x, tri, left=True) + run
        run = cum[LANES - 1:LANES, :]
        for h in range(N_HEADS):
            o_ref[0, h, k * LANES:(k + 1) * LANES, :] = jnp.broadcast_to(
                cum[:, _TAIL_F + h:_TAIL_F + h + 1], (LANES, LANES))
    carry[...] = run


def _logf_cumsum(tail, tc):
    bsz, t, _ = tail.shape
    return pl.pallas_call(
        _cumsum_kernel,
        grid=(bsz, t // tc),
        in_specs=[pl.BlockSpec((1, tc, LANES), lambda b, i: (b, i, 0))],
        out_specs=pl.BlockSpec((1, N_HEADS, tc, LANES), lambda b, i: (b, 0, i, 0)),
        out_shape=jax.ShapeDtypeStruct((bsz, N_HEADS, t, LANES), F32),
        scratch_shapes=[pltpu.VMEM((1, LANES), F32)],
        compiler_params=_cparams(2),
        name="logf_cumsum",
    )(tail)


_PAIRS = N_HEADS // 2


def _chunk_t(a, width):
    bsz, t, _ = a.shape
    return a.reshape(bsz, t // width, width, _PAIRS, LANES).transpose(0, 3, 1, 4, 2)


def _unchunk_t(a):
    bsz, _, n, _, width = a.shape
    return a.transpose(0, 2, 4, 1, 3).reshape(bsz, n * width, _GW)


def _head_halves(x_t, top):
    zero = jnp.zeros_like(x_t)
    return jnp.where(top, x_t, zero), jnp.where(top, zero, x_t)


def _pair_step_t(q0, q1, kc, v_t, b0, b1, carry, top):
    m0, l0, m1, l1, acc = carry
    s0 = jnp.dot(kc, q0, preferred_element_type=F32) + b0
    s1 = jnp.dot(kc, q1, preferred_element_type=F32) + b1
    n0 = jnp.maximum(m0, jnp.max(s0, axis=0, keepdims=True))
    n1 = jnp.maximum(m1, jnp.max(s1, axis=0, keepdims=True))
    a0, a1 = jnp.exp2(m0 - n0), jnp.exp2(m1 - n1)
    p0, p1 = jnp.exp2(s0 - n0), jnp.exp2(s1 - n1)
    l0 = a0 * l0 + jnp.sum(p0, axis=0, keepdims=True)
    l1 = a1 * l1 + jnp.sum(p1, axis=0, keepdims=True)
    pv0 = jnp.dot(v_t, p0.astype(BF16), preferred_element_type=F32)
    pv1 = jnp.dot(v_t, p1.astype(BF16), preferred_element_type=F32)
    acc = jnp.where(top, a0 * acc + pv0, a1 * acc + pv1)
    return n0, l0, n1, l1, acc


def _pair_init_t(tq):
    row = lambda v: jnp.full((1, tq), v, F32)
    return row(-jnp.inf), row(0.0), row(-jnp.inf), row(0.0), jnp.zeros((LANES, tq), F32)


def _pair_finish_t(carry, top):
    _, l0, _, l1, acc = carry
    return acc / jnp.where(top, l0, l1)


def _causal_loop(it, tq, kc, step, init):
    per_tile = tq // kc
    carry = lax.fori_loop(0, it * per_tile, lambda j, c: step(j, c, None), init)
    for d in range(per_tile):
        carry = step(it * per_tile + d, carry, d * kc)
    return carry


def _fox_kernel(q_ref, k_ref, v_ref, ck_ref, o_ref):
    npair, tq, kc = q_ref.shape[1], q_ref.shape[-1], v_ref.shape[-1]
    it = pl.program_id(2)
    top = lax.broadcasted_iota(jnp.int32, (LANES, tq), 0) < HEAD_DIM
    qs = [_head_halves(q_ref[0, p, 0], top) for p in range(npair)]
    key = lax.broadcasted_iota(jnp.int32, (kc, tq), 0)
    qry = lax.broadcasted_iota(jnp.int32, (kc, tq), 1)
    wide = lambda x: jnp.concatenate([x] * (tq // LANES), axis=1)

    def step(j, carries, diag_off):
        off = pl.multiple_of(j * kc, kc)
        causal = None if diag_off is None else key + diag_off <= qry
        out = []
        for p in range(npair):
            b0 = wide(ck_ref[0, 2 * p, pl.ds(off, kc), :] * -LOG2E)
            b1 = wide(ck_ref[0, 2 * p + 1, pl.ds(off, kc), :] * -LOG2E)
            if causal is not None:
                b0, b1 = jnp.where(causal, b0, NEG), jnp.where(causal, b1, NEG)
            out.append(_pair_step_t(qs[p][0], qs[p][1], k_ref[0, pl.ds(off, kc), p * LANES:(p + 1) * LANES],
                                    v_ref[0, p, j], b0, b1, carries[p], top))
        return tuple(out)

    carries = _causal_loop(it, tq, kc, step, tuple(_pair_init_t(tq) for _ in range(npair)))
    for p in range(npair):
        o_ref[0, p, 0] = _pair_finish_t(carries[p], top).astype(o_ref.dtype)


def _fox_prompt(q_t, kb, v_t, ckb, pairs_per_step):
    bsz, _, nq, _, tq = q_t.shape
    nk, kc = v_t.shape[2], v_t.shape[4]
    t = kb.shape[1]
    pp = pairs_per_step
    return pl.pallas_call(
        _fox_kernel,
        grid=(bsz, _PAIRS // pp, nq),
        in_specs=[pl.BlockSpec((1, pp, 1, LANES, tq), lambda b, g, i: (b, g, i, 0, 0)),
                  pl.BlockSpec((1, t, pp * LANES), lambda b, g, i: (b, 0, g)),
                  pl.BlockSpec((1, pp, nk, LANES, kc), lambda b, g, i: (b, g, 0, 0, 0)),
                  pl.BlockSpec((1, 2 * pp, t, LANES), lambda b, g, i: (b, g, 0, 0))],
        out_specs=pl.BlockSpec((1, pp, 1, LANES, tq), lambda b, g, i: (b, g, i, 0, 0)),
        out_shape=jax.ShapeDtypeStruct(q_t.shape, BF16),
        compiler_params=_cparams(3),
        name="fox_prompt",
    )(q_t, kb, v_t, ckb)


def _topk_to_bias(sc_ref, nch, nvalid, ksel, keys_axis, bounds=None, n_bisect=16):
    kc = sc_ref.shape[1 + keys_axis]
    kf = float(ksel)
    ninf = -jnp.inf
    stat = lambda v: jnp.full(nvalid.shape, v, F32)

    if keys_axis == 1:
        def over_keys(elem, red, combine, init):
            return red(red(elem(sc_ref[...]), axis=0), axis=1, keepdims=True)
    else:
        def over_keys(elem, red, combine, init):
            return lax.fori_loop(
                0, nch, lambda j, a: combine(a, red(elem(sc_ref[j]), axis=0, keepdims=True)), stat(init))

    def count(t, strict):
        hit = (lambda x: jnp.where(x > t, 1.0, 0.0)) if strict else (lambda x: jnp.where(x >= t, 1.0, 0.0))
        return over_keys(hit, jnp.sum, jnp.add, 0.0)

    def count_ge(t):
        return count(t, False)

    def max_below(h):
        return over_keys(lambda x: jnp.where(x < h, x, ninf), jnp.max, jnp.maximum, ninf)

    def min_valid():
        return over_keys(lambda x: jnp.where(x == ninf, jnp.inf, x), jnp.min, jnp.minimum, jnp.inf)

    small = nvalid <= kf
    if bounds is None:
        lo = min_valid()
        v = max_below(stat(jnp.inf))
        ok = count_ge(v) >= kf
        res = jnp.where(jnp.logical_or(small, ok), 1.0, 0.0)
        thr = jnp.where(small, ninf, v)
        hi = jnp.where(res > 0.0, jnp.inf, v)
    else:
        lo, top = bounds
        res = jnp.where(small, 1.0, 0.0)
        thr = stat(ninf)
        hi = jnp.where(small, jnp.inf, top + (jnp.abs(top) * 1e-6 + 1e-30))

    def bisect(_, st):
        lo, hi = st
        mid = 0.5 * (lo + hi)
        ge = count_ge(mid) >= kf
        return jnp.where(ge, mid, lo), jnp.where(ge, hi, mid)

    lo, hi = lax.fori_loop(0, n_bisect, bisect, (lo, hi))

    def snap_cond(st):
        return jnp.min(st[2]) < 1.0

    def snap(st):
        thr, hi, res = st
        v = max_below(hi)
        ok = count_ge(v) >= kf
        open_ = res < 1.0
        new = jnp.logical_and(open_, ok)
        drop = jnp.logical_and(open_, jnp.logical_not(ok))
        return jnp.where(new, v, thr), jnp.where(drop, v, hi), jnp.where(ok, 1.0, res)

    thr, hi, res = lax.while_loop(snap_cond, snap, (thr, hi, res))

    r = lax.broadcasted_iota(jnp.int32, (kc, kc), 0)
    c = lax.broadcasted_iota(jnp.int32, (kc, kc), 1)
    prefix = jnp.where((r < c) if keys_axis == 1 else (c < r), 1.0, 0.0).astype(BF16)

    def ties(s):
        tie = jnp.logical_and(s == thr, s > ninf)
        return tie, jnp.where(tie, 1.0, 0.0)

    def write_bias(j, s, tie, before, need):
        sel = jnp.logical_or(s > thr, jnp.logical_and(tie, before < need))
        sc_ref[j] = jnp.where(sel, 0.0, NEG)

    if keys_axis == 1:
        need = kf - count(thr, True)
        s = sc_ref[...]
        tie, tief = ties(s)
        rows = s.shape[1]
        within = jnp.dot(tief.reshape(nch * rows, kc).astype(BF16), prefix,
                         preferred_element_type=F32).reshape(nch, rows, kc)
        totals = jnp.sum(tief, axis=2, keepdims=True)
        run = stat(0.0)
        for j in range(nch):
            write_bias(j, s[j], tie[j], within[j] + run, need)
            run = run + totals[j]
    else:
        plain = jnp.max(count_ge(thr)) <= kf

        @pl.when(plain)
        def _():
            def all_ge(j, carry):
                sc_ref[j] = jnp.where(sc_ref[j] >= thr, 0.0, NEG)
                return carry

            lax.fori_loop(0, nch, all_ge, 0)

        @pl.when(jnp.logical_not(plain))
        def _():
            need = kf - count(thr, True)

            def to_bias(j, run):
                s = sc_ref[j]
                tie, tief = ties(s)
                before = jnp.dot(prefix, tief.astype(BF16), preferred_element_type=F32) + run
                write_bias(j, s, tie, before, need)
                return run + jnp.sum(tief, axis=0, keepdims=True)

            lax.fori_loop(0, nch, to_bias, stat(0.0))


def _dsa_kernel(qi_ref, w_ref, kid_ref, qa_ref, ka_ref, va_ref, o_ref, sc_ref, *, ksel):
    tq, kc = qi_ref.shape[-1], va_ref.shape[-1]
    it = pl.program_id(1)
    top = lax.broadcasted_iota(jnp.int32, (LANES, tq), 0) < HEAD_DIM
    key = lax.broadcasted_iota(jnp.int32, (kc, tq), 0)
    qry = lax.broadcasted_iota(jnp.int32, (kc, tq), 1)

    qm = []
    for p in range(N_IDX_HEADS // 2):
        qm.extend(_head_halves(qi_ref[0, p, 0], top))
    w = w_ref[0]

    def score_chunk(j, carry, diag_off):
        off = pl.multiple_of(j * kc, kc)
        keys = kid_ref[0, pl.ds(off, kc), :]
        sc = jnp.zeros((kc, tq), F32)
        for h in range(N_IDX_HEADS):
            lg = jnp.dot(keys, qm[h], preferred_element_type=F32)
            sc = sc + jnp.maximum(lg, 0.0) * w[h:h + 1, :]
        lo = jnp.minimum(carry[0], jnp.min(sc, axis=0, keepdims=True))
        if diag_off is not None:
            sc = jnp.where(key + diag_off <= qry, sc, -jnp.inf)
        sc_ref[j] = sc
        return lo, jnp.maximum(carry[1], jnp.max(sc, axis=0, keepdims=True))

    bounds = _causal_loop(it, tq, kc, score_chunk,
                          (jnp.full((1, tq), jnp.inf, F32), jnp.full((1, tq), -jnp.inf, F32)))
    nch = (it + 1) * (tq // kc)
    nvalid = (it * tq + 1 + lax.broadcasted_iota(jnp.int32, (1, tq), 1)).astype(F32)
    _topk_to_bias(sc_ref, nch, nvalid, ksel, keys_axis=0, bounds=bounds)

    qs = [_head_halves(qa_ref[0, p, 0], top) for p in range(_PAIRS)]

    def step(j, carries):
        off = pl.multiple_of(j * kc, kc)
        bias = sc_ref[j]
        return tuple(
            _pair_step_t(qs[p][0], qs[p][1], ka_ref[0, pl.ds(off, kc), p * LANES:(p + 1) * LANES], va_ref[0, p, j],
                         bias, bias, carries[p], top)
            for p in range(_PAIRS))

    carries = lax.fori_loop(0, nch, step, tuple(_pair_init_t(tq) for _ in range(_PAIRS)))
    for p in range(_PAIRS):
        o_ref[0, p, 0] = _pair_finish_t(carries[p], top).astype(o_ref.dtype)


def _dsa_prompt(qi_t, w_t, kid, qa_t, ka, va_t, ksel):
    bsz, _, nq, _, tq = qa_t.shape
    nk, kc = va_t.shape[2], va_t.shape[4]
    t = ka.shape[1]
    q_spec = pl.BlockSpec((1, _PAIRS, 1, LANES, tq), lambda b, i: (b, 0, i, 0, 0))
    return pl.pallas_call(
        functools.partial(_dsa_kernel, ksel=ksel),
        grid=(bsz, nq),
        in_specs=[q_spec,
                  pl.BlockSpec((1, N_IDX_HEADS, tq), lambda b, i: (b, 0, i)),
                  pl.BlockSpec((1, t, LANES), lambda b, i: (b, 0, 0)),
                  q_spec,
                  pl.BlockSpec((1, t, _GW), lambda b, i: (b, 0, 0)),
                  pl.BlockSpec((1, _PAIRS, nk, LANES, kc), lambda b, i: (b, 0, 0, 0, 0))],
        out_specs=q_spec,
        out_shape=jax.ShapeDtypeStruct(qa_t.shape, BF16),
        scratch_shapes=[pltpu.VMEM((nk, kc, tq), F32)],
        compiler_params=_cparams(2),
        name="dsa_prompt",
    )(qi_t, w_t, kid, qa_t, ka, va_t)


def _sample_idx_kernel(pt_ref, qi_ref, w_ref, knew_ref, *rest, pages_per_step):
    page_refs = rest[:pages_per_step]
    o_ref = rest[pages_per_step]
    s = pl.program_id(1)
    nsteps = pl.num_programs(1)
    npg, tn = o_ref.shape[1] - 1, o_ref.shape[2]
    qi = qi_ref[0]
    w = w_ref[0]

    def scores(keys_t):
        lg = jnp.dot(qi, keys_t, preferred_element_type=F32)
        z = jnp.maximum(lg, 0.0) * w
        out = z[0:tn]
        for h in range(1, N_IDX_HEADS):
            out = out + z[h * tn:(h + 1) * tn]
        return out

    for i, pr in enumerate(page_refs):
        o_ref[0, s * pages_per_step + i] = scores(pr[0].astype(BF16))

    @pl.when(s == nsteps - 1)
    def _():
        r = lax.broadcasted_iota(jnp.int32, (tn, PAGE), 0)
        c = lax.broadcasted_iota(jnp.int32, (tn, PAGE), 1)
        o_ref[0, npg] = jnp.where(c <= r, scores(knew_ref[0]), -jnp.inf)


def _sample_indexer(page_table, qi_rows, w_rows, ki_new_pad, kidx_pool, pages_per_step):
    db, npg = page_table.shape
    tn = qi_rows.shape[1] // N_IDX_HEADS
    nsteps = npg // pages_per_step

    def page_spec(i):
        return pl.BlockSpec((1, D_IDX, PAGE), lambda b, s, pt: (pt[b, s * pages_per_step + i], 0, 0))

    grid_spec = pltpu.PrefetchScalarGridSpec(
        num_scalar_prefetch=1,
        grid=(db, nsteps),
        in_specs=[pl.BlockSpec((1,) + qi_rows.shape[1:], lambda b, s, pt: (b, 0, 0)),
                  pl.BlockSpec((1,) + w_rows.shape[1:], lambda b, s, pt: (b, 0, 0)),
                  pl.BlockSpec((1, D_IDX, PAGE), lambda b, s, pt: (b, 0, 0))]
                 + [page_spec(i) for i in range(pages_per_step)],
        out_specs=pl.BlockSpec((1, npg + 1, tn, PAGE), lambda b, s, pt: (b, 0, 0, 0)),
    )
    return pl.pallas_call(
        functools.partial(_sample_idx_kernel, pages_per_step=pages_per_step),
        grid_spec=grid_spec,
        out_shape=jax.ShapeDtypeStruct((db, npg + 1, tn, PAGE), F32),
        compiler_params=_cparams(2),
        name="sample_indexer",
    )(page_table, qi_rows, w_rows, ki_new_pad, *([kidx_pool] * pages_per_step))


def _sample_topk_kernel(s_ref, o_ref, *, ksel, tn, past):
    rows = s_ref.shape[1]
    o_ref[...] = s_ref[...]
    q = lax.broadcasted_iota(jnp.int32, (rows, 1), 0) & (tn - 1)
    _topk_to_bias(o_ref, o_ref.shape[0], (past + 1 + q).astype(F32), ksel, keys_axis=1)


def _sample_topk(scores, ksel, row_block):
    db, nch, tn, _ = scores.shape
    rows = db * tn
    rb = min(rows, row_block)
    by_chunk = scores.transpose(1, 0, 2, 3).reshape(nch, rows, PAGE)
    bias = pl.pallas_call(
        functools.partial(_sample_topk_kernel, ksel=ksel, tn=tn, past=(nch - 1) * PAGE),
        grid=(rows // rb,),
        in_specs=[pl.BlockSpec((nch, rb, PAGE), lambda i: (0, i, 0))],
        out_specs=pl.BlockSpec((nch, rb, PAGE), lambda i: (0, i, 0)),
        out_shape=jax.ShapeDtypeStruct((nch, rows, PAGE), F32),
        compiler_params=_cparams(1),
        name="sample_topk",
    )(by_chunk)
    return bias.reshape(nch, db, tn, PAGE).transpose(1, 0, 2, 3)


def _sample_attn_kernel(pt_ref, qa_ref, qb_ref, bias_ref, biasn_ref, lfn_ref,
                        kan_ref, van_ref, kbn_ref, vbn_ref, *rest, pps):
    lft_refs, ka_refs, va_refs, kb_refs, vb_refs = (rest[i * pps:(i + 1) * pps] for i in range(5))
    oa_ref, ob_ref, ma, la, acca, mb, lb, accb, run = rest[5 * pps:]
    p = pl.program_id(1)
    nsteps = pl.num_programs(1)
    rows = qa_ref.shape[1]
    tn = rows // N_HEADS

    @pl.when(p == 0)
    def _():
        for m in (ma, mb):
            m[...] = jnp.full(m.shape, -jnp.inf, F32)
        for z in (la, lb, acca, accb, run):
            z[...] = jnp.zeros(z.shape, F32)

    r = lax.broadcasted_iota(jnp.int32, (PAGE, PAGE), 0)
    c = lax.broadcasted_iota(jnp.int32, (PAGE, PAGE), 1)
    incl = jnp.where(r <= c, 1.0, 0.0).astype(BF16)

    def update(q, kts, vts, biases, m_ref, l_ref, acc_ref):
        s = [jnp.dot(q, kt, preferred_element_type=F32) + b for kt, b in zip(kts, biases)]
        smax = functools.reduce(jnp.maximum, s)
        m_old = m_ref[...]
        m_new = jnp.maximum(m_old, jnp.max(smax, axis=1, keepdims=True))
        a = jnp.exp(m_old - m_new)
        ps = [jnp.exp(x - m_new) for x in s]
        l_ref[...] = a * l_ref[...] + jnp.sum(functools.reduce(jnp.add, ps), axis=1, keepdims=True)
        pv = [lax.dot_general(pr.astype(BF16), vt, _NT, preferred_element_type=F32) for pr, vt in zip(ps, vts)]
        acc_ref[...] = a * acc_ref[...] + functools.reduce(jnp.add, pv)
        m_ref[...] = m_new

    def head_rows(x):
        return jnp.concatenate([jnp.broadcast_to(x[h:h + 1], (tn, x.shape[1])) for h in range(N_HEADS)], axis=0)

    def tile_rows(x):
        return jnp.concatenate([x] * N_HEADS, axis=0)

    def forget_biases(lf_ts):
        out, run_v = [], run[...]
        for lf_t in lf_ts:
            cum = _dot3(head_rows(lf_t), incl, left=False) + run_v
            run_v = cum[:, PAGE - 1:PAGE]
            out.append(-cum)
        run[...] = run_v
        return out

    cast = lambda refs: [x[0].astype(BF16) for x in refs]
    update(qa_ref[0], cast(ka_refs), cast(va_refs), [tile_rows(bias_ref[0, i]) for i in range(pps)], ma, la, acca)
    update(qb_ref[0], cast(kb_refs), cast(vb_refs), forget_biases([x[0] for x in lft_refs]), mb, lb, accb)

    @pl.when(p == nsteps - 1)
    def _():
        update(qa_ref[0], [kan_ref[0]], [van_ref[0]], [tile_rows(biasn_ref[0, 0])], ma, la, acca)
        qrow = lax.broadcasted_iota(jnp.int32, (rows, PAGE), 0) & (tn - 1)
        key = lax.broadcasted_iota(jnp.int32, (rows, PAGE), 1)
        fb = jnp.where(key <= qrow, forget_biases([lfn_ref[0]])[0], NEG)
        update(qb_ref[0], [kbn_ref[0]], [vbn_ref[0]], [fb], mb, lb, accb)
        lane = lax.broadcasted_iota(jnp.int32, (tn, _GW), 1)
        for l_ref, acc_ref, o_ref in ((la, acca, oa_ref), (lb, accb, ob_ref)):
            full = acc_ref[...] / l_ref[...]
            out = jnp.zeros((tn, _GW), F32)
            for h in range(N_HEADS):
                mine = jnp.logical_and(lane >= h * HEAD_DIM, lane < (h + 1) * HEAD_DIM)
                out = jnp.where(mine, full[h * tn:(h + 1) * tn], out)
            o_ref[0] = out.astype(o_ref.dtype)


def _sample_attn(page_table, qa_bd, qb_bd, bias, lft_pool, lfn, kan, van, kbn, vbn,
                 ka_pool, va_pool, kb_pool, vb_pool, pps):
    db, npg = page_table.shape
    rows = qa_bd.shape[1]
    tn = rows // N_HEADS
    per_b = lambda shp: pl.BlockSpec((1,) + shp, lambda b, p, pt: (b,) + (0,) * len(shp))

    def paged(shp):
        return [pl.BlockSpec((1,) + shp, lambda b, p, pt, i=i: (pt[b, p * pps + i],) + (0,) * len(shp))
                for i in range(pps)]

    grid_spec = pltpu.PrefetchScalarGridSpec(
        num_scalar_prefetch=1,
        grid=(db, npg // pps),
        in_specs=[per_b((rows, _GW)), per_b((rows, _GW)),
                  pl.BlockSpec((1, pps, tn, PAGE), lambda b, p, pt: (b, p, 0, 0)),
                  pl.BlockSpec((1, 1, tn, PAGE), lambda b, p, pt: (b, npg, 0, 0)),
                  per_b((N_HEADS, PAGE)),
                  per_b((_GW, PAGE)), per_b((_GW, PAGE)), per_b((_GW, PAGE)), per_b((_GW, PAGE))]
                 + paged((N_HEADS, PAGE)) + paged((_GW, PAGE)) + paged((_GW, PAGE))
                 + paged((_GW, PAGE)) + paged((_GW, PAGE)),
        out_specs=(per_b((tn, _GW)), per_b((tn, _GW))),
        scratch_shapes=[pltpu.VMEM((rows, 1), F32), pltpu.VMEM((rows, 1), F32), pltpu.VMEM((rows, _GW), F32),
                        pltpu.VMEM((rows, 1), F32), pltpu.VMEM((rows, 1), F32), pltpu.VMEM((rows, _GW), F32),
                        pltpu.VMEM((rows, 1), F32)],
    )
    o = jax.ShapeDtypeStruct((db, tn, _GW), BF16)
    rep = lambda a: [a] * pps
    return pl.pallas_call(
        functools.partial(_sample_attn_kernel, pps=pps),
        grid_spec=grid_spec,
        out_shape=(o, o),
        compiler_params=_cparams(2),
        name="sample_attn",
    )(page_table, qa_bd, qb_bd, bias, bias, lfn, kan, van, kbn, vbn,
      *rep(lft_pool), *rep(ka_pool), *rep(va_pool), *rep(kb_pool), *rep(vb_pool))


def _out_ln_kernel(a_ref, b_ref, x_ref, g_ref, wa_ref, wb_ref, lg_ref, lb_ref, o_ref, *, alpha):
    def times(ref, w):
        if len(ref.shape) == 2:
            return jnp.dot(ref[...], w, preferred_element_type=F32)
        a_t = ref[0, :, 0].reshape(w.shape[0], ref.shape[-1])
        return lax.dot_general(a_t, w, (((0,), (0,)), ((), ())), preferred_element_type=F32)

    mix = times(a_ref, wa_ref[...]) + times(b_ref, wb_ref[...])
    z = alpha * x_ref[...] + (1.0 + g_ref[0]) * mix
    o_ref[...] = _ln(z, lg_ref[...], lb_ref[...])


def _out_ln(a, b, x, gate, w_out, ln_g, ln_b, alpha, tm):
    rows, d = x.shape
    nt = rows // tm
    if a.ndim == 5:
        ka, tpb = a.shape[1] * a.shape[3], a.shape[2]
        attn_spec = lambda arr: pl.BlockSpec((1,) + arr.shape[1:2] + (1,) + arr.shape[3:],
                                             lambda i: (i // tpb, 0, i % tpb, 0, 0))
    else:
        ka = a.shape[1]
        attn_spec = lambda arr: pl.BlockSpec((tm, arr.shape[1]), lambda i: (i, 0))
    wa, wb = w_out[:ka].astype(BF16), w_out[ka:].astype(BF16)
    const = _const_spec
    lg, lb = ln_g.reshape(1, d), ln_b.reshape(1, d)
    return pl.pallas_call(
        functools.partial(_out_ln_kernel, alpha=alpha),
        grid=(nt,),
        in_specs=[attn_spec(a), attn_spec(b),
                  pl.BlockSpec((tm, d), lambda i: (i, 0)),
                  _mod_spec(gate, tm, nt),
                  const(wa), const(wb), const(lg), const(lb)],
        out_specs=pl.BlockSpec((tm, d), lambda i: (i, 0)),
        out_shape=jax.ShapeDtypeStruct((rows, d), F32),
        compiler_params=_cparams(1),
        name="out_ln",
    )(a, b, x, gate, wa, wb, lg, lb)


_FF_CW = 2816


def _ffn_kernel(x_ref, sc_ref, sh_ref, g_ref, prev_ref, wg_ref, wv_ref, cw_ref, wd_ref, lg_ref, lb_ref,
                o_ref, up_ref, bufg, bufv, carry, *, alpha, nseq, tiles_per_seq):
    tm, d = x_ref.shape
    ts = tm // nseq
    nch, _, cw = wg_ref.shape
    dff = nch * cw
    x = x_ref[...]
    h = (x * (1.0 + sc_ref[0]) + sh_ref[0]).astype(BF16)
    chained = tiles_per_seq > 1

    if chained:
        @pl.when(pl.program_id(0) % tiles_per_seq == 0)
        def _():
            carry[...] = prev_ref[0]

    acc = jnp.zeros((tm, d), F32)
    for c in range(nch):
        cwr = cw_ref[c]
        ys = []
        for half, (w_ref, buf) in enumerate(((wg_ref, bufg), (wv_ref, bufv))):
            lo = half * dff + c * cw
            u = jnp.dot(h, w_ref[c], preferred_element_type=F32)
            if chained:
                buf[0, 0:SUBLANES, :] = carry[:, lo:lo + cw]
                carry[:, lo:lo + cw] = u[tm - SUBLANES:, :]
                up_ref[:, lo:lo + cw] = u[tm - SUBLANES:, :]
            else:
                buf[:, 0:SUBLANES, :] = prev_ref[:, :, lo:lo + cw]
                up_ref[:, lo:lo + cw] = u
            buf[:, SUBLANES:SUBLANES + ts, :] = u.reshape(nseq, ts, cw)
            um1 = buf[:, SUBLANES - 1:SUBLANES - 1 + ts, :].reshape(tm, cw)
            um2 = buf[:, SUBLANES - 2:SUBLANES - 2 + ts, :].reshape(tm, cw)
            k0 = 4 * half
            y = cwr[k0 + 3:k0 + 4, :] + cwr[k0:k0 + 1, :] * um2
            y = y + cwr[k0 + 1:k0 + 2, :] * um1
            ys.append(y + cwr[k0 + 2:k0 + 3, :] * u)
        act = (_gelu(ys[0]) * ys[1]).astype(BF16)
        acc = acc + jnp.dot(act, wd_ref[c], preferred_element_type=F32)

    z = alpha * x + (1.0 + g_ref[0]) * acc
    o_ref[...] = _ln(z, lg_ref[...], lb_ref[...])


def _conv_ffn(x, sc, sh, gate, prev8, w_up, w_conv, b_conv, w_down, ln_g, ln_b, alpha, tm, seq_len):
    rows, d = x.shape
    dff = w_down.shape[0]
    nch = dff // _FF_CW
    nt = rows // tm
    if seq_len >= tm:
        nseq, tiles_per_seq = 1, seq_len // tm
        prev_spec = pl.BlockSpec((1, SUBLANES, 2 * dff), lambda i: (i // tiles_per_seq, 0, 0))
        up_rows, up_spec = nt * SUBLANES, pl.BlockSpec((SUBLANES, 2 * dff), lambda i: (i, 0))
    else:
        nseq, tiles_per_seq = tm // seq_len, 1
        prev_spec = pl.BlockSpec((nseq, SUBLANES, 2 * dff), lambda i: (i, 0, 0))
        up_rows, up_spec = rows, pl.BlockSpec((tm, 2 * dff), lambda i: (i, 0))
    ts = tm // nseq
    chunked = lambda w: w.reshape(w.shape[0], nch, _FF_CW).transpose(1, 0, 2)
    wg = chunked(w_up[:, :dff]).astype(BF16)
    wv = chunked(w_up[:, dff:]).astype(BF16)
    wd = w_down.reshape(nch, _FF_CW, d).astype(BF16)
    conv_rows = jnp.concatenate([w_conv[:, :dff], b_conv[None, :dff], w_conv[:, dff:], b_conv[None, dff:]], axis=0)
    cwr = chunked(conv_rows)
    lg, lb = ln_g.reshape(1, d), ln_b.reshape(1, d)
    const = _const_spec
    mod_spec = lambda m: _mod_spec(m, tm, nt)
    return pl.pallas_call(
        functools.partial(_ffn_kernel, alpha=alpha, nseq=nseq, tiles_per_seq=tiles_per_seq),
        grid=(nt,),
        in_specs=[pl.BlockSpec((tm, d), lambda i: (i, 0)), mod_spec(sc), mod_spec(sh), mod_spec(gate), prev_spec,
                  const(wg), const(wv), const(cwr), const(wd), const(lg), const(lb)],
        out_specs=(pl.BlockSpec((tm, d), lambda i: (i, 0)), up_spec),
        out_shape=(jax.ShapeDtypeStruct((rows, d), F32), jax.ShapeDtypeStruct((up_rows, 2 * dff), F32)),
        scratch_shapes=[pltpu.VMEM((nseq, ts + SUBLANES, _FF_CW), F32),
                        pltpu.VMEM((nseq, ts + SUBLANES, _FF_CW), F32),
                        pltpu.VMEM((SUBLANES, 2 * dff), F32)],
        compiler_params=_cparams(1),
        name="conv_ffn",
    )(x, sc, sh, gate, prev8, wg, wv, cwr, wd, lg, lb)


def _gmlp_kernel(x_ref, sc_ref, sh_ref, g_ref, wv_ref, wu_ref, lvg_ref, lvb_ref, wmix_ref, bs_ref, wo_ref,
                 lg_ref, lb_ref, o_ref, *maybe_v_ref, alpha):
    ng = wmix_ref.shape[0]
    gw = wu_ref.shape[1] // ng
    x = x_ref[...]
    h = (x * (1.0 + sc_ref[0]) + sh_ref[0]).astype(BF16)
    v = _ln(_gelu(jnp.dot(h, wv_ref[...], preferred_element_type=F32)), lvg_ref[...], lvb_ref[...])
    if maybe_v_ref:
        maybe_v_ref[0][...] = v
    vb = v.astype(BF16)
    bs = bs_ref[...]
    u = _gelu(jnp.dot(h, wu_ref[...], preferred_element_type=F32))
    mixed = jnp.concatenate(
        [jnp.dot(wmix_ref[g], vb[:, g * gw:(g + 1) * gw], preferred_element_type=F32) + bs[:, g:g + 1]
         for g in range(ng)], axis=1)
    acc = jnp.dot((u * mixed).astype(BF16), wo_ref[...], preferred_element_type=F32)
    z = alpha * x + (1.0 + g_ref[0]) * acc
    o_ref[...] = _ln(z, lg_ref[...], lb_ref[...])


def _chunk_gmlp(x, sc, sh, gate, w_in, lnv_g, lnv_b, w_spatial, b_spatial, w_out, ln_g, ln_b,
                alpha, tm, chunk_len, emit_v):
    rows, d = x.shape
    dcg = w_out.shape[0]
    ng = w_spatial.shape[0]
    gw = dcg // ng
    nt = rows // tm
    wu = w_in[:, :dcg].astype(BF16)
    wv = w_in[:, dcg:].astype(BF16)
    wo = w_out.astype(BF16)
    tri = (jnp.arange(chunk_len)[:, None] >= jnp.arange(chunk_len)[None, :]).astype(w_spatial.dtype)
    wc = w_spatial[:, :chunk_len, :chunk_len] * tri[None]
    eye = jnp.eye(tm // chunk_len, dtype=w_spatial.dtype)
    wmix = jnp.einsum("ab,gts->gatbs", eye, wc).reshape(ng, tm, tm).astype(BF16)
    bs = jnp.tile(b_spatial[:, :chunk_len].T, (tm // chunk_len, 1))
    bs = jnp.pad(bs, ((0, 0), (0, LANES - ng)))
    lvg, lvb = lnv_g.reshape(1, dcg), lnv_b.reshape(1, dcg)
    lg, lb = ln_g.reshape(1, d), ln_b.reshape(1, d)
    const = _const_spec
    mod_spec = lambda m: _mod_spec(m, tm, nt)
    row_spec = lambda w: pl.BlockSpec((tm, w), lambda i: (i, 0))
    out_specs, out_shape = [row_spec(d)], [jax.ShapeDtypeStruct((rows, d), F32)]
    if emit_v:
        out_specs.append(row_spec(dcg))
        out_shape.append(jax.ShapeDtypeStruct((rows, dcg), F32))
    return pl.pallas_call(
        functools.partial(_gmlp_kernel, alpha=alpha),
        grid=(nt,),
        in_specs=[row_spec(d), mod_spec(sc), mod_spec(sh), mod_spec(gate), const(wv), const(wu), const(lvg),
                  const(lvb), const(wmix), const(bs), const(wo), const(lg), const(lb)],
        out_specs=tuple(out_specs),
        out_shape=tuple(out_shape),
        compiler_params=_cparams(1),
        name="chunk_gmlp",
    )(x, sc, sh, gate, wv, wu, lvg, lvb, wmix, bs, wo, lg, lb)


def kernel(x_prompt, x_sample, cache_a_k, cache_a_v, cache_a_kidx, cache_b_k, cache_b_v, cache_b_logf,
           state_ffn_conv, page_table, c_prompt, c_sample, w_mod, b_mod, ln1_g, ln1_b, ln2_g, ln2_b,
           w_in_ab, b_forget, w_out_ab, w_in_c, lnv_g, lnv_b, w_spatial, b_spatial, w_out_c,
           w_up, w_conv, b_conv, w_down):
    bsz, t, d = x_prompt.shape
    db, tn, _ = x_sample.shape
    depth = w_mod.shape[0]
    dff2 = w_up.shape[2]
    npg = page_table.shape[1]
    past = npg * PAGE
    alpha = (2 * depth) ** 0.25
    rows_p, rows_s = bsz * t, db * tn
    tm_p = 256
    tm_ffn = 512
    tq = 512

    nc = bsz + db
    c_all = jnp.concatenate([c_prompt, c_sample, jnp.zeros((-nc % SUBLANES, d), F32)], axis=0)
    mod = _adaln_mod(c_all, w_mod, b_mod)

    xp = x_prompt.reshape(rows_p, d)
    xs = x_sample.reshape(rows_s, d)
    cos_p, sin_p = _rope_tables(jnp.arange(t, dtype=jnp.int32))
    cos_s, sin_s = _rope_tables(jnp.tile(past + jnp.arange(tn, dtype=jnp.int32), db))

    outs_p = {k: [] for k in ("ak", "av", "aki", "bk", "bv", "blf", "conv")}
    outs_s = {k: [] for k in ("ak", "av", "aki", "bk", "bv", "blf", "conv", "cv")}

    for i in range(depth):
        j = i // 2
        mp = mod[i, :bsz].reshape(bsz, 6, 1, d)
        ms = jnp.repeat(mod[i, bsz:nc].reshape(db, 6, d), tn, axis=0).reshape(1, rows_s, 6, d)
        sh1p, sc1p, g1p, sh2p, sc2p, g2p = (mp[:, k] for k in range(6))
        sh1s, sc1s, g1s, sh2s, sc2s, g2s = (ms[:, :, k] for k in range(6))

        if i % 2 == 0:
            w_packed = _pack_w_ab(w_in_ab[j])
            bf_row = jnp.zeros((1, LANES), F32).at[0, _TAIL_F:_TAIL_F + N_HEADS].set(b_forget[j])
            w_out = w_out_ab[j]

            (qa_t, kaf_t, kab, vaf_t, va_t, qi_t, qb_t, kbf_t, kbb, vbf_t, vb_t, tail, kid) = _proj_ab(
                xp, sc1p, sh1p, cos_p, sin_p, w_packed, bf_row, tq, transposed_batch=bsz)
            r3 = lambda a: a.reshape(bsz, t, a.shape[-1])
            tail3 = r3(tail)
            ckb = _logf_cumsum(tail3, min(t, 512))
            b_out = _fox_prompt(qb_t, r3(kbb), vb_t, ckb, 2)
            w_t = tail3[:, :, _TAIL_WI:_TAIL_WI + N_IDX_HEADS].transpose(0, 2, 1)
            a_out = _dsa_prompt(qi_t, w_t, r3(kid), qa_t, r3(kab), va_t, min(TOPK, t // 4))
            xp = _out_ln(a_out, b_out, xp, g1p, w_out, ln1_g[i], ln1_b[i], alpha, tq)
            heads = lambda a: a.reshape(bsz, N_HEADS, HEAD_DIM, t).transpose(0, 3, 1, 2)
            outs_p["ak"].append(heads(kaf_t)); outs_p["av"].append(heads(vaf_t))
            outs_p["aki"].append(tail3[:, :, :D_IDX])
            outs_p["bk"].append(heads(kbf_t)); outs_p["bv"].append(heads(vbf_t))
            outs_p["blf"].append(tail3[:, :, _TAIL_F:_TAIL_F + N_HEADS])

            (qa, kaf, kab, vaf, vab, qi, qb, kbf, kbb, vbf, vbb, tail, kid) = _proj_ab(
                xs, sc1s, sh1s, cos_s, sin_s, w_packed, bf_row, rows_s)
            s3 = lambda a: a.reshape(db, tn, a.shape[-1])
            tail3 = s3(tail)
            hq = lambda a: a.reshape(db, tn, N_HEADS, HEAD_DIM).transpose(0, 2, 1, 3)
            qi_rows = hq(qi).reshape(db, N_IDX_HEADS * tn, D_IDX)
            w_hq = tail3[:, :, _TAIL_WI:_TAIL_WI + N_IDX_HEADS].transpose(0, 2, 1).reshape(db, N_IDX_HEADS * tn, 1)
            w_rows = jnp.broadcast_to(w_hq, (db, N_IDX_HEADS * tn, LANES))
            keys_t = lambda a: jnp.pad(a.transpose(0, 2, 1), ((0, 0), (0, 0), (0, PAGE - tn)))
            ki_new = keys_t(s3(kid)[:, :, :D_IDX])
            scores = _sample_indexer(page_table, qi_rows, w_rows, ki_new, cache_a_kidx[j].transpose(0, 2, 1),
                                     pages_per_step=min(npg, 64))
            bias = _sample_topk(scores, min(TOPK, (past + tn) // 4), row_block=128)
            eye = jnp.eye(N_HEADS, dtype=BF16)
            bd = lambda a: jnp.einsum("bhqd,hg->bhqgd", hq(a), eye).reshape(db, N_HEADS * tn, _GW)
            lft_pool = cache_b_logf[j].transpose(0, 2, 1)
            lfn = jnp.pad(tail3[:, :, _TAIL_F:_TAIL_F + N_HEADS].transpose(0, 2, 1),
                          ((0, 0), (0, 0), (0, PAGE - tn)))
            pool = lambda cch: cch[j].transpose(0, 2, 3, 1).reshape(cch.shape[1], _GW, PAGE)
            a_out, b_out = _sample_attn(page_table, bd(qa), bd(qb), bias, lft_pool, lfn,
                                        keys_t(s3(kab)), keys_t(s3(vab)), keys_t(s3(kbb)), keys_t(s3(vbb)),
                                        pool(cache_a_k), pool(cache_a_v), pool(cache_b_k), pool(cache_b_v),
                                        pps=min(npg, 16))
            xs = _out_ln(a_out.reshape(rows_s, _GW), b_out.reshape(rows_s, _GW), xs, g1s, w_out,
                         ln1_g[i], ln1_b[i], alpha, rows_s)
            heads = lambda a: a.reshape(db, tn, N_HEADS, HEAD_DIM)
            outs_s["ak"].append(heads(kaf)); outs_s["av"].append(heads(vaf))
            outs_s["aki"].append(tail3[:, :, :D_IDX])
            outs_s["bk"].append(heads(kbf)); outs_s["bv"].append(heads(vbf))
            outs_s["blf"].append(tail3[:, :, _TAIL_F:_TAIL_F + N_HEADS])
        else:
            args = (w_in_c[j], lnv_g[j], lnv_b[j], w_spatial[j], b_spatial[j], w_out_c[j], ln1_g[i], ln1_b[i], alpha)
            (xp,) = _chunk_gmlp(xp, sc1p, sh1p, g1p, *args, tm_p, CHUNK, False)
            xs, cv = _chunk_gmlp(xs, sc1s, sh1s, g1s, *args, rows_s, tn, True)
            outs_s["cv"].append(cv.reshape(db, tn, -1))

        ffn_w = (w_up[i], w_conv[i], b_conv[i], w_down[i], ln2_g[i], ln2_b[i], alpha)
        xp, up_p = _conv_ffn(xp, sc2p, sh2p, g2p, jnp.zeros((bsz, SUBLANES, dff2), F32), *ffn_w, tm_ffn, t)
        outs_p["conv"].append(up_p.reshape(bsz, t // tm_ffn, SUBLANES, dff2)[:, -1, SUBLANES - (CONV_W - 1):])
        prev_s = jnp.pad(state_ffn_conv[i], ((0, 0), (SUBLANES - (CONV_W - 1), 0), (0, 0)))
        xs, up_s = _conv_ffn(xs, sc2s, sh2s, g2s, prev_s, *ffn_w, min(rows_s, 8 * tn), tn)
        outs_s["conv"].append(up_s.reshape(db, tn, dff2)[:, tn - (CONV_W - 1):])

    st = jnp.stack
    return (xp.reshape(bsz, t, d), xs.reshape(db, tn, d),
            st(outs_p["ak"]), st(outs_p["av"]), st(outs_p["aki"]), st(outs_p["bk"]), st(outs_p["bv"]),
            st(outs_p["blf"]), st(outs_p["conv"]),
            st(outs_s["ak"]), st(outs_s["av"]), st(outs_s["aki"]), st(outs_s["bk"]), st(outs_s["bv"]),
            st(outs_s["blf"]), st(outs_s["conv"]), st(outs_s["cv"]))
```
